```python
import math
import jax
import jax.numpy as jnp
from jax import lax
import numpy as np

D_MODEL = 2048
BATCH = 4
SEQ = 4096
DEPTH = 1

GRID_W = 64
CTX_LEN = 256
EPS = 1e-6
ATT_HEADS = 16
ATT_KV_HEADS = 4
HEAD_DIM = 64
D_ATT = ATT_HEADS * HEAD_DIM
D_KV = ATT_KV_HEADS * HEAD_DIM
WINDOW = 128
BLOCK = 128
ROPE_THETA = 10000.0
SSD_HEADS = 16
SSD_HEAD_DIM = 64
D_SSD = SSD_HEADS * SSD_HEAD_DIM
SSD_GROUPS = 4
SSD_HPG = SSD_HEADS // SSD_GROUPS
D_STATE = 128
D_XBC = D_SSD + 2 * SSD_GROUPS * D_STATE
CONV_W = 3
CHUNK = 128
D_MIX = D_ATT + D_SSD
N_IN = 2 * D_ATT + 2 * D_KV + D_SSD + D_XBC + 2 * SSD_HEADS
SPLIT_IDX = (D_ATT, D_ATT + D_KV, D_ATT + 2 * D_KV, 2 * D_ATT + 2 * D_KV, 2 * D_ATT + 2 * D_KV + D_SSD, 2 * D_ATT + 2 * D_KV + D_SSD + D_XBC, 2 * D_ATT + 2 * D_KV + D_SSD + D_XBC + SSD_HEADS)

kernel_name = 'hybrid_swa_ssd_prefix_dit_layer'


def rmsnorm(x, g):
    xf = x.astype(jnp.float32)
    y = xf * lax.rsqrt(jnp.mean(xf * xf, axis=-1, keepdims=True) + EPS)
    return (y * g.astype(jnp.float32)).astype(x.dtype)


def axial_rope(t):
    n = t.shape[1]
    rows = n // GRID_W
    row = jnp.repeat(jnp.arange(rows), GRID_W).astype(jnp.float32)
    col = jnp.tile(jnp.arange(GRID_W), rows).astype(jnp.float32)
    half = HEAD_DIM // 2
    quarter = half // 2
    freq = 1.0 / (ROPE_THETA ** (jnp.arange(quarter, dtype=jnp.float32) / quarter))

    def rot(u, pos):
        ang = pos[:, None] * freq[None, :]
        cos = jnp.cos(ang)[None, :, None, :].astype(u.dtype)
        sin = jnp.sin(ang)[None, :, None, :].astype(u.dtype)
        u1, u2 = u[..., :quarter], u[..., quarter:]
        return jnp.concatenate([u1 * cos - u2 * sin, u2 * cos + u1 * sin], axis=-1)

    return jnp.concatenate([rot(t[..., :half], row), rot(t[..., half:], col)], axis=-1)


def band_blocks(t, nb):
    b, _, h, d = t.shape
    tp = jnp.pad(t, ((0, 0), (BLOCK, BLOCK), (0, 0), (0, 0))).reshape(b, nb + 2, BLOCK, h, d)
    return jnp.concatenate([tp[:, :-2], tp[:, 1:-1], tp[:, 2:]], axis=2)


def windowed_gqa(q, k, v, k_ctx, v_ctx, sink):
    b, s, h, hd = q.shape
    kvh = k.shape[2]
    grp = h // kvh
    nb = s // BLOCK
    nl = 3 * BLOCK
    nctx = k_ctx.shape[1]
    scale = hd ** -0.5
    qb = q.reshape(b, nb, BLOCK, kvh, grp, hd)
    kw, vw = band_blocks(k, nb), band_blocks(v, nb)
    s_loc = jnp.einsum('bnqkgd,bnjkd->bnkgqj', qb, kw).astype(jnp.float32) * scale
    s_ctx = jnp.einsum('bnqkgd,bckd->bnkgqc', qb, k_ctx).astype(jnp.float32) * scale
    qi = jnp.arange(BLOCK)[:, None]
    kj = jnp.arange(nl)[None, :]
    kpos = (jnp.arange(nb)[:, None, None] - 1) * BLOCK + kj
    valid = (jnp.abs(kj - BLOCK - qi) <= WINDOW) & (kpos >= 0) & (kpos < s)
    s_loc = jnp.where(valid[None, :, None, None], s_loc, -jnp.inf)
    s_sink = jnp.broadcast_to(sink.astype(jnp.float32).reshape(1, 1, kvh, grp, 1, 1), s_loc.shape[:-1] + (1,))
    p = jax.nn.softmax(jnp.concatenate([s_loc, s_ctx, s_sink], axis=-1), axis=-1).astype(v.dtype)
    o = jnp.einsum('bnkgqj,bnjkd->bnqkgd', p[..., :nl], vw) + jnp.einsum('bnkgqc,bckd->bnqkgd', p[..., nl:nl + nctx], v_ctx)
    return o.reshape(b, s, h * hd)


def ctx_attention(q, k, v, sink):
    b, n, h, hd = q.shape
    kvh = k.shape[2]
    grp = h // kvh
    qg = q.reshape(b, n, kvh, grp, hd)
    s = jnp.einsum('bqkgd,bckd->bkgqc', qg, k).astype(jnp.float32) * hd ** -0.5
    s_sink = jnp.broadcast_to(sink.astype(jnp.float32).reshape(1, kvh, grp, 1, 1), s.shape[:-1] + (1,))
    p = jax.nn.softmax(jnp.concatenate([s, s_sink], axis=-1), axis=-1)[..., :-1].astype(v.dtype)
    o = jnp.einsum('bkgqc,bckd->bqkgd', p, v)
    return o.reshape(b, n, h * hd)


def segsum(a):
    cs = jnp.cumsum(a, axis=-1)
    t = a.shape[-1]
    mask = jnp.tril(jnp.ones((t, t), dtype=bool))
    return jnp.where(mask, cs[..., :, None] - cs[..., None, :], -jnp.inf)


def ssd_scan(x, dt, a, bm, cm, h0):
    b, l, g, r, p = x.shape
    n = bm.shape[-1]
    nc = l // CHUNK
    xf = x.astype(jnp.float32).reshape(b, nc, CHUNK, g, r, p)
    dtc = dt.reshape(b, nc, CHUNK, g, r)
    bc = bm.astype(jnp.float32).reshape(b, nc, CHUNK, g, n)
    cc = cm.astype(jnp.float32).reshape(b, nc, CHUNK, g, n)
    xdt = xf * dtc[..., None]
    a_t = jnp.moveaxis(dtc * a, (1, 2), (3, 4))
    a_cs = jnp.cumsum(a_t, axis=-1)
    cb = jnp.einsum('bclgn,bcsgn->bgcls', cc, bc)
    wts = cb[:, :, None] * jnp.exp(segsum(a_t))
    y_diag = jnp.einsum('bgrcls,bcsgrp->bclgrp', wts, xdt)
    decay_states = jnp.exp(a_cs[..., -1:] - a_cs)
    states = jnp.einsum('bclgn,bgrcl,bclgrp->bcgrpn', bc, decay_states, xdt)
    states = jnp.concatenate([h0[:, None], states], axis=1)
    a_last = jnp.pad(a_cs[..., -1], ((0, 0), (0, 0), (0, 0), (1, 0)))
    decay_chunk = jnp.exp(segsum(a_last))
    new_states = jnp.einsum('bgrzc,bcgrpn->bzgrpn', decay_chunk, states)
    prev, final = new_states[:, :-1], new_states[:, -1]
    y_off = jnp.einsum('bclgn,bcgrpn,bgrcl->bclgrp', cc, prev, jnp.exp(a_cs))
    return (y_diag + y_off).reshape(b, l, g, r, p), final


def dwconv(u, w, bias):
    out = lax.conv_general_dilated(u, w[:, None, :], window_strides=(1,), padding=[(CONV_W // 2, CONV_W // 2)], dimension_numbers=('NWC', 'WIO', 'NWC'), feature_group_count=u.shape[-1])
    return out + bias


def ssd_inputs(xbc, dt_f, dt_b, conv_w, conv_b, dt_bias_f, dt_bias_b):
    b, n, _ = xbc.shape
    u = jax.nn.silu(dwconv(xbc, conv_w, conv_b))
    xs, bm, cm = jnp.split(u, [D_SSD, D_SSD + SSD_GROUPS * D_STATE], axis=-1)
    xs = xs.reshape(b, n, SSD_GROUPS, SSD_HPG, SSD_HEAD_DIM)
    bm = bm.reshape(b, n, SSD_GROUPS, D_STATE)
    cm = cm.reshape(b, n, SSD_GROUPS, D_STATE)
    dtf = jax.nn.softplus((dt_f + dt_bias_f).astype(jnp.float32)).reshape(b, n, SSD_GROUPS, SSD_HPG)
    dtb = jax.nn.softplus((dt_b + dt_bias_b).astype(jnp.float32)).reshape(b, n, SSD_GROUPS, SSD_HPG)
    return xs, bm, cm, dtf, dtb


def bi_ssd(xs, bm, cm, dtf, dtb, a_f, a_b, d_skip, h0_f, h0_b):
    flip = lambda t: jnp.flip(t, axis=1)
    y_f, h_f = ssd_scan(xs, dtf, a_f, bm, cm, h0_f)
    y_b, h_b = ssd_scan(flip(xs), flip(dtb), a_b, flip(bm), flip(cm), h0_b)
    y = y_f + flip(y_b) + xs.astype(jnp.float32) * d_skip.astype(jnp.float32).reshape(SSD_GROUPS, SSD_HPG, 1)
    b, n = xs.shape[:2]
    return y.astype(xs.dtype).reshape(b, n, D_SSD), h_f, h_b


def merge_groups(att, g_att, y, z, att_norm_g, ssd_norm_g, w_out):
    h = jnp.concatenate([rmsnorm(att * jax.nn.silu(g_att), att_norm_g), rmsnorm(y * jax.nn.silu(z), ssd_norm_g)], axis=-1)
    return h @ w_out


def hybrid_layer(x, ctx, c, c_ctx, w_mod, b_mod, norm_g, w_in, conv_w, conv_b, a_log_f, a_log_b, dt_bias_f, dt_bias_b, d_skip, att_norm_g, ssd_norm_g, sink, w_out, update_ctx):
    b, s, _ = x.shape
    nc = ctx.shape[1]
    shift, scale, gate = jnp.split((jax.nn.silu(c) @ w_mod + b_mod)[:, None, :], 3, axis=-1)
    shift_c, scale_c, gate_c = jnp.split(jax.nn.silu(c_ctx) @ w_mod + b_mod, 3, axis=-1)
    px = (rmsnorm(x, norm_g) * (1 + scale) + shift) @ w_in
    pc = (rmsnorm(ctx, norm_g) * (1 + scale_c) + shift_c) @ w_in
    qx, kx, vx, gx, zx, xbcx, dtfx, dtbx = jnp.split(px, list(SPLIT_IDX), axis=-1)
    qc, kc, vc, gc, zc, xbcc, dtfc, dtbc = jnp.split(pc, list(SPLIT_IDX), axis=-1)
    kc_h = kc.reshape(b, nc, ATT_KV_HEADS, HEAD_DIM)
    vc_h = vc.reshape(b, nc, ATT_KV_HEADS, HEAD_DIM)
    qx_h = axial_rope(qx.reshape(b, s, ATT_HEADS, HEAD_DIM))
    kx_h = axial_rope(kx.reshape(b, s, ATT_KV_HEADS, HEAD_DIM))
    vx_h = vx.reshape(b, s, ATT_KV_HEADS, HEAD_DIM)
    att_x = windowed_gqa(qx_h, kx_h, vx_h, kc_h, vc_h, sink)
    a_f = -jnp.exp(a_log_f.astype(jnp.float32)).reshape(SSD_GROUPS, SSD_HPG)
    a_b = -jnp.exp(a_log_b.astype(jnp.float32)).reshape(SSD_GROUPS, SSD_HPG)
    xs_c, bm_c, cm_c, dtf_c, dtb_c = ssd_inputs(xbcc, dtfc, dtbc, conv_w, conv_b, dt_bias_f, dt_bias_b)
    h0 = jnp.zeros((b, SSD_GROUPS, SSD_HPG, SSD_HEAD_DIM, D_STATE), jnp.float32)
    y_c, h_cf, h_cb = bi_ssd(xs_c, bm_c, cm_c, dtf_c, dtb_c, a_f, a_b, d_skip, h0, h0)
    xs_x, bm_x, cm_x, dtf_x, dtb_x = ssd_inputs(xbcx, dtfx, dtbx, conv_w, conv_b, dt_bias_f, dt_bias_b)
    y_x, _, _ = bi_ssd(xs_x, bm_x, cm_x, dtf_x, dtb_x, a_f, a_b, d_skip, h_cf, h_cb)
    x = x + gate * merge_groups(att_x, gx, y_x, zx, att_norm_g, ssd_norm_g, w_out)
    if update_ctx:
        att_c = ctx_attention(qc.reshape(b, nc, ATT_HEADS, HEAD_DIM), kc_h, vc_h, sink)
        ctx = ctx + gate_c * merge_groups(att_c, gc, y_c, zc, att_norm_g, ssd_norm_g, w_out)
    return x, ctx


def setup_inputs(seed: int = 0) -> dict:
    key = jax.random.key(seed)
    ks = jax.random.split(key, 20)
    f32 = jnp.float32
    nrm = lambda k, shape, sc: jax.random.normal(k, shape, f32) * sc
    dt0 = jnp.exp(jax.random.uniform(ks[10], (2, DEPTH, SSD_HEADS), f32, math.log(1e-3), math.log(1e-1)))
    dt_bias = dt0 + jnp.log(-jnp.expm1(-dt0))
    a_log = jnp.log(jax.random.uniform(ks[11], (2, DEPTH, SSD_HEADS), f32, 1.0, 16.0))
    return {
        'x': nrm(ks[0], (BATCH, SEQ, D_MODEL), 1.0),
        'c': nrm(ks[1], (BATCH, D_MODEL), 1.0),
        'ctx': nrm(ks[2], (BATCH, CTX_LEN, D_MODEL), 1.0),
        'c_ctx': nrm(ks[3], (D_MODEL,), 1.0),
        'w_mod': nrm(ks[4], (DEPTH, D_MODEL, 3 * D_MODEL), 0.5 * D_MODEL ** -0.5),
        'b_mod': nrm(ks[5], (DEPTH, 3 * D_MODEL), 0.01),
        'norm_g': 1.0 + nrm(ks[6], (DEPTH, D_MODEL), 0.02),
        'w_in': nrm(ks[7], (DEPTH, D_MODEL, N_IN), D_MODEL ** -0.5),
        'conv_w': nrm(ks[8], (DEPTH, CONV_W, D_XBC), CONV_W ** -0.5),
        'conv_b': nrm(ks[9], (DEPTH, D_XBC), 0.01),
        'a_log_f': a_log[0],
        'a_log_b': a_log[1],
        'dt_bias_f': dt_bias[0],
        'dt_bias_b': dt_bias[1],
        'd_skip': 1.0 + nrm(ks[12], (DEPTH, SSD_HEADS), 0.02),
        'att_norm_g': 1.0 + nrm(ks[13], (DEPTH, D_ATT), 0.02),
        'ssd_norm_g': 1.0 + nrm(ks[14], (DEPTH, D_SSD), 0.02),
        'sink': nrm(ks[15], (DEPTH, ATT_HEADS), 0.5),
        'w_out': nrm(ks[16], (DEPTH, D_MIX, D_MODEL), D_MIX ** -0.5),
        'final_norm_g': 1.0 + nrm(ks[17], (D_MODEL,), 0.02),
    }


def reference(x, c, ctx, c_ctx, w_mod, b_mod, norm_g, w_in, conv_w, conv_b, a_log_f, a_log_b, dt_bias_f, dt_bias_b, d_skip, att_norm_g, ssd_norm_g, sink, w_out, final_norm_g):
    for l in range(DEPTH):
        x, ctx = hybrid_layer(x, ctx, c, c_ctx, w_mod[l], b_mod[l], norm_g[l], w_in[l], conv_w[l], conv_b[l], a_log_f[l], a_log_b[l], dt_bias_f[l], dt_bias_b[l], d_skip[l], att_norm_g[l], ssd_norm_g[l], sink[l], w_out[l], l < DEPTH - 1)
    return rmsnorm(x, final_norm_g)
```

```python
import functools

import jax
import jax.numpy as jnp
from jax import lax
from jax.experimental import pallas as pl
from jax.experimental.pallas import tpu as pltpu

F32 = jnp.float32
BF16 = jnp.bfloat16

EPS = 1e-6
GRID_W = 64
ROPE_THETA = 10000.0
ATT_HEADS = 16
ATT_KV_HEADS = 4
HEAD_DIM = 64
D_ATT = ATT_HEADS * HEAD_DIM
D_KV = ATT_KV_HEADS * HEAD_DIM
Q_PER_KV = ATT_HEADS // ATT_KV_HEADS
BLOCK = 128
SSD_HEADS = 16
SSD_HEAD_DIM = 64
D_SSD = SSD_HEADS * SSD_HEAD_DIM
SSD_GROUPS = 4
SSD_HPG = SSD_HEADS // SSD_GROUPS
D_STATE = 128
D_BC = SSD_GROUPS * D_STATE
D_XBC = D_SSD + 2 * D_BC
CONV_W = 3
CHUNK = 128
D_QKV = D_ATT + 2 * D_KV
D_GZX = D_ATT + D_SSD + D_XBC

LANE = 128
SUBLANE = 8
V7X_VMEM_BYTES = 64 * 1024 * 1024
MIB = 1024 * 1024

NEG = -1e30


def _silu(v):
    return v / (1.0 + jnp.exp(-v))


def _params(semantics, vmem_mib):
    return pltpu.CompilerParams(dimension_semantics=semantics, vmem_limit_bytes=vmem_mib * MIB)


def _row_tile(m, candidates=(1024, 512, 256, 128)):
    for t in candidates:
        if m % t == 0:
            return t
    raise ValueError(f"row count {m} not tileable")


def _mod_kernel(c_ref, w_ref, b_ref, o_ref):
    a = _silu(c_ref[...]).astype(BF16)
    o_ref[...] = jnp.dot(a, w_ref[...].astype(BF16), preferred_element_type=F32) + b_ref[...]


def _modulation(cc, w_mod, b_mod):
    rows, d = cc.shape
    n = w_mod.shape[1]
    tn = 768
    assert n % tn == 0
    return pl.pallas_call(
        _mod_kernel,
        grid=(n // tn,),
        in_specs=[pl.BlockSpec((rows, d), lambda j: (0, 0)),
                  pl.BlockSpec((d, tn), lambda j: (0, j)),
                  pl.BlockSpec((1, tn), lambda j: (0, j))],
        out_specs=pl.BlockSpec((rows, tn), lambda j: (0, j)),
        out_shape=jax.ShapeDtypeStruct((rows, n), F32),
        compiler_params=_params(("parallel",), 32),
        name="modulation",
    )(cc, w_mod, b_mod)


def _prenorm_kernel(x_ref, ctx_ref, shift_ref, scale_ref, g_ref, wdt_ref, xn_ref, dt_ref, *, n_lat):
    j = pl.program_id(1)

    def emit(v):
        ms = jnp.mean(v * v, axis=-1, keepdims=True)
        y = v * lax.rsqrt(ms + EPS) * g_ref[...]
        y = (y * (1.0 + scale_ref[0]) + shift_ref[0]).astype(BF16)
        xn_ref[0] = y
        dt_ref[0] = jnp.dot(y, wdt_ref[...], preferred_element_type=F32)

    @pl.when(j < n_lat)
    def _():
        emit(x_ref[0])

    @pl.when(j >= n_lat)
    def _():
        emit(ctx_ref[0])


def _prenorm(x, ctx, shift, scale, norm_g, w_dt):
    b, s, d = x.shape
    nc = ctx.shape[1]
    tr = 256
    assert s % tr == 0 and nc % tr == 0
    n_lat, n_ctx = s // tr, nc // tr
    t_all = s + nc
    mod_idx = lambda bi, j: (jnp.where(j < n_lat, bi, b), 0, 0)
    return pl.pallas_call(
        functools.partial(_prenorm_kernel, n_lat=n_lat),
        grid=(b, n_lat + n_ctx),
        in_specs=[pl.BlockSpec((1, tr, d), lambda bi, j: (bi, jnp.minimum(j, n_lat - 1), 0)),
                  pl.BlockSpec((1, tr, d), lambda bi, j: (bi, jnp.maximum(j - n_lat, 0), 0)),
                  pl.BlockSpec((1, 1, d), mod_idx),
                  pl.BlockSpec((1, 1, d), mod_idx),
                  pl.BlockSpec((1, d), lambda bi, j: (0, 0)),
                  pl.BlockSpec((d, LANE), lambda bi, j: (0, 0))],
        out_specs=[pl.BlockSpec((1, tr, d), lambda bi, j: (bi, j, 0)),
                   pl.BlockSpec((1, tr, LANE), lambda bi, j: (bi, j, 0))],
        out_shape=[jax.ShapeDtypeStruct((b, t_all, d), BF16),
                   jax.ShapeDtypeStruct((b, t_all, LANE), F32)],
        compiler_params=_params(("parallel", "parallel"), 32),
        name="prenorm",
    )(x, ctx, shift, scale, norm_g, w_dt)


def _qkv_kernel(x_ref, w_ref, cos_ref, sa_ref, sb_ref, o_ref, *, n_rope):
    acc = jnp.dot(x_ref[...], w_ref[...], preferred_element_type=F32)
    cos, sa, sb = cos_ref[...], sa_ref[...], sb_ref[...]
    quarter = HEAD_DIM // 4
    for jb in range(acc.shape[1] // LANE):
        blk = acc[:, jb * LANE:(jb + 1) * LANE]
        if jb < n_rope:
            blk = (blk * cos + pltpu.roll(blk, LANE - quarter, 1) * sa
                   + pltpu.roll(blk, quarter, 1) * sb)
        o_ref[:, jb * LANE:(jb + 1) * LANE] = blk.astype(o_ref.dtype)


def _qkv_proj(xn2d, w_qkv, cos_t, sa_t, sb_t):
    m, d = xn2d.shape
    n = w_qkv.shape[1]
    t_all = cos_t.shape[0]
    tm = 256
    assert m % tm == 0 and t_all % tm == 0
    nt = t_all // tm
    tab = pl.BlockSpec((tm, LANE), lambda i: (i % nt, 0))
    return pl.pallas_call(
        functools.partial(_qkv_kernel, n_rope=(D_ATT + D_KV) // LANE),
        grid=(m // tm,),
        in_specs=[pl.BlockSpec((tm, d), lambda i: (i, 0)),
                  pl.BlockSpec((d, n), lambda i: (0, 0)),
                  tab, tab, tab],
        out_specs=pl.BlockSpec((tm, n), lambda i: (i, 0)),
        out_shape=jax.ShapeDtypeStruct((m, n), BF16),
        compiler_params=_params(("parallel",), 40),
        name="qkv_proj",
    )(xn2d, w_qkv, cos_t, sa_t, sb_t)


def _mm_kernel(x_ref, w_ref, o_ref):
    o_ref[...] = jnp.dot(x_ref[...], w_ref[...], preferred_element_type=F32).astype(o_ref.dtype)


def _matmul(x2d, w, out_dtype, tn=1024):
    m, d = x2d.shape
    n = w.shape[1]
    tm = _row_tile(m)
    assert n % tn == 0
    return pl.pallas_call(
        _mm_kernel,
        grid=(n // tn, m // tm),
        in_specs=[pl.BlockSpec((tm, d), lambda j, i: (i, 0)),
                  pl.BlockSpec((d, tn), lambda j, i: (0, j))],
        out_specs=pl.BlockSpec((tm, tn), lambda j, i: (i, j)),
        out_shape=jax.ShapeDtypeStruct((m, n), out_dtype),
        compiler_params=_params(("parallel", "parallel"), 48),
        name="gzx_proj",
    )(x2d, w)


def _attn_kernel(sink_ref, q_ref, kp_ref, ko_ref, kn_ref, kc_ref, vp_ref, vo_ref, vn_ref, vc_ref,
                 g_ref, ng_ref, o_ref, *, nb):
    n = pl.program_id(1)
    nctx = kc_ref.shape[1]
    nk = 3 * BLOCK + nctx
    q = q_ref[0] * jnp.asarray(HEAD_DIM ** -0.5, BF16)
    k_all = jnp.concatenate([kp_ref[0], ko_ref[0], kn_ref[0], kc_ref[0]], axis=0)
    v_all = jnp.concatenate([vp_ref[0], vo_ref[0], vn_ref[0], vc_ref[0]], axis=0)

    qi = lax.broadcasted_iota(jnp.int32, (BLOCK, nk), 0)
    kj = lax.broadcasted_iota(jnp.int32, (BLOCK, nk), 1)
    rel = kj - BLOCK - qi
    valid = (jnp.abs(rel) <= BLOCK) & ((kj >= BLOCK) | (n > 0)) & ((kj < 2 * BLOCK) | (n < nb - 1))
    valid = valid | (kj >= 3 * BLOCK)
    bias1 = jnp.where(valid, 0.0, NEG).astype(F32)
    bias = jnp.concatenate([bias1] * Q_PER_KV, axis=0)
    head_of_row = lax.broadcasted_iota(jnp.int32, (Q_PER_KV * BLOCK, 1), 0) // BLOCK

    pieces = []
    for kh in range(ATT_KV_HEADS):
        k_h = k_all[:, kh * HEAD_DIM:(kh + 1) * HEAD_DIM]
        v_h = v_all[:, kh * HEAD_DIM:(kh + 1) * HEAD_DIM]
        heads = [kh * Q_PER_KV + i for i in range(Q_PER_KV)]
        qg = jnp.concatenate([q[:, h * HEAD_DIM:(h + 1) * HEAD_DIM] for h in heads], axis=0)
        s = lax.dot_general(qg, k_h, (((1,), (1,)), ((), ())), preferred_element_type=F32) + bias
        sink_col = jnp.zeros((Q_PER_KV * BLOCK, 1), F32)
        for i, h in enumerate(heads):
            sink_col = jnp.where(head_of_row == i, sink_ref[h], sink_col)
        m = jnp.maximum(jnp.max(s, axis=1, keepdims=True), sink_col)
        p = jnp.exp(s - m)
        denom = jnp.sum(p, axis=1, keepdims=True) + jnp.exp(sink_col - m)
        o = jnp.dot(p.astype(BF16), v_h, preferred_element_type=F32) / denom
        pieces.extend(o[i * BLOCK:(i + 1) * BLOCK] for i in range(Q_PER_KV))
    att = jnp.concatenate(pieces, axis=1)

    a = att * _silu(g_ref[0])
    ms = jnp.mean(a * a, axis=-1, keepdims=True)
    o_ref[0] = (a * lax.rsqrt(ms + EPS) * ng_ref[...]).astype(o_ref.dtype)


def _attention(sink, qkv, gzx, att_norm_g, s, nc):
    b = qkv.shape[0]
    assert s % BLOCK == 0 and s % nc == 0
    nb = s // BLOCK
    kcol, vcol = D_ATT // D_KV, D_ATT // D_KV + 1
    kv = lambda col, f: pl.BlockSpec((1, BLOCK, D_KV), lambda bi, n: (bi, f(n), col))
    prev_i = lambda n: jnp.maximum(n - 1, 0)
    own_i = lambda n: n
    next_i = lambda n: jnp.minimum(n + 1, nb - 1)
    ctx_spec = lambda col: pl.BlockSpec((1, nc, D_KV), lambda bi, n: (bi, s // nc, col))
    return pl.pallas_call(
        functools.partial(_attn_kernel, nb=nb),
        grid=(b, nb),
        in_specs=[pl.BlockSpec(memory_space=pltpu.SMEM),
                  pl.BlockSpec((1, BLOCK, D_ATT), lambda bi, n: (bi, n, 0)),
                  kv(kcol, prev_i), kv(kcol, own_i), kv(kcol, next_i), ctx_spec(kcol),
                  kv(vcol, prev_i), kv(vcol, own_i), kv(vcol, next_i), ctx_spec(vcol),
                  pl.BlockSpec((1, BLOCK, D_ATT), lambda bi, n: (bi, n, 0)),
                  pl.BlockSpec((1, D_ATT), lambda bi, n: (0, 0))],
        out_specs=pl.BlockSpec((1, BLOCK, D_ATT), lambda bi, n: (bi, n, 0)),
        out_shape=jax.ShapeDtypeStruct((b, s, D_ATT), BF16),
        compiler_params=_params(("parallel", "parallel"), 40),
        name="window_attn",
    )(sink, qkv, qkv, qkv, qkv, qkv, qkv, qkv, qkv, qkv, gzx, att_norm_g)


def _ssd_chunk_index(t, n_lat, n_ctx):
    n_all = n_lat + n_ctx
    k = t - n_all
    fwd = jnp.where(k < n_ctx, n_lat + k, k - n_ctx)
    return jnp.where(t >= n_all, fwd, n_all - 1 - t)


def _ssd_kernel(xbc_ref, hp_ref, hn_ref, dt_ref, z_ref, cw_ref, cb_ref, dtb_ref, alog_ref, dsk_ref,
                ng_ref, ex_ref, o_ref, sf_ref, sb_ref, sin_ref, *, n_lat, n_ctx):
    t = pl.program_id(1)
    n_all = n_lat + n_ctx
    cidx = _ssd_chunk_index(t, n_lat, n_ctx)
    sweep1 = t >= n_all
    is_lat = cidx < n_lat
    L = CHUNK

    @pl.when(t == 0)
    def _():
        sf_ref[...] = jnp.zeros_like(sf_ref)
        sb_ref[...] = jnp.zeros_like(sb_ref)

    is_first = (cidx == 0) | (cidx == n_lat)
    is_last = (cidx == n_lat - 1) | (cidx == n_all - 1)
    xc = xbc_ref[0]
    prev_row = jnp.where(is_first, 0.0, hp_ref[0][SUBLANE - 1:SUBLANE, :])
    next_row = jnp.where(is_last, 0.0, hn_ref[0][0:1, :])
    rows = lax.broadcasted_iota(jnp.int32, (L, 1), 0)
    x_prev = jnp.where(rows == 0, prev_row, pltpu.roll(xc, 1, 0))
    x_next = jnp.where(rows == L - 1, next_row, pltpu.roll(xc, L - 1, 0))
    cw = cw_ref[...]
    u = _silu(x_prev * cw[0:1] + xc * cw[1:2] + x_next * cw[2:3] + cb_ref[...])
    xs = u[:, :D_SSD]
    b_g = [u[:, D_SSD + g * D_STATE:D_SSD + (g + 1) * D_STATE].astype(BF16) for g in range(SSD_GROUPS)]
    c_g = [u[:, D_SSD + D_BC + g * D_STATE:D_SSD + D_BC + (g + 1) * D_STATE].astype(BF16)
           for g in range(SSD_GROUPS)]

    dtr = dt_ref[0] + dtb_ref[...]
    dt = jnp.maximum(dtr, 0.0) + jnp.log(1.0 + jnp.exp(-jnp.abs(dtr)))
    a = dt * (-jnp.exp(alog_ref[...]))
    pre, suf = a, a
    d = 1
    while d < L:
        pre = pre + jnp.where(rows >= d, pltpu.roll(pre, d, 0), 0.0)
        suf = suf + jnp.where(rows < L - d, pltpu.roll(suf, L - d, 0), 0.0)
        d *= 2
    lane = lax.broadcasted_iota(jnp.int32, (1, LANE), 1)
    is_fwd_lane = lane < SSD_HEADS
    cs = jnp.where(is_fwd_lane, pre, suf)
    tot = jnp.where(is_fwd_lane, cs[L - 1:L, :], cs[0:1, :])
    e = jnp.exp(cs)
    wdt = jnp.exp(tot - cs) * dt

    fac = jnp.where(lane < 2 * SSD_HEADS, wdt, jnp.where(lane < 4 * SSD_HEADS, pltpu.roll(e, 2 * SSD_HEADS, 1), 0.0))
    hi = fac.astype(BF16).astype(F32)
    r1 = fac - hi
    mid = r1.astype(BF16).astype(F32)
    lo = (r1 - mid).astype(BF16)
    lhs_ex = jnp.concatenate([(hi + pltpu.roll(mid, LANE // 2, 1)).astype(BF16), lo], axis=1)

    def expand(block, rows_slice=None):
        lhs = lhs_ex if rows_slice is None else lhs_ex[rows_slice]
        return jnp.dot(lhs, ex_ref[:, block * D_SSD:(block + 1) * D_SSD], preferred_element_type=F32)

    def local_states(wdt_exp):
        xw = (xs * wdt_exp).astype(BF16)
        return jnp.concatenate(
            [lax.dot_general(b_g[g], xw[:, g * SSD_HPG * SSD_HEAD_DIM:(g + 1) * SSD_HPG * SSD_HEAD_DIM],
                             (((0,), (0,)), ((), ())), preferred_element_type=F32)
             for g in range(SSD_GROUPS)], axis=1)

    @pl.when(jnp.logical_not(sweep1))
    def _():
        @pl.when(is_lat)
        def _():
            sin_ref[jnp.minimum(cidx, n_lat - 1)] = sb_ref[...].astype(BF16)
        tot_b = expand(3, slice(0, SUBLANE))[0:1, :]
        sb_ref[...] = sb_ref[...] * tot_b + local_states(expand(1))

    @pl.when(sweep1 & jnp.logical_not(is_lat))
    def _():
        tot_f = expand(2, slice(L - SUBLANE, L))[SUBLANE - 1:SUBLANE, :]
        sf_ref[...] = sf_ref[...] * tot_f + local_states(expand(0))

    @pl.when(sweep1 & is_lat)
    def _():
        wexp = [expand(0), expand(1)]
        eexp = [expand(2), expand(3)]
        pack = jnp.where(lane < 2 * SSD_HEADS, cs, pltpu.roll(dt, 2 * SSD_HEADS, 1))
        pack_t = pack.T
        li = lax.broadcasted_iota(jnp.int32, (L, L), 0)
        si = lax.broadcasted_iota(jnp.int32, (L, L), 1)
        masks = [si <= li, si >= li]
        gw = SSD_HPG * SSD_HEAD_DIM
        head_of_lane = lax.broadcasted_iota(jnp.int32, (1, gw), 1) // SSD_HEAD_DIM
        state_prev = [sf_ref[...].astype(BF16), sin_ref[jnp.minimum(cidx, n_lat - 1)]]
        xs_bf = xs.astype(BF16)
        y_groups = []
        for g in range(SSD_GROUPS):
            cb = lax.dot_general(c_g[g], b_g[g], (((1,), (1,)), ((), ())), preferred_element_type=F32)
            xg = xs_bf[:, g * gw:(g + 1) * gw]
            blockdiag = jnp.concatenate(
                [jnp.where(head_of_lane == hh, xg, jnp.zeros_like(xg)) for hh in range(SSD_HPG)], axis=0)
            yg = None
            for dirn in range(2):
                ws = []
                for hh in range(SSD_HPG):
                    ln = dirn * SSD_HEADS + g * SSD_HPG + hh
                    col = cs[:, ln:ln + 1]
                    row = pack_t[ln:ln + 1, :]
                    dtrow = pack_t[2 * SSD_HEADS + ln:2 * SSD_HEADS + ln + 1, :]
                    dec = jnp.exp(jnp.where(masks[dirn], col - row, NEG))
                    ws.append((cb * dec * dtrow).astype(BF16))
                y_diag = jnp.dot(jnp.concatenate(ws, axis=1), blockdiag, preferred_element_type=F32)
                y_off = jnp.dot(c_g[g], state_prev[dirn][:, g * gw:(g + 1) * gw],
                                preferred_element_type=F32) * eexp[dirn][:, g * gw:(g + 1) * gw]
                yg = y_diag + y_off if yg is None else yg + y_diag + y_off
            y_groups.append(yg)
        y = jnp.concatenate(y_groups, axis=1) + xs * dsk_ref[...]
        sf_ref[...] = sf_ref[...] * eexp[0][L - 1:L, :] + local_states(wexp[0])

        yz = y * _silu(z_ref[0])
        ms = jnp.mean(yz * yz, axis=-1, keepdims=True)
        o_ref[0] = (yz * lax.rsqrt(ms + EPS) * ng_ref[...]).astype(o_ref.dtype)


def _expansion_matrix():
    r = jnp.arange(2 * LANE)[:, None]
    col = jnp.arange(4 * D_SSD)[None, :]
    src = (col // D_SSD) * SSD_HEADS + (col % D_SSD) // SSD_HEAD_DIM
    half = LANE // 2
    return ((r % half == src) & (r < 3 * half)).astype(BF16)


def _ssd(gzx, dtraw, conv_w, conv_b, dt_bias, a_log, d_skip_exp, ssd_norm_g, s, nc):
    b, t_all, _ = gzx.shape
    assert s % CHUNK == 0 and nc % CHUNK == 0
    n_lat, n_ctx = s // CHUNK, nc // CHUNK
    n_all = n_lat + n_ctx
    halo_per_chunk = CHUNK // SUBLANE
    n_halo = t_all // SUBLANE
    cidx = lambda t: _ssd_chunk_index(t, n_lat, n_ctx)
    xbc_col = (D_ATT + D_SSD) // D_XBC
    z_col = D_ATT // D_SSD
    out_idx = lambda t: jnp.where(t >= n_all + n_ctx, t - n_all - n_ctx, 0)
    const = lambda shape: pl.BlockSpec(shape, lambda bi, t: tuple(0 for _ in shape))
    return pl.pallas_call(
        functools.partial(_ssd_kernel, n_lat=n_lat, n_ctx=n_ctx),
        grid=(b, 2 * n_all),
        in_specs=[pl.BlockSpec((1, CHUNK, D_XBC), lambda bi, t: (bi, cidx(t), xbc_col)),
                  pl.BlockSpec((1, SUBLANE, D_XBC),
                               lambda bi, t: (bi, jnp.maximum(cidx(t) * halo_per_chunk - 1, 0), xbc_col)),
                  pl.BlockSpec((1, SUBLANE, D_XBC),
                               lambda bi, t: (bi, jnp.minimum((cidx(t) + 1) * halo_per_chunk, n_halo - 1), xbc_col)),
                  pl.BlockSpec((1, CHUNK, LANE), lambda bi, t: (bi, cidx(t), 0)),
                  pl.BlockSpec((1, CHUNK, D_SSD), lambda bi, t: (bi, cidx(t), z_col)),
                  const((CONV_W, D_XBC)), const((1, D_XBC)), const((1, LANE)), const((1, LANE)),
                  const((1, D_SSD)), const((1, D_SSD)), const((2 * LANE, 4 * D_SSD))],
        out_specs=pl.BlockSpec((1, CHUNK, D_SSD), lambda bi, t: (bi, out_idx(t), 0)),
        out_shape=jax.ShapeDtypeStruct((b, s, D_SSD), BF16),
        scratch_shapes=[pltpu.VMEM((D_STATE, D_SSD), F32),
                        pltpu.VMEM((D_STATE, D_SSD), F32),
                        pltpu.VMEM((n_lat, D_STATE, D_SSD), BF16)],
        compiler_params=_params(("arbitrary", "arbitrary"), 48),
        name="bi_ssd",
    )(gzx, gzx, gzx, dtraw, gzx, conv_w, conv_b, dt_bias, a_log, d_skip_exp, ssd_norm_g,
      _expansion_matrix())


def _out_kernel(ha_ref, hs_ref, wa_ref, ws_ref, x_ref, gate_ref, ng_ref, o_ref):
    upd = (jnp.dot(ha_ref[...], wa_ref[...], preferred_element_type=F32)
           + jnp.dot(hs_ref[...], ws_ref[...], preferred_element_type=F32))
    y = x_ref[...] + gate_ref[0] * upd
    ms = jnp.mean(y * y, axis=-1, keepdims=True)
    o_ref[...] = (y * lax.rsqrt(ms + EPS) * ng_ref[...]).astype(o_ref.dtype)


def _out_proj(h_att, h_ssd, w_out_att, w_out_ssd, x2d, gate, final_norm_g, s):
    m, d = x2d.shape
    tm = _row_tile(s, (512, 256, 128))
    tiles_per_batch = s // tm
    return pl.pallas_call(
        _out_kernel,
        grid=(m // tm,),
        in_specs=[pl.BlockSpec((tm, D_ATT), lambda i: (i, 0)),
                  pl.BlockSpec((tm, D_SSD), lambda i: (i, 0)),
                  pl.BlockSpec((D_ATT, d), lambda i: (0, 0)),
                  pl.BlockSpec((D_SSD, d), lambda i: (0, 0)),
                  pl.BlockSpec((tm, d), lambda i: (i, 0)),
                  pl.BlockSpec((1, 1, d), lambda i: (i // tiles_per_batch, 0, 0)),
                  pl.BlockSpec((1, d), lambda i: (0, 0))],
        out_specs=pl.BlockSpec((tm, d), lambda i: (i, 0)),
        out_shape=jax.ShapeDtypeStruct((m, d), x2d.dtype),
        compiler_params=_params(("parallel",), 48),
        name="out_proj",
    )(h_att, h_ssd, w_out_att, w_out_ssd, x2d, gate, final_norm_g)


def _rope_tables(s, nc):
    pos = jnp.arange(s)
    row = (pos // GRID_W).astype(F32)
    colp = (pos % GRID_W).astype(F32)
    quarter = HEAD_DIM // 4
    freq = 1.0 / (ROPE_THETA ** (jnp.arange(quarter, dtype=F32) / quarter))
    lane = jnp.arange(LANE)
    dim = lane % HEAD_DIM
    use_col = (dim // (HEAD_DIM // 2)) == 1
    r = dim % (HEAD_DIM // 2)
    first = r < quarter
    ang = jnp.where(use_col[None, :], colp[:, None], row[:, None]) * freq[r % quarter][None, :]
    cos, sin = jnp.cos(ang), jnp.sin(ang)
    sa = jnp.where(first[None, :], -sin, 0.0)
    sb = jnp.where(first[None, :], 0.0, sin)
    ident = lambda v, fill: jnp.concatenate([v, jnp.full((nc, LANE), fill, F32)], axis=0)
    return ident(cos, 1.0), ident(sa, 0.0), ident(sb, 0.0)


def kernel(x, c, ctx, c_ctx, w_mod, b_mod, norm_g, w_in, conv_w, conv_b, a_log_f, a_log_b, dt_bias_f,
           dt_bias_b, d_skip, att_norm_g, ssd_norm_g, sink, w_out, final_norm_g):
    assert w_mod.shape[0] == 1, "single-layer operation"
    b, s, d = x.shape
    nc = ctx.shape[1]
    t_all = s + nc

    pad_rows = (-(b + 1)) % SUBLANE
    cc = jnp.concatenate([c, c_ctx[None, :], jnp.zeros((pad_rows, d), F32)], axis=0)
    mod = _modulation(cc, w_mod[0], b_mod)
    shift = mod[:b + 1, :d].reshape(b + 1, 1, d)
    scale = mod[:b + 1, d:2 * d].reshape(b + 1, 1, d)
    gate = mod[:b, 2 * d:].reshape(b, 1, d)

    w = w_in[0]
    w_qkv = w[:, :D_QKV].astype(BF16)
    w_gzx = w[:, D_QKV:D_QKV + D_GZX].astype(BF16)
    w_dt = jnp.pad(w[:, D_QKV + D_GZX:], ((0, 0), (0, LANE - 2 * SSD_HEADS))).astype(BF16)
    w_out_bf = w_out[0].astype(BF16)

    xn, dtraw = _prenorm(x, ctx, shift, scale, norm_g, w_dt)
    xn2d = xn.reshape(b * t_all, d)

    cos_t, sa_t, sb_t = _rope_tables(s, nc)
    qkv = _qkv_proj(xn2d, w_qkv, cos_t, sa_t, sb_t).reshape(b, t_all, D_QKV)
    gzx = _matmul(xn2d, w_gzx, F32).reshape(b, t_all, D_GZX)

    h_att = _attention(sink[0], qkv, gzx, att_norm_g, s, nc)

    lane_pad = LANE - 2 * SSD_HEADS
    dt_bias = jnp.pad(jnp.concatenate([dt_bias_f[0], dt_bias_b[0]]), (0, lane_pad))[None, :]
    a_log = jnp.pad(jnp.concatenate([a_log_f[0], a_log_b[0]]), (0, lane_pad))[None, :]
    d_skip_exp = jnp.repeat(d_skip[0], SSD_HEAD_DIM)[None, :]
    h_ssd = _ssd(gzx, dtraw, conv_w[0], conv_b, dt_bias, a_log, d_skip_exp, ssd_norm_g, s, nc)

    out = _out_proj(h_att.reshape(b * s, D_ATT), h_ssd.reshape(b * s, D_SSD),
                    w_out_bf[:D_ATT], w_out_bf[D_ATT:], x.reshape(b * s, d), gate,
                    final_norm_g[None, :], s)
    return out.reshape(b, s, d)
```

```python
import functools

import jax
import jax.numpy as jnp
from jax import lax
from jax.experimental import pallas as pl
from jax.experimental.pallas import tpu as pltpu

F32 = jnp.float32
BF16 = jnp.bfloat16

EPS = 1e-6
GRID_W = 64
ROPE_THETA = 10000.0
ATT_HEADS = 16
ATT_KV_HEADS = 4
HEAD_DIM = 64
D_ATT = ATT_HEADS * HEAD_DIM
D_KV = ATT_KV_HEADS * HEAD_DIM
Q_PER_KV = ATT_HEADS // ATT_KV_HEADS
BLOCK = 128
SSD_HEADS = 16
SSD_HEAD_DIM = 64
D_SSD = SSD_HEADS * SSD_HEAD_DIM
SSD_GROUPS = 4
SSD_HPG = SSD_HEADS // SSD_GROUPS
D_STATE = 128
D_BC = SSD_GROUPS * D_STATE
D_XBC = D_SSD + 2 * D_BC
CONV_W = 3
CHUNK = 128
D_QKV = D_ATT + 2 * D_KV
D_GZX = D_ATT + D_SSD + D_XBC

LANE = 128
SUBLANE = 8
V7X_VMEM_BYTES = 64 * 1024 * 1024
MIB = 1024 * 1024

NEG = -1e30
LOG2E = 1.4426950408889634


def _silu(v):
    return v / (1.0 + jnp.exp(-v))


def _params(semantics, vmem_mib):
    return pltpu.CompilerParams(dimension_semantics=semantics, vmem_limit_bytes=vmem_mib * MIB)


def _row_tile(m, candidates=(1024, 512, 256, 128)):
    for t in candidates:
        if m % t == 0:
            return t
    raise ValueError(f"row count {m} not tileable")


def _mod_kernel(c_ref, w_ref, b_ref, o_ref):
    a = _silu(c_ref[...]).astype(BF16)
    o_ref[...] = jnp.dot(a, w_ref[...].astype(BF16), preferred_element_type=F32) + b_ref[...]


def _modulation(cc, w_mod, b_mod):
    rows, d = cc.shape
    n = w_mod.shape[1]
    tn = 768
    assert n % tn == 0
    return pl.pallas_call(
        _mod_kernel,
        grid=(n // tn,),
        in_specs=[pl.BlockSpec((rows, d), lambda j: (0, 0)),
                  pl.BlockSpec((d, tn), lambda j: (0, j)),
                  pl.BlockSpec((1, tn), lambda j: (0, j))],
        out_specs=pl.BlockSpec((rows, tn), lambda j: (0, j)),
        out_shape=jax.ShapeDtypeStruct((rows, n), F32),
        compiler_params=_params(("parallel",), 32),
        name="modulation",
    )(cc, w_mod, b_mod)


def _prenorm_kernel(x_ref, ctx_ref, shift_ref, scale_ref, g_ref, wdt_ref, xn_ref, dt_ref, *, n_lat):
    j = pl.program_id(1)

    def emit(v):
        ms = jnp.mean(v * v, axis=-1, keepdims=True)
        y = v * lax.rsqrt(ms + EPS) * g_ref[...]
        y = (y * (1.0 + scale_ref[0]) + shift_ref[0]).astype(BF16)
        xn_ref[0] = y
        dt_ref[0] = jnp.dot(y, wdt_ref[...], preferred_element_type=F32)

    @pl.when(j < n_lat)
    def _():
        emit(x_ref[0])

    @pl.when(j >= n_lat)
    def _():
        emit(ctx_ref[0])


def _prenorm(x, ctx, shift, scale, norm_g, w_dt):
    b, s, d = x.shape
    nc = ctx.shape[1]
    tr = 256
    assert s % tr == 0 and nc % tr == 0
    n_lat, n_ctx = s // tr, nc // tr
    t_all = s + nc
    mod_idx = lambda bi, j: (jnp.where(j < n_lat, bi, b), 0, 0)
    return pl.pallas_call(
        functools.partial(_prenorm_kernel, n_lat=n_lat),
        grid=(b, n_lat + n_ctx),
        in_specs=[pl.BlockSpec((1, tr, d), lambda bi, j: (bi, jnp.minimum(j, n_lat - 1), 0)),
                  pl.BlockSpec((1, tr, d), lambda bi, j: (bi, jnp.maximum(j - n_lat, 0), 0)),
                  pl.BlockSpec((1, 1, d), mod_idx),
                  pl.BlockSpec((1, 1, d), mod_idx),
                  pl.BlockSpec((1, d), lambda bi, j: (0, 0)),
                  pl.BlockSpec((d, LANE), lambda bi, j: (0, 0))],
        out_specs=[pl.BlockSpec((1, tr, d), lambda bi, j: (bi, j, 0)),
                   pl.BlockSpec((1, tr, LANE), lambda bi, j: (bi, j, 0))],
        out_shape=[jax.ShapeDtypeStruct((b, t_all, d), BF16),
                   jax.ShapeDtypeStruct((b, t_all, LANE), F32)],
        compiler_params=_params(("parallel", "parallel"), 32),
        name="prenorm",
    )(x, ctx, shift, scale, norm_g, w_dt)


def _qkv_kernel(x_ref, w_ref, cos_ref, sa_ref, sb_ref, o_ref, *, n_rope):
    acc = jnp.dot(x_ref[...], w_ref[...], preferred_element_type=F32)
    cos, sa, sb = cos_ref[...], sa_ref[...], sb_ref[...]
    quarter = HEAD_DIM // 4
    for jb in range(acc.shape[1] // LANE):
        blk = acc[:, jb * LANE:(jb + 1) * LANE]
        if jb < n_rope:
            blk = (blk * cos + pltpu.roll(blk, LANE - quarter, 1) * sa
                   + pltpu.roll(blk, quarter, 1) * sb)
        o_ref[:, jb * LANE:(jb + 1) * LANE] = blk.astype(o_ref.dtype)


def _qkv_proj(xn2d, w_qkv, cos_t, sa_t, sb_t):
    m, d = xn2d.shape
    n = w_qkv.shape[1]
    t_all = cos_t.shape[0]
    tm = 256
    assert m % tm == 0 and t_all % tm == 0
    nt = t_all // tm
    tab = pl.BlockSpec((tm, LANE), lambda i: (i % nt, 0))
    return pl.pallas_call(
        functools.partial(_qkv_kernel, n_rope=(D_ATT + D_KV) // LANE),
        grid=(m // tm,),
        in_specs=[pl.BlockSpec((tm, d), lambda i: (i, 0)),
                  pl.BlockSpec((d, n), lambda i: (0, 0)),
                  tab, tab, tab],
        out_specs=pl.BlockSpec((tm, n), lambda i: (i, 0)),
        out_shape=jax.ShapeDtypeStruct((m, n), BF16),
        compiler_params=_params(("parallel",), 40),
        name="qkv_proj",
    )(xn2d, w_qkv, cos_t, sa_t, sb_t)


def _mm_kernel(x_ref, w_ref, o_ref):
    o_ref[...] = jnp.dot(x_ref[...], w_ref[...], preferred_element_type=F32).astype(o_ref.dtype)


def _matmul(x2d, w, out_dtype, tn=1024):
    m, d = x2d.shape
    n = w.shape[1]
    tm = _row_tile(m)
    assert n % tn == 0
    return pl.pallas_call(
        _mm_kernel,
        grid=(n // tn, m // tm),
        in_specs=[pl.BlockSpec((tm, d), lambda j, i: (i, 0)),
                  pl.BlockSpec((d, tn), lambda j, i: (0, j))],
        out_specs=pl.BlockSpec((tm, tn), lambda j, i: (i, j)),
        out_shape=jax.ShapeDtypeStruct((m, n), out_dtype),
        compiler_params=_params(("parallel", "parallel"), 48),
        name="gzx_proj",
    )(x2d, w)


def _attn_kernel(sink_ref, q_ref, kp_ref, ko_ref, kn_ref, kc_ref, vp_ref, vo_ref, vn_ref, vc_ref,
                 g_ref, ng_ref, o_ref, *, nb):
    n = pl.program_id(1)
    nctx = kc_ref.shape[1]
    nk = 3 * BLOCK + nctx
    q = q_ref[0] * jnp.asarray(HEAD_DIM ** -0.5, BF16)
    k_all = jnp.concatenate([kp_ref[0], ko_ref[0], kn_ref[0], kc_ref[0]], axis=0)
    v_all = jnp.concatenate([vp_ref[0], vo_ref[0], vn_ref[0], vc_ref[0]], axis=0)

    qi = lax.broadcasted_iota(jnp.int32, (BLOCK, nk), 0)
    kj = lax.broadcasted_iota(jnp.int32, (BLOCK, nk), 1)
    rel = kj - BLOCK - qi
    valid = (jnp.abs(rel) <= BLOCK) & ((kj >= BLOCK) | (n > 0)) & ((kj < 2 * BLOCK) | (n < nb - 1))
    valid = valid | (kj >= 3 * BLOCK)
    bias1 = jnp.where(valid, 0.0, NEG).astype(F32)
    bias = jnp.concatenate([bias1] * Q_PER_KV, axis=0)
    head_of_row = lax.broadcasted_iota(jnp.int32, (Q_PER_KV * BLOCK, 1), 0) // BLOCK

    pieces = []
    for kh in range(ATT_KV_HEADS):
        k_h = k_all[:, kh * HEAD_DIM:(kh + 1) * HEAD_DIM]
        v_h = v_all[:, kh * HEAD_DIM:(kh + 1) * HEAD_DIM]
        heads = [kh * Q_PER_KV + i for i in range(Q_PER_KV)]
        qg = jnp.concatenate([q[:, h * HEAD_DIM:(h + 1) * HEAD_DIM] for h in heads], axis=0)
        s = lax.dot_general(qg, k_h, (((1,), (1,)), ((), ())), preferred_element_type=F32) + bias
        sink_col = jnp.zeros((Q_PER_KV * BLOCK, 1), F32)
        for i, h in enumerate(heads):
            sink_col = jnp.where(head_of_row == i, sink_ref[h], sink_col)
        m = jnp.maximum(jnp.max(s, axis=1, keepdims=True), sink_col)
        p = jnp.exp(s - m)
        denom = jnp.sum(p, axis=1, keepdims=True) + jnp.exp(sink_col - m)
        o = jnp.dot(p.astype(BF16), v_h, preferred_element_type=F32) / denom
        pieces.extend(o[i * BLOCK:(i + 1) * BLOCK] for i in range(Q_PER_KV))
    att = jnp.concatenate(pieces, axis=1)

    a = att * _silu(g_ref[0])
    ms = jnp.mean(a * a, axis=-1, keepdims=True)
    o_ref[0] = (a * lax.rsqrt(ms + EPS) * ng_ref[...]).astype(o_ref.dtype)


def _attention(sink, qkv, gzx, att_norm_g, s, nc):
    b = qkv.shape[0]
    assert s % BLOCK == 0 and s % nc == 0
    nb = s // BLOCK
    kcol, vcol = D_ATT // D_KV, D_ATT // D_KV + 1
    kv = lambda col, f: pl.BlockSpec((1, BLOCK, D_KV), lambda bi, n: (bi, f(n), col))
    prev_i = lambda n: jnp.maximum(n - 1, 0)
    own_i = lambda n: n
    next_i = lambda n: jnp.minimum(n + 1, nb - 1)
    ctx_spec = lambda col: pl.BlockSpec((1, nc, D_KV), lambda bi, n: (bi, s // nc, col))
    return pl.pallas_call(
        functools.partial(_attn_kernel, nb=nb),
        grid=(b, nb),
        in_specs=[pl.BlockSpec(memory_space=pltpu.SMEM),
                  pl.BlockSpec((1, BLOCK, D_ATT), lambda bi, n: (bi, n, 0)),
                  kv(kcol, prev_i), kv(kcol, own_i), kv(kcol, next_i), ctx_spec(kcol),
                  kv(vcol, prev_i), kv(vcol, own_i), kv(vcol, next_i), ctx_spec(vcol),
                  pl.BlockSpec((1, BLOCK, D_ATT), lambda bi, n: (bi, n, 0)),
                  pl.BlockSpec((1, D_ATT), lambda bi, n: (0, 0))],
        out_specs=pl.BlockSpec((1, BLOCK, D_ATT), lambda bi, n: (bi, n, 0)),
        out_shape=jax.ShapeDtypeStruct((b, s, D_ATT), BF16),
        compiler_params=_params(("parallel", "parallel"), 40),
        name="window_attn",
    )(sink, qkv, qkv, qkv, qkv, qkv, qkv, qkv, qkv, qkv, gzx, att_norm_g)


def _ssd_chunk_index(t, n_lat, n_ctx):
    n_all = n_lat + n_ctx
    k = t - n_all
    fwd = jnp.where(k < n_ctx, n_lat + k, k - n_ctx)
    return jnp.where(t >= n_all, fwd, n_all - 1 - t)


def _ssd_kernel(xbc_ref, hp_ref, hn_ref, dt_ref, z_ref, cw_ref, cb_ref, dtb_ref, alog_ref, dsk_ref,
                ng_ref, ex_ref, o_ref, sf_ref, sb_ref, sin_ref, ux_ref, ubc_ref, lhs_ref, pack_ref, y_ref,
                *, n_lat, n_ctx):
    t = pl.program_id(1)
    n_all = n_lat + n_ctx
    cidx = _ssd_chunk_index(t, n_lat, n_ctx)
    lat_idx = jnp.minimum(cidx, n_lat - 1)
    sweep1 = t >= n_all
    is_lat = cidx < n_lat
    L = CHUNK
    gw = SSD_HPG * SSD_HEAD_DIM
    lane = lax.broadcasted_iota(jnp.int32, (1, LANE), 1)
    rows = lax.broadcasted_iota(jnp.int32, (L, 1), 0)

    @pl.when(t == 0)
    def _():
        sf_ref[...] = jnp.zeros_like(sf_ref)
        sb_ref[...] = jnp.zeros_like(sb_ref)
        ux_ref[n_all - 1] = jnp.zeros((CHUNK, D_SSD), F32)
        ubc_ref[n_all - 1] = jnp.zeros((CHUNK, 2 * D_BC), BF16)
        lhs_ref[n_all - 1] = jnp.zeros((CHUNK, 2 * LANE), BF16)

    def expand(lhs, block, g):
        c0 = block * D_SSD + g * gw
        return jnp.dot(lhs, ex_ref[:, c0:c0 + gw], preferred_element_type=F32)

    def update_state(state_ref, chunk, lhs_ex, g, wblock, eblock, end_row):
        sl = slice(g * gw, (g + 1) * gw)
        r8 = (end_row // SUBLANE) * SUBLANE
        xw = (ux_ref[chunk, :, sl] * expand(lhs_ex, wblock, g)).astype(BF16)
        bg = ubc_ref[chunk, :, g * D_STATE:(g + 1) * D_STATE]
        loc = lax.dot_general(bg, xw, (((0,), (0,)), ((), ())), preferred_element_type=F32)
        tot = expand(lhs_ex[r8:r8 + SUBLANE], eblock, g)[end_row - r8:end_row - r8 + 1]
        new = state_ref[:, sl] * tot + loc
        state_ref[:, sl] = new
        return new

    @pl.when(jnp.logical_not(sweep1))
    def _():
        done = jnp.minimum(cidx + 1, n_all - 1)
        lhs_done = lhs_ref[done]
        groups = range(SSD_GROUPS)
        gsl = [slice(g * gw, (g + 1) * gw) for g in groups]
        is_first = (cidx == 0) | (cidx == n_lat)
        is_last = (cidx == n_lat - 1) | (cidx == n_all - 1)

        def conv_strip(j):
            sl = slice(j * gw, (j + 1) * gw)
            xc = xbc_ref[0, :, sl]
            prev_row = jnp.where(is_first, 0.0, hp_ref[0, SUBLANE - 1:SUBLANE, sl])
            next_row = jnp.where(is_last, 0.0, hn_ref[0, 0:1, sl])
            x_prev = jnp.where(rows == 0, prev_row, pltpu.roll(xc, 1, 0))
            x_next = jnp.where(rows == L - 1, next_row, pltpu.roll(xc, L - 1, 0))
            u = _silu(x_prev * cw_ref[0:1, sl] + xc * cw_ref[1:2, sl] + x_next * cw_ref[2:3, sl]
                      + cb_ref[:, sl])
            if j < D_SSD // gw:
                ux_ref[cidx, :, sl] = u
            else:
                ubc_ref[cidx, :, j * gw - D_SSD:(j + 1) * gw - D_SSD] = u.astype(BF16)

        wexp = [expand(lhs_done, 1, g) for g in groups]
        tots = [expand(lhs_done[0:SUBLANE], 3, g)[0:1] for g in groups]
        conv_strip(0)
        conv_strip(1)
        xws = [(ux_ref[done, :, gsl[g]] * wexp[g]).astype(BF16) for g in groups]
        locs = [lax.dot_general(ubc_ref[done, :, g * D_STATE:(g + 1) * D_STATE], xws[g],
                                (((0,), (0,)), ((), ())), preferred_element_type=F32) for g in groups]
        conv_strip(2)
        conv_strip(3)
        conv_strip(4)
        for g in groups:
            new = sb_ref[:, gsl[g]] * tots[g] + locs[g]
            sb_ref[:, gsl[g]] = new
            sin_ref[lat_idx, :, gsl[g]] = new.astype(BF16)
        conv_strip(5)
        conv_strip(6)
        conv_strip(7)

        dtr = dt_ref[0] + dtb_ref[...]
        dt = jnp.maximum(dtr, 0.0) + jnp.log(1.0 + jnp.exp(-jnp.abs(dtr)))
        a = dt * (-jnp.exp(alog_ref[...]))
        pre, suf = a, a
        d = 1
        while d < L:
            pre = pre + jnp.where(rows >= d, pltpu.roll(pre, d, 0), 0.0)
            suf = suf + jnp.where(rows < L - d, pltpu.roll(suf, L - d, 0), 0.0)
            d *= 2
        is_fwd_lane = lane < SSD_HEADS
        cs = jnp.where(is_fwd_lane, pre, suf)
        tot = jnp.where(is_fwd_lane, cs[L - 1:L, :], cs[0:1, :])
        e = jnp.exp(cs)
        wdt = jnp.exp(tot - cs) * dt
        pack_ref[cidx] = jnp.where(lane < 2 * SSD_HEADS, cs * LOG2E, pltpu.roll(dt, 2 * SSD_HEADS, 1))

        fac = jnp.where(lane < 2 * SSD_HEADS, wdt,
                        jnp.where(lane < 4 * SSD_HEADS, pltpu.roll(e, 2 * SSD_HEADS, 1), 0.0))
        hi = fac.astype(BF16).astype(F32)
        r1 = fac - hi
        mid = r1.astype(BF16).astype(F32)
        lo = (r1 - mid).astype(BF16)
        lhs_ref[cidx] = jnp.concatenate([(hi + pltpu.roll(mid, LANE // 2, 1)).astype(BF16), lo], axis=1)

    @pl.when(sweep1 & jnp.logical_not(is_lat))
    def _():
        lhs_ex = lhs_ref[cidx]
        for g in range(SSD_GROUPS):
            update_state(sf_ref, cidx, lhs_ex, g, 0, 2, L - 1)

    @pl.when(sweep1 & is_lat)
    def _():
        lhs_ex = lhs_ref[cidx]
        pack = pack_ref[cidx]
        pack_t = pack.T
        li = lax.broadcasted_iota(jnp.int32, (L, L), 0)
        si = lax.broadcasted_iota(jnp.int32, (L, L), 1)
        masks = [si <= li, si >= li]
        head_of_lane = lax.broadcasted_iota(jnp.int32, (1, gw), 1) // SSD_HEAD_DIM
        groups = range(SSD_GROUPS)
        gsl = [slice(g * gw, (g + 1) * gw) for g in groups]

        def early(g):
            bg = ubc_ref[cidx, :, g * D_STATE:(g + 1) * D_STATE]
            cg = ubc_ref[cidx, :, D_BC + g * D_STATE:D_BC + (g + 1) * D_STATE]
            cb = lax.dot_general(cg, bg, (((1,), (1,)), ((), ())), preferred_element_type=F32)
            cbm = [jnp.where(m, cb, 0.0) for m in masks]
            y_off = (jnp.dot(cg, sf_ref[:, gsl[g]].astype(BF16), preferred_element_type=F32)
                     * expand(lhs_ex, 2, g)
                     + jnp.dot(cg, sin_ref[lat_idx, :, gsl[g]], preferred_element_type=F32)
                     * expand(lhs_ex, 3, g))
            update_state(sf_ref, cidx, lhs_ex, g, 0, 2, L - 1)
            return cbm, y_off

        def main(g, cbm):
            xg = ux_ref[cidx, :, gsl[g]].astype(BF16)
            blockdiag = jnp.concatenate(
                [jnp.where(head_of_lane == hh, xg, jnp.zeros_like(xg)) for hh in range(SSD_HPG)], axis=0)
            y_diag = None
            for dirn in range(2):
                ws = []
                for hh in range(SSD_HPG):
                    ln = dirn * SSD_HEADS + g * SSD_HPG + hh
                    col = pack[:, ln:ln + 1]
                    row = pack_t[ln:ln + 1, :]
                    dtrow = pack_t[2 * SSD_HEADS + ln:2 * SSD_HEADS + ln + 1, :]
                    dec = jnp.exp2(jnp.minimum(col - row, 0.0))
                    ws.append((cbm[dirn] * dec * dtrow).astype(BF16))
                yd = jnp.dot(jnp.concatenate(ws, axis=1), blockdiag, preferred_element_type=F32)
                y_diag = yd if y_diag is None else y_diag + yd
            return y_diag

        def finish(g, y_off, y_diag):
            yg = ux_ref[cidx, :, gsl[g]] * dsk_ref[:, gsl[g]] + y_diag + y_off
            yz = yg * _silu(z_ref[0, :, gsl[g]])
            y_ref[:, gsl[g]] = yz
            return jnp.sum(yz * yz, axis=-1, keepdims=True)

        pending = {0: early(0)}
        diag = {}
        ss = jnp.zeros((L, 1), F32)
        for g in groups:
            if g + 1 < SSD_GROUPS:
                pending[g + 1] = early(g + 1)
            diag[g] = main(g, pending[g][0])
            if g >= 1:
                ss = ss + finish(g - 1, pending[g - 1][1], diag[g - 1])
        ss = ss + finish(SSD_GROUPS - 1, pending[SSD_GROUPS - 1][1], diag[SSD_GROUPS - 1])
        rs = lax.rsqrt(ss * (1.0 / D_SSD) + EPS)
        for g in groups:
            o_ref[0, :, gsl[g]] = (y_ref[:, gsl[g]] * rs * ng_ref[:, gsl[g]]).astype(o_ref.dtype)


def _expansion_matrix():
    r = jnp.arange(2 * LANE)[:, None]
    col = jnp.arange(4 * D_SSD)[None, :]
    src = (col // D_SSD) * SSD_HEADS + (col % D_SSD) // SSD_HEAD_DIM
    half = LANE // 2
    return ((r % half == src) & (r < 3 * half)).astype(BF16)


def _ssd(gzx, dtraw, conv_w, conv_b, dt_bias, a_log, d_skip_exp, ssd_norm_g, s, nc):
    b, t_all, _ = gzx.shape
    assert s % CHUNK == 0 and nc % CHUNK == 0
    n_lat, n_ctx = s // CHUNK, nc // CHUNK
    n_all = n_lat + n_ctx
    halo_per_chunk = CHUNK // SUBLANE
    n_halo = t_all // SUBLANE
    c0 = lambda t: jnp.where(t < n_all, _ssd_chunk_index(t, n_lat, n_ctx), 0)
    c1 = lambda t: jnp.where(t >= n_all + n_ctx, t - n_all - n_ctx, 0)
    xbc_col = (D_ATT + D_SSD) // D_XBC
    z_col = D_ATT // D_SSD
    const = lambda shape: pl.BlockSpec(shape, lambda bi, t: tuple(0 for _ in shape))
    return pl.pallas_call(
        functools.partial(_ssd_kernel, n_lat=n_lat, n_ctx=n_ctx),
        grid=(b, 2 * n_all),
        in_specs=[pl.BlockSpec((1, CHUNK, D_XBC), lambda bi, t: (bi, c0(t), xbc_col)),
                  pl.BlockSpec((1, SUBLANE, D_XBC),
                               lambda bi, t: (bi, jnp.maximum(c0(t) * halo_per_chunk - 1, 0), xbc_col)),
                  pl.BlockSpec((1, SUBLANE, D_XBC),
                               lambda bi, t: (bi, jnp.minimum((c0(t) + 1) * halo_per_chunk, n_halo - 1), xbc_col)),
                  pl.BlockSpec((1, CHUNK, LANE), lambda bi, t: (bi, c0(t), 0)),
                  pl.BlockSpec((1, CHUNK, D_SSD), lambda bi, t: (bi, c1(t), z_col)),
                  const((CONV_W, D_XBC)), const((1, D_XBC)), const((1, LANE)), const((1, LANE)),
                  const((1, D_SSD)), const((1, D_SSD)), const((2 * LANE, 4 * D_SSD))],
        out_specs=pl.BlockSpec((1, CHUNK, D_SSD), lambda bi, t: (bi, c1(t), 0)),
        out_shape=jax.ShapeDtypeStruct((b, s, D_SSD), BF16),
        scratch_shapes=[pltpu.VMEM((D_STATE, D_SSD), F32),
                        pltpu.VMEM((D_STATE, D_SSD), F32),
                        pltpu.VMEM((n_lat, D_STATE, D_SSD), BF16),
                        pltpu.VMEM((n_all, CHUNK, D_SSD), F32),
                        pltpu.VMEM((n_all, CHUNK, 2 * D_BC), BF16),
                        pltpu.VMEM((n_all, CHUNK, 2 * LANE), BF16),
                        pltpu.VMEM((n_all, CHUNK, LANE), F32),
                        pltpu.VMEM((CHUNK, D_SSD), F32)],
        compiler_params=_params(("arbitrary", "arbitrary"), 60),
        name="bi_ssd",
    )(gzx, gzx, gzx, dtraw, gzx, conv_w, conv_b, dt_bias, a_log, d_skip_exp, ssd_norm_g,
      _expansion_matrix())


def _out_kernel(ha_ref, hs_ref, wa_ref, ws_ref, x_ref, gate_ref, ng_ref, o_ref):
    upd = (jnp.dot(ha_ref[...], wa_ref[...], preferred_element_type=F32)
           + jnp.dot(hs_ref[...], ws_ref[...], preferred_element_type=F32))
    y = x_ref[...] + gate_ref[0] * upd
    ms = jnp.mean(y * y, axis=-1, keepdims=True)
    o_ref[...] = (y * lax.rsqrt(ms + EPS) * ng_ref[...]).astype(o_ref.dtype)


def _out_proj(h_att, h_ssd, w_out_att, w_out_ssd, x2d, gate, final_norm_g, s):
    m, d = x2d.shape
    tm = _row_tile(s, (512, 256, 128))
    tiles_per_batch = s // tm
    return pl.pallas_call(
        _out_kernel,
        grid=(m // tm,),
        in_specs=[pl.BlockSpec((tm, D_ATT), lambda i: (i, 0)),
                  pl.BlockSpec((tm, D_SSD), lambda i: (i, 0)),
                  pl.BlockSpec((D_ATT, d), lambda i: (0, 0)),
                  pl.BlockSpec((D_SSD, d), lambda i: (0, 0)),
                  pl.BlockSpec((tm, d), lambda i: (i, 0)),
                  pl.BlockSpec((1, 1, d), lambda i: (i // tiles_per_batch, 0, 0)),
                  pl.BlockSpec((1, d), lambda i: (0, 0))],
        out_specs=pl.BlockSpec((tm, d), lambda i: (i, 0)),
        out_shape=jax.ShapeDtypeStruct((m, d), x2d.dtype),
        compiler_params=_params(("parallel",), 48),
        name="out_proj",
    )(h_att, h_ssd, w_out_att, w_out_ssd, x2d, gate, final_norm_g)


def _rope_tables(s, nc):
    pos = jnp.arange(s)
    row = (pos // GRID_W).astype(F32)
    colp = (pos % GRID_W).astype(F32)
    quarter = HEAD_DIM // 4
    freq = 1.0 / (ROPE_THETA ** (jnp.arange(quarter, dtype=F32) / quarter))
    lane = jnp.arange(LANE)
    dim = lane % HEAD_DIM
    use_col = (dim // (HEAD_DIM // 2)) == 1
    r = dim % (HEAD_DIM // 2)
    first = r < quarter
    ang = jnp.where(use_col[None, :], colp[:, None], row[:, None]) * freq[r % quarter][None, :]
    cos, sin = jnp.cos(ang), jnp.sin(ang)
    sa = jnp.where(first[None, :], -sin, 0.0)
    sb = jnp.where(first[None, :], 0.0, sin)
    ident = lambda v, fill: jnp.concatenate([v, jnp.full((nc, LANE), fill, F32)], axis=0)
    return ident(cos, 1.0), ident(sa, 0.0), ident(sb, 0.0)


def kernel(x, c, ctx, c_ctx, w_mod, b_mod, norm_g, w_in, conv_w, conv_b, a_log_f, a_log_b, dt_bias_f,
           dt_bias_b, d_skip, att_norm_g, ssd_norm_g, sink, w_out, final_norm_g):
    assert w_mod.shape[0] == 1, "single-layer operation"
    b, s, d = x.shape
    nc = ctx.shape[1]
    t_all = s + nc

    pad_rows = (-(b + 1)) % SUBLANE
    cc = jnp.concatenate([c, c_ctx[None, :], jnp.zeros((pad_rows, d), F32)], axis=0)
    mod = _modulation(cc, w_mod[0], b_mod)
    shift = mod[:b + 1, :d].reshape(b + 1, 1, d)
    scale = mod[:b + 1, d:2 * d].reshape(b + 1, 1, d)
    gate = mod[:b, 2 * d:].reshape(b, 1, d)

    w = w_in[0]
    w_qkv = w[:, :D_QKV].astype(BF16)
    w_gzx = w[:, D_QKV:D_QKV + D_GZX].astype(BF16)
    w_dt = jnp.pad(w[:, D_QKV + D_GZX:], ((0, 0), (0, LANE - 2 * SSD_HEADS))).astype(BF16)
    w_out_bf = w_out[0].astype(BF16)

    xn, dtraw = _prenorm(x, ctx, shift, scale, norm_g, w_dt)
    xn2d = xn.reshape(b * t_all, d)

    cos_t, sa_t, sb_t = _rope_tables(s, nc)
    qkv = _qkv_proj(xn2d, w_qkv, cos_t, sa_t, sb_t).reshape(b, t_all, D_QKV)
    gzx = _matmul(xn2d, w_gzx, F32).reshape(b, t_all, D_GZX)

    h_att = _attention(sink[0], qkv, gzx, att_norm_g, s, nc)

    lane_pad = LANE - 2 * SSD_HEADS
    dt_bias = jnp.pad(jnp.concatenate([dt_bias_f[0], dt_bias_b[0]]), (0, lane_pad))[None, :]
    a_log = jnp.pad(jnp.concatenate([a_log_f[0], a_log_b[0]]), (0, lane_pad))[None, :]
    d_skip_exp = jnp.repeat(d_skip[0], SSD_HEAD_DIM)[None, :]
    h_ssd = _ssd(gzx, dtraw, conv_w[0], conv_b, dt_bias, a_log, d_skip_exp, ssd_norm_g, s, nc)

    out = _out_proj(h_att.reshape(b * s, D_ATT), h_ssd.reshape(b * s, D_SSD),
                    w_out_bf[:D_ATT], w_out_bf[D_ATT:], x.reshape(b * s, d), gate,
                    final_norm_g[None, :], s)
    return out.reshape(b, s, d)
```

```python
import functools

import jax
import jax.numpy as jnp
from jax import lax
from jax.experimental import pallas as pl
from jax.experimental.pallas import tpu as pltpu

F32 = jnp.float32
BF16 = jnp.bfloat16

EPS = 1e-6
GRID_W = 64
ROPE_THETA = 10000.0
ATT_HEADS = 16
ATT_KV_HEADS = 4
HEAD_DIM = 64
D_ATT = ATT_HEADS * HEAD_DIM
D_KV = ATT_KV_HEADS * HEAD_DIM
Q_PER_KV = ATT_HEADS // ATT_KV_HEADS
BLOCK = 128
SSD_HEADS = 16
SSD_HEAD_DIM = 64
D_SSD = SSD_HEADS * SSD_HEAD_DIM
SSD_GROUPS = 4
SSD_HPG = SSD_HEADS // SSD_GROUPS
D_STATE = 128
D_BC = SSD_GROUPS * D_STATE
D_XBC = D_SSD + 2 * D_BC
CONV_W = 3
CHUNK = 128
D_QKV = D_ATT + 2 * D_KV
D_GZX = D_ATT + D_SSD + D_XBC

LANE = 128
SUBLANE = 8
V7X_VMEM_BYTES = 64 * 1024 * 1024
MIB = 1024 * 1024

NEG = -1e30
LOG2E = 1.4426950408889634


def _silu(v):
    return v / (1.0 + jnp.exp(-v))


def _params(semantics, vmem_mib):
    return pltpu.CompilerParams(dimension_semantics=semantics, vmem_limit_bytes=vmem_mib * MIB)


def _row_tile(m, candidates=(1024, 512, 256, 128)):
    for t in candidates:
        if m % t == 0:
            return t
    raise ValueError(f"row count {m} not tileable")


def _mod_kernel(c_ref, w_ref, b_ref, o_ref):
    a = _silu(c_ref[...]).astype(BF16)
    o_ref[...] = jnp.dot(a, w_ref[...].astype(BF16), preferred_element_type=F32) + b_ref[...]


def _modulation(cc, w_mod, b_mod):
    rows, d = cc.shape
    n = w_mod.shape[1]
    tn = 768
    assert n % tn == 0
    return pl.pallas_call(
        _mod_kernel,
        grid=(n // tn,),
        in_specs=[pl.BlockSpec((rows, d), lambda j: (0, 0)),
                  pl.BlockSpec((d, tn), lambda j: (0, j)),
                  pl.BlockSpec((1, tn), lambda j: (0, j))],
        out_specs=pl.BlockSpec((rows, tn), lambda j: (0, j)),
        out_shape=jax.ShapeDtypeStruct((rows, n), F32),
        compiler_params=_params(("parallel",), 32),
        name="modulation",
    )(cc, w_mod, b_mod)


def _norm_qkv_kernel(x_ref, ctx_ref, shift_ref, scale_ref, g_ref, wdt_ref, w_ref, cos_ref, sa_ref, sb_ref,
                     qkv_ref, xn_ref, dt_ref, ya_ref, yb_ref, *, n_lat, tiles_per_batch, n_tiles, n_rope, n_q):
    i = pl.program_id(0)
    is_lat = (jnp.minimum(i, n_tiles - 1) % tiles_per_batch) < n_lat

    @pl.when(i == 0)
    def _():
        yb_ref[...] = jnp.zeros_like(yb_ref)

    def step(y_prev_ref, y_next_ref):
        y = y_prev_ref[...]
        dt_ref[0] = jnp.dot(y, wdt_ref[...], preferred_element_type=F32)
        acc = jnp.dot(y, w_ref[...], preferred_element_type=F32)
        cos, sa, sb = cos_ref[...], sa_ref[...], sb_ref[...]
        quarter = HEAD_DIM // 4
        for jb in range(acc.shape[1] // LANE):
            blk = acc[:, jb * LANE:(jb + 1) * LANE]
            if jb < n_rope:
                blk = (blk * cos + pltpu.roll(blk, LANE - quarter, 1) * sa
                       + pltpu.roll(blk, quarter, 1) * sb)
            if jb < n_q:
                blk = blk * (HEAD_DIM ** -0.5 * LOG2E)
            qkv_ref[0, :, jb * LANE:(jb + 1) * LANE] = blk.astype(qkv_ref.dtype)

        v = jnp.where(is_lat, x_ref[0], ctx_ref[0])
        ms = jnp.mean(v * v, axis=-1, keepdims=True)
        gain = g_ref[...] * (1.0 + scale_ref[0])
        yn = (v * lax.rsqrt(ms + EPS) * gain + shift_ref[0]).astype(BF16)
        xn_ref[0] = yn
        y_next_ref[...] = yn

    @pl.when(i % 2 == 0)
    def _():
        step(yb_ref, ya_ref)

    @pl.when(i % 2 == 1)
    def _():
        step(ya_ref, yb_ref)


def _norm_qkv(x, ctx, shift, scale, norm_g, w_dt, w_bf, cos_t, sa_t, sb_t):
    b, s, d = x.shape
    nc = ctx.shape[1]
    tr = 256
    assert s % tr == 0 and nc % tr == 0
    n_lat, n_ctx = s // tr, nc // tr
    tpb = n_lat + n_ctx
    n_tiles = b * tpb
    t_all = s + nc
    cur = lambda i: jnp.minimum(i, n_tiles - 1)
    prev = lambda i: jnp.maximum(i - 1, 0)
    bj = lambda t: (t // tpb, t % tpb)
    x_idx = lambda i: (bj(cur(i))[0], jnp.minimum(bj(cur(i))[1], n_lat - 1), 0)
    ctx_idx = lambda i: (bj(cur(i))[0], jnp.maximum(bj(cur(i))[1] - n_lat, 0), 0)
    mod_idx = lambda i: (jnp.where(bj(cur(i))[1] < n_lat, bj(cur(i))[0], b), 0, 0)
    tab = pl.BlockSpec((tr, LANE), lambda i: (bj(prev(i))[1], 0))
    out_prev = lambda width: pl.BlockSpec((1, tr, width), lambda i: (*bj(prev(i)), 0))
    const = lambda shape: pl.BlockSpec(shape, lambda i: tuple(0 for _ in shape))
    return pl.pallas_call(
        functools.partial(_norm_qkv_kernel, n_lat=n_lat, tiles_per_batch=tpb, n_tiles=n_tiles,
                          n_rope=(D_ATT + D_KV) // LANE, n_q=D_ATT // LANE),
        grid=(n_tiles + 1,),
        in_specs=[pl.BlockSpec((1, tr, d), x_idx),
                  pl.BlockSpec((1, tr, d), ctx_idx),
                  pl.BlockSpec((1, 1, d), mod_idx),
                  pl.BlockSpec((1, 1, d), mod_idx),
                  const((1, d)), const((d, LANE)), const((d, D_QKV)),
                  tab, tab, tab],
        out_specs=[out_prev(D_QKV),
                   pl.BlockSpec((1, tr, d), lambda i: (*bj(cur(i)), 0)),
                   out_prev(LANE)],
        out_shape=[jax.ShapeDtypeStruct((b, t_all, D_QKV), BF16),
                   jax.ShapeDtypeStruct((b, t_all, d), BF16),
                   jax.ShapeDtypeStruct((b, t_all, LANE), F32)],
        scratch_shapes=[pltpu.VMEM((tr, d), BF16), pltpu.VMEM((tr, d), BF16)],
        compiler_params=_params(("arbitrary",), 48),
        name="norm_qkv",
    )(x, ctx, shift, scale, norm_g, w_dt, w_bf, cos_t, sa_t, sb_t)


def _mm_kernel(x_ref, w_ref, o_ref):
    o_ref[...] = jnp.dot(x_ref[...], w_ref[...], preferred_element_type=F32).astype(o_ref.dtype)


def _matmul(x2d, w, col0, n, out_dtype, tn=512):
    m, d = x2d.shape
    tm = _row_tile(m)
    assert n % tn == 0 and col0 % tn == 0
    return pl.pallas_call(
        _mm_kernel,
        grid=(m // tm, n // tn),
        in_specs=[pl.BlockSpec((tm, d), lambda i, j: (i, 0)),
                  pl.BlockSpec((d, tn), lambda i, j: (0, col0 // tn + j))],
        out_specs=pl.BlockSpec((tm, tn), lambda i, j: (i, j)),
        out_shape=jax.ShapeDtypeStruct((m, n), out_dtype),
        compiler_params=_params(("parallel", "parallel"), 48),
        name="gzx_proj",
    )(x2d, w)


ATT_QBLOCKS = 4


def _attn_kernel(sink_ref, q_ref, kp_ref, ko_ref, kn_ref, kc_ref, vp_ref, vo_ref, vn_ref, vc_ref,
                 g_ref, ng_ref, o_ref, *, n_steps):
    n = pl.program_id(1)
    nctx = kc_ref.shape[1]
    nk = 3 * BLOCK + nctx
    cols_g = Q_PER_KV * BLOCK
    k_blocks = [kp_ref[0]] + [ko_ref[0, u * BLOCK:(u + 1) * BLOCK] for u in range(ATT_QBLOCKS)] + [kn_ref[0]]
    v_blocks = [vp_ref[0]] + [vo_ref[0, u * BLOCK:(u + 1) * BLOCK] for u in range(ATT_QBLOCKS)] + [vn_ref[0]]

    kj = lax.broadcasted_iota(jnp.int32, (BLOCK, BLOCK), 0)
    qi = lax.broadcasted_iota(jnp.int32, (BLOCK, BLOCK), 1)
    tile = lambda v: jnp.concatenate([v] * Q_PER_KV, axis=1)
    tri_prev = tile(jnp.where(kj >= qi, 0.0, NEG).astype(F32))
    tri_next = tile(jnp.where(kj <= qi, 0.0, NEG).astype(F32))
    edge_prev = tile(jnp.where((kj >= qi) & (n > 0), 0.0, NEG).astype(F32))
    edge_next = tile(jnp.where((kj <= qi) & (n < n_steps - 1), 0.0, NEG).astype(F32))
    head_of_col = lax.broadcasted_iota(jnp.int32, (1, cols_g), 1) // BLOCK
    vlane = lax.broadcasted_iota(jnp.int32, (nk, LANE - HEAD_DIM), 1)
    v_tail = jnp.where(vlane == 0, 1.0, 0.0).astype(BF16)

    def scores_t(u, kh):
        hs = slice(kh * HEAD_DIM, (kh + 1) * HEAD_DIM)
        k_h = jnp.concatenate([blk[:, hs] for blk in k_blocks[u:u + 3]] + [kc_ref[0, :, hs]], axis=0)
        qg = jnp.concatenate(
            [q_ref[0, u * BLOCK:(u + 1) * BLOCK, (kh * Q_PER_KV + i) * HEAD_DIM:(kh * Q_PER_KV + i + 1) * HEAD_DIM]
             for i in range(Q_PER_KV)], axis=0)
        return lax.dot_general(k_h, qg, (((1,), (1,)), ((), ())), preferred_element_type=F32)

    def attend(u, kh, st):
        bias_prev = edge_prev if u == 0 else tri_prev
        bias_next = edge_next if u == ATT_QBLOCKS - 1 else tri_next
        st = jnp.concatenate([st[:BLOCK] + bias_prev, st[BLOCK:2 * BLOCK],
                              st[2 * BLOCK:3 * BLOCK] + bias_next, st[3 * BLOCK:]], axis=0)
        sink_row = jnp.zeros((1, cols_g), F32)
        for i in range(Q_PER_KV):
            sink_row = jnp.where(head_of_col == i, sink_ref[kh * Q_PER_KV + i] * LOG2E, sink_row)
        m = jnp.maximum(jnp.max(st, axis=0, keepdims=True), sink_row)
        pt = jnp.exp2(st - m).astype(BF16)
        hs = slice(kh * HEAD_DIM, (kh + 1) * HEAD_DIM)
        v_h = jnp.concatenate([blk[:, hs] for blk in v_blocks[u:u + 3]] + [vc_ref[0, :, hs]], axis=0)
        v_ext = jnp.concatenate([v_h, v_tail], axis=1)
        ot = lax.dot_general(v_ext, pt, (((0,), (0,)), ((), ())), preferred_element_type=F32)
        inv = 1.0 / (ot[HEAD_DIM:HEAD_DIM + 1] + jnp.exp2(sink_row - m))
        on = ot[:HEAD_DIM] * inv
        pairs = [jnp.concatenate([on[:, (2 * j) * BLOCK:(2 * j + 1) * BLOCK],
                                  on[:, (2 * j + 1) * BLOCK:(2 * j + 2) * BLOCK]], axis=0).T
                 for j in range(Q_PER_KV // 2)]
        return jnp.concatenate(pairs, axis=1)

    def finish(u, groups):
        rows = slice(u * BLOCK, (u + 1) * BLOCK)
        a = jnp.concatenate(groups, axis=1) * _silu(g_ref[0, rows])
        ms = jnp.mean(a * a, axis=-1, keepdims=True)
        o_ref[0, rows] = (a * lax.rsqrt(ms + EPS) * ng_ref[...]).astype(o_ref.dtype)

    order = [(u, kh) for u in range(ATT_QBLOCKS) for kh in range(ATT_KV_HEADS)]
    ahead = 1
    pending = [scores_t(*order[i]) for i in range(ahead)]
    groups = []
    for idx, (u, kh) in enumerate(order):
        s_cur = pending.pop(0)
        if idx + ahead < len(order):
            pending.append(scores_t(*order[idx + ahead]))
        groups.append(attend(u, kh, s_cur))
        if kh == ATT_KV_HEADS - 1:
            finish(u, groups)
            groups = []


def _attention(sink, qkv, gzx, att_norm_g, s, nc):
    b = qkv.shape[0]
    rows = ATT_QBLOCKS * BLOCK
    assert s % rows == 0 and s % nc == 0
    nb = s // BLOCK
    n_steps = s // rows
    kcol, vcol = D_ATT // D_KV, D_ATT // D_KV + 1
    before = lambda col: pl.BlockSpec(
        (1, BLOCK, D_KV), lambda bi, n: (bi, jnp.maximum(ATT_QBLOCKS * n - 1, 0), col))
    own = lambda col: pl.BlockSpec((1, rows, D_KV), lambda bi, n: (bi, n, col))
    after = lambda col: pl.BlockSpec(
        (1, BLOCK, D_KV), lambda bi, n: (bi, jnp.minimum(ATT_QBLOCKS * (n + 1), nb - 1), col))
    ctx_spec = lambda col: pl.BlockSpec((1, nc, D_KV), lambda bi, n: (bi, s // nc, col))
    return pl.pallas_call(
        functools.partial(_attn_kernel, n_steps=n_steps),
        grid=(b, n_steps),
        in_specs=[pl.BlockSpec(memory_space=pltpu.SMEM),
                  pl.BlockSpec((1, rows, D_ATT), lambda bi, n: (bi, n, 0)),
                  before(kcol), own(kcol), after(kcol), ctx_spec(kcol),
                  before(vcol), own(vcol), after(vcol), ctx_spec(vcol),
                  pl.BlockSpec((1, rows, D_ATT), lambda bi, n: (bi, n, 0)),
                  pl.BlockSpec((1, D_ATT), lambda bi, n: (0, 0))],
        out_specs=pl.BlockSpec((1, rows, D_ATT), lambda bi, n: (bi, n, 0)),
        out_shape=jax.ShapeDtypeStruct((b, s, D_ATT), BF16),
        compiler_params=_params(("parallel", "parallel"), 40),
        name="window_attn",
    )(sink, qkv, qkv, qkv, qkv, qkv, qkv, qkv, qkv, qkv, gzx, att_norm_g)


def _ssd_chunk_index(t, n_lat, n_ctx):
    n_all = n_lat + n_ctx
    k = t - n_all
    fwd = jnp.where(k < n_ctx, n_lat + k, k - n_ctx)
    return jnp.where(t >= n_all, fwd, n_all - 1 - t)


def _ssd_kernel(xbc_ref, hp_ref, hn_ref, dt_ref, z_ref, cw_ref, cb_ref, dtb_ref, alog_ref, dsk_ref,
                ng_ref, ex_ref, o_ref, sf_ref, sb_ref, sin_ref, ux_ref, ubc_ref, lhs_ref, pack_ref, y_ref,
                *, n_lat, n_ctx):
    t = pl.program_id(1)
    n_all = n_lat + n_ctx
    cidx = _ssd_chunk_index(t, n_lat, n_ctx)
    lat_idx = jnp.minimum(cidx, n_lat - 1)
    sweep1 = t >= n_all
    is_lat = cidx < n_lat
    L = CHUNK
    gw = SSD_HPG * SSD_HEAD_DIM
    lane = lax.broadcasted_iota(jnp.int32, (1, LANE), 1)
    rows = lax.broadcasted_iota(jnp.int32, (L, 1), 0)

    @pl.when(t == 0)
    def _():
        sf_ref[...] = jnp.zeros_like(sf_ref)
        sb_ref[...] = jnp.zeros_like(sb_ref)
        ux_ref[n_all - 1] = jnp.zeros((CHUNK, D_SSD), F32)
        ubc_ref[n_all - 1] = jnp.zeros((CHUNK, 2 * D_BC), BF16)
        lhs_ref[n_all - 1] = jnp.zeros((CHUNK, 2 * LANE), BF16)

    def expand(lhs, block, g):
        c0 = block * D_SSD + g * gw
        return jnp.dot(lhs, ex_ref[:, c0:c0 + gw], preferred_element_type=F32)

    def update_state(state_ref, chunk, lhs_ex, g, wblock, eblock, end_row):
        sl = slice(g * gw, (g + 1) * gw)
        r8 = (end_row // SUBLANE) * SUBLANE
        xw = (ux_ref[chunk, :, sl] * expand(lhs_ex, wblock, g)).astype(BF16)
        bg = ubc_ref[chunk, :, g * D_STATE:(g + 1) * D_STATE]
        loc = lax.dot_general(bg, xw, (((0,), (0,)), ((), ())), preferred_element_type=F32)
        tot = expand(lhs_ex[r8:r8 + SUBLANE], eblock, g)[end_row - r8:end_row - r8 + 1]
        new = state_ref[:, sl] * tot + loc
        state_ref[:, sl] = new
        return new

    @pl.when(jnp.logical_not(sweep1))
    def _():
        done = jnp.minimum(cidx + 1, n_all - 1)
        lhs_done = lhs_ref[done]
        groups = range(SSD_GROUPS)
        gsl = [slice(g * gw, (g + 1) * gw) for g in groups]
        is_first = (cidx == 0) | (cidx == n_lat)
        is_last = (cidx == n_lat - 1) | (cidx == n_all - 1)

        def conv_strip(j):
            sl = slice(j * gw, (j + 1) * gw)
            xc = xbc_ref[0, :, sl]
            prev_row = jnp.where(is_first, 0.0, hp_ref[0, SUBLANE - 1:SUBLANE, sl])
            next_row = jnp.where(is_last, 0.0, hn_ref[0, 0:1, sl])
            x_prev = jnp.where(rows == 0, prev_row, pltpu.roll(xc, 1, 0))
            x_next = jnp.where(rows == L - 1, next_row, pltpu.roll(xc, L - 1, 0))
            u = _silu(x_prev * cw_ref[0:1, sl] + xc * cw_ref[1:2, sl] + x_next * cw_ref[2:3, sl]
                      + cb_ref[:, sl])
            if j < D_SSD // gw:
                ux_ref[cidx, :, sl] = u
            else:
                ubc_ref[cidx, :, j * gw - D_SSD:(j + 1) * gw - D_SSD] = u.astype(BF16)

        wexp = [expand(lhs_done, 1, g) for g in groups]
        tots = [expand(lhs_done[0:SUBLANE], 3, g)[0:1] for g in groups]
        conv_strip(0)
        conv_strip(1)
        xws = [(ux_ref[done, :, gsl[g]] * wexp[g]).astype(BF16) for g in groups]
        locs = [lax.dot_general(ubc_ref[done, :, g * D_STATE:(g + 1) * D_STATE], xws[g],
                                (((0,), (0,)), ((), ())), preferred_element_type=F32) for g in groups]
        conv_strip(2)
        conv_strip(3)
        conv_strip(4)
        for g in groups:
            new = sb_ref[:, gsl[g]] * tots[g] + locs[g]
            sb_ref[:, gsl[g]] = new
            sin_ref[lat_idx, :, gsl[g]] = new.astype(BF16)
        conv_strip(5)
        conv_strip(6)
        conv_strip(7)

        dtr = dt_ref[0] + dtb_ref[...]
        dt = jnp.maximum(dtr, 0.0) + jnp.log(1.0 + jnp.exp(-jnp.abs(dtr)))
        a = dt * (-jnp.exp(alog_ref[...]))
        pre, suf = a, a
        d = 1
        while d < L:
            pre = pre + jnp.where(rows >= d, pltpu.roll(pre, d, 0), 0.0)
            suf = suf + jnp.where(rows < L - d, pltpu.roll(suf, L - d, 0), 0.0)
            d *= 2
        is_fwd_lane = lane < SSD_HEADS
        cs = jnp.where(is_fwd_lane, pre, suf)
        tot = jnp.where(is_fwd_lane, cs[L - 1:L, :], cs[0:1, :])
        e = jnp.exp(cs)
        wdt = jnp.exp(tot - cs) * dt
        pack_ref[cidx] = jnp.where(lane < 2 * SSD_HEADS, cs * LOG2E, pltpu.roll(dt, 2 * SSD_HEADS, 1))

        fac = jnp.where(lane < 2 * SSD_HEADS, wdt,
                        jnp.where(lane < 4 * SSD_HEADS, pltpu.roll(e, 2 * SSD_HEADS, 1), 0.0))
        hi = fac.astype(BF16).astype(F32)
        r1 = fac - hi
        mid = r1.astype(BF16).astype(F32)
        lo = (r1 - mid).astype(BF16)
        lhs_ref[cidx] = jnp.concatenate([(hi + pltpu.roll(mid, LANE // 2, 1)).astype(BF16), lo], axis=1)

    @pl.when(sweep1 & jnp.logical_not(is_lat))
    def _():
        lhs_ex = lhs_ref[cidx]
        for g in range(SSD_GROUPS):
            update_state(sf_ref, cidx, lhs_ex, g, 0, 2, L - 1)

    @pl.when(sweep1 & is_lat)
    def _():
        lhs_ex = lhs_ref[cidx]
        pack = pack_ref[cidx]
        pack_t = pack.T
        li = lax.broadcasted_iota(jnp.int32, (L, L), 0)
        si = lax.broadcasted_iota(jnp.int32, (L, L), 1)
        masks = [si <= li, si >= li]
        head_of_lane = lax.broadcasted_iota(jnp.int32, (1, gw), 1) // SSD_HEAD_DIM
        groups = range(SSD_GROUPS)
        gsl = [slice(g * gw, (g + 1) * gw) for g in groups]

        def early(g):
            bg = ubc_ref[cidx, :, g * D_STATE:(g + 1) * D_STATE]
            cg = ubc_ref[cidx, :, D_BC + g * D_STATE:D_BC + (g + 1) * D_STATE]
            cb = lax.dot_general(cg, bg, (((1,), (1,)), ((), ())), preferred_element_type=F32)
            cbm = [jnp.where(m, cb, 0.0) for m in masks]
            y_off = (jnp.dot(cg, sf_ref[:, gsl[g]].astype(BF16), preferred_element_type=F32)
                     * expand(lhs_ex, 2, g)
                     + jnp.dot(cg, sin_ref[lat_idx, :, gsl[g]], preferred_element_type=F32)
                     * expand(lhs_ex, 3, g))
            update_state(sf_ref, cidx, lhs_ex, g, 0, 2, L - 1)
            return cbm, y_off

        def main(g, cbm):
            xg = ux_ref[cidx, :, gsl[g]].astype(BF16)
            blockdiag = jnp.concatenate(
                [jnp.where(head_of_lane == hh, xg, jnp.zeros_like(xg)) for hh in range(SSD_HPG)], axis=0)
            y_diag = None
            for dirn in range(2):
                ws = []
                for hh in range(SSD_HPG):
                    ln = dirn * SSD_HEADS + g * SSD_HPG + hh
                    col = pack[:, ln:ln + 1]
                    row = pack_t[ln:ln + 1, :]
                    dtrow = pack_t[2 * SSD_HEADS + ln:2 * SSD_HEADS + ln + 1, :]
                    dec = jnp.exp2(jnp.minimum(col - row, 0.0))
                    ws.append((cbm[dirn] * dec * dtrow).astype(BF16))
                yd = jnp.dot(jnp.concatenate(ws, axis=1), blockdiag, preferred_element_type=F32)
                y_diag = yd if y_diag is None else y_diag + yd
            return y_diag

        def finish(g, y_off, y_diag):
            yg = ux_ref[cidx, :, gsl[g]] * dsk_ref[:, gsl[g]] + y_diag + y_off
            yz = yg * _silu(z_ref[0, :, gsl[g]])
            y_ref[:, gsl[g]] = yz
            return jnp.sum(yz * yz, axis=-1, keepdims=True)

        pending = {0: early(0)}
        diag = {}
        ss = jnp.zeros((L, 1), F32)
        for g in groups:
            if g + 1 < SSD_GROUPS:
                pending[g + 1] = early(g + 1)
            diag[g] = main(g, pending[g][0])
            if g >= 1:
                ss = ss + finish(g - 1, pending[g - 1][1], diag[g - 1])
        ss = ss + finish(SSD_GROUPS - 1, pending[SSD_GROUPS - 1][1], diag[SSD_GROUPS - 1])
        rs = lax.rsqrt(ss * (1.0 / D_SSD) + EPS)
        for g in groups:
            o_ref[0, :, gsl[g]] = (y_ref[:, gsl[g]] * rs * ng_ref[:, gsl[g]]).astype(o_ref.dtype)


def _expansion_matrix():
    r = jnp.arange(2 * LANE)[:, None]
    col = jnp.arange(4 * D_SSD)[None, :]
    src = (col // D_SSD) * SSD_HEADS + (col % D_SSD) // SSD_HEAD_DIM
    half = LANE // 2
    return ((r % half == src) & (r < 3 * half)).astype(BF16)


def _ssd(gzx, dtraw, conv_w, conv_b, dt_bias, a_log, d_skip_exp, ssd_norm_g, s, nc):
    b, t_all, _ = gzx.shape
    assert s % CHUNK == 0 and nc % CHUNK == 0
    n_lat, n_ctx = s // CHUNK, nc // CHUNK
    n_all = n_lat + n_ctx
    halo_per_chunk = CHUNK // SUBLANE
    n_halo = t_all // SUBLANE
    c0 = lambda t: jnp.where(t < n_all, _ssd_chunk_index(t, n_lat, n_ctx), 0)
    c1 = lambda t: jnp.where(t >= n_all + n_ctx, t - n_all - n_ctx, 0)
    xbc_col = (D_ATT + D_SSD) // D_XBC
    z_col = D_ATT // D_SSD
    const = lambda shape: pl.BlockSpec(shape, lambda bi, t: tuple(0 for _ in shape))
    return pl.pallas_call(
        functools.partial(_ssd_kernel, n_lat=n_lat, n_ctx=n_ctx),
        grid=(b, 2 * n_all),
        in_specs=[pl.BlockSpec((1, CHUNK, D_XBC), lambda bi, t: (bi, c0(t), xbc_col)),
                  pl.BlockSpec((1, SUBLANE, D_XBC),
                               lambda bi, t: (bi, jnp.maximum(c0(t) * halo_per_chunk - 1, 0), xbc_col)),
                  pl.BlockSpec((1, SUBLANE, D_XBC),
                               lambda bi, t: (bi, jnp.minimum((c0(t) + 1) * halo_per_chunk, n_halo - 1), xbc_col)),
                  pl.BlockSpec((1, CHUNK, LANE), lambda bi, t: (bi, c0(t), 0)),
                  pl.BlockSpec((1, CHUNK, D_SSD), lambda bi, t: (bi, c1(t), z_col)),
                  const((CONV_W, D_XBC)), const((1, D_XBC)), const((1, LANE)), const((1, LANE)),
                  const((1, D_SSD)), const((1, D_SSD)), const((2 * LANE, 4 * D_SSD))],
        out_specs=pl.BlockSpec((1, CHUNK, D_SSD), lambda bi, t: (bi, c1(t), 0)),
        out_shape=jax.ShapeDtypeStruct((b, s, D_SSD), BF16),
        scratch_shapes=[pltpu.VMEM((D_STATE, D_SSD), F32),
                        pltpu.VMEM((D_STATE, D_SSD), F32),
                        pltpu.VMEM((n_lat, D_STATE, D_SSD), BF16),
                        pltpu.VMEM((n_all, CHUNK, D_SSD), F32),
                        pltpu.VMEM((n_all, CHUNK, 2 * D_BC), BF16),
                        pltpu.VMEM((n_all, CHUNK, 2 * LANE), BF16),
                        pltpu.VMEM((n_all, CHUNK, LANE), F32),
                        pltpu.VMEM((CHUNK, D_SSD), F32)],
        compiler_params=_params(("arbitrary", "arbitrary"), 60),
        name="bi_ssd",
    )(gzx, gzx, gzx, dtraw, gzx, conv_w, conv_b, dt_bias, a_log, d_skip_exp, ssd_norm_g,
      _expansion_matrix())


def _out_kernel(ha_ref, hs_ref, wa_ref, ws_ref, x_ref, gate_ref, ng_ref, o_ref):
    upd = (jnp.dot(ha_ref[...], wa_ref[...], preferred_element_type=F32)
           + jnp.dot(hs_ref[...], ws_ref[...], preferred_element_type=F32))
    y = x_ref[...] + gate_ref[0] * upd
    ms = jnp.mean(y * y, axis=-1, keepdims=True)
    o_ref[...] = (y * lax.rsqrt(ms + EPS) * ng_ref[...]).astype(o_ref.dtype)


def _out_proj(h_att, h_ssd, w_out_att, w_out_ssd, x2d, gate, final_norm_g, s):
    m, d = x2d.shape
    tm = _row_tile(s, (512, 256, 128))
    tiles_per_batch = s // tm
    return pl.pallas_call(
        _out_kernel,
        grid=(m // tm,),
        in_specs=[pl.BlockSpec((tm, D_ATT), lambda i: (i, 0)),
                  pl.BlockSpec((tm, D_SSD), lambda i: (i, 0)),
                  pl.BlockSpec((D_ATT, d), lambda i: (0, 0)),
                  pl.BlockSpec((D_SSD, d), lambda i: (0, 0)),
                  pl.BlockSpec((tm, d), lambda i: (i, 0)),
                  pl.BlockSpec((1, 1, d), lambda i: (i // tiles_per_batch, 0, 0)),
                  pl.BlockSpec((1, d), lambda i: (0, 0))],
        out_specs=pl.BlockSpec((tm, d), lambda i: (i, 0)),
        out_shape=jax.ShapeDtypeStruct((m, d), x2d.dtype),
        compiler_params=_params(("parallel",), 48),
        name="out_proj",
    )(h_att, h_ssd, w_out_att, w_out_ssd, x2d, gate, final_norm_g)


def _rope_tables(s, nc):
    pos = jnp.arange(s)
    row = (pos // GRID_W).astype(F32)
    colp = (pos % GRID_W).astype(F32)
    quarter = HEAD_DIM // 4
    freq = 1.0 / (ROPE_THETA ** (jnp.arange(quarter, dtype=F32) / quarter))
    lane = jnp.arange(LANE)
    dim = lane % HEAD_DIM
    use_col = (dim // (HEAD_DIM // 2)) == 1
    r = dim % (HEAD_DIM // 2)
    first = r < quarter
    ang = jnp.where(use_col[None, :], colp[:, None], row[:, None]) * freq[r % quarter][None, :]
    cos, sin = jnp.cos(ang), jnp.sin(ang)
    sa = jnp.where(first[None, :], -sin, 0.0)
    sb = jnp.where(first[None, :], 0.0, sin)
    ident = lambda v, fill: jnp.concatenate([v, jnp.full((nc, LANE), fill, F32)], axis=0)
    return ident(cos, 1.0), ident(sa, 0.0), ident(sb, 0.0)


def kernel(x, c, ctx, c_ctx, w_mod, b_mod, norm_g, w_in, conv_w, conv_b, a_log_f, a_log_b, dt_bias_f,
           dt_bias_b, d_skip, att_norm_g, ssd_norm_g, sink, w_out, final_norm_g):
    assert w_mod.shape[0] == 1, "single-layer operation"
    b, s, d = x.shape
    nc = ctx.shape[1]
    t_all = s + nc

    pad_rows = (-(b + 1)) % SUBLANE
    cc = jnp.concatenate([c, c_ctx[None, :], jnp.zeros((pad_rows, d), F32)], axis=0)
    mod = _modulation(cc, w_mod[0], b_mod)
    shift = mod[:b + 1, :d].reshape(b + 1, 1, d)
    scale = mod[:b + 1, d:2 * d].reshape(b + 1, 1, d)
    gate = mod[:b, 2 * d:].reshape(b, 1, d)

    w_bf = w_in[0].astype(BF16)
    w_dt = jnp.pad(w_bf[:, D_QKV + D_GZX:], ((0, 0), (0, LANE - 2 * SSD_HEADS)))
    w_out_bf = w_out[0].astype(BF16)

    cos_t, sa_t, sb_t = _rope_tables(s, nc)
    qkv, xn, dtraw = _norm_qkv(x, ctx, shift, scale, norm_g, w_dt, w_bf, cos_t, sa_t, sb_t)

    gzx = _matmul(xn.reshape(b * t_all, d), w_bf, D_QKV, D_GZX, F32).reshape(b, t_all, D_GZX)

    h_att = _attention(sink[0], qkv, gzx, att_norm_g, s, nc)

    lane_pad = LANE - 2 * SSD_HEADS
    dt_bias = jnp.pad(jnp.concatenate([dt_bias_f[0], dt_bias_b[0]]), (0, lane_pad))[None, :]
    a_log = jnp.pad(jnp.concatenate([a_log_f[0], a_log_b[0]]), (0, lane_pad))[None, :]
    d_skip_exp = jnp.repeat(d_skip[0], SSD_HEAD_DIM)[None, :]
    h_ssd = _ssd(gzx, dtraw, conv_w[0], conv_b, dt_bias, a_log, d_skip_exp, ssd_norm_g, s, nc)

    out = _out_proj(h_att.reshape(b * s, D_ATT), h_ssd.reshape(b * s, D_SSD),
                    w_out_bf[:D_ATT], w_out_bf[D_ATT:], x.reshape(b * s, d), gate,
                    final_norm_g[None, :], s)
    return out.reshape(b, s, d)
```

```python
import functools

import jax
import jax.numpy as jnp
from jax import lax
from jax.experimental import pallas as pl
from jax.experimental.pallas import tpu as pltpu

F32 = jnp.float32
BF16 = jnp.bfloat16

EPS = 1e-6
GRID_W = 64
ROPE_THETA = 10000.0
ATT_HEADS = 16
ATT_KV_HEADS = 4
HEAD_DIM = 64
D_ATT = ATT_HEADS * HEAD_DIM
D_KV = ATT_KV_HEADS * HEAD_DIM
Q_PER_KV = ATT_HEADS // ATT_KV_HEADS
BLOCK = 128
SSD_HEADS = 16
SSD_HEAD_DIM = 64
D_SSD = SSD_HEADS * SSD_HEAD_DIM
SSD_GROUPS = 4
SSD_HPG = SSD_HEADS // SSD_GROUPS
D_STATE = 128
D_BC = SSD_GROUPS * D_STATE
D_XBC = D_SSD + 2 * D_BC
CONV_W = 3
CHUNK = 128
D_QKV = D_ATT + 2 * D_KV
D_GZX = D_ATT + D_SSD + D_XBC

LANE = 128
SUBLANE = 8
V7X_VMEM_BYTES = 64 * 1024 * 1024
MIB = 1024 * 1024

NEG = -1e30
LOG2E = 1.4426950408889634


def _silu(v):
    return v / (1.0 + jnp.exp(-v))


def _params(semantics, vmem_mib):
    return pltpu.CompilerParams(dimension_semantics=semantics, vmem_limit_bytes=vmem_mib * MIB)


def _row_tile(m, candidates=(1024, 512, 256, 128)):
    for t in candidates:
        if m % t == 0:
            return t
    raise ValueError(f"row count {m} not tileable")


def _mod_kernel(c_ref, w_ref, b_ref, o_ref):
    a = _silu(c_ref[...]).astype(BF16)
    o_ref[...] = jnp.dot(a, w_ref[...].astype(BF16), preferred_element_type=F32) + b_ref[...]


def _modulation(cc, w_mod, b_mod):
    rows, d = cc.shape
    n = w_mod.shape[1]
    tn = 768
    assert n % tn == 0
    return pl.pallas_call(
        _mod_kernel,
        grid=(n // tn,),
        in_specs=[pl.BlockSpec((rows, d), lambda j: (0, 0)),
                  pl.BlockSpec((d, tn), lambda j: (0, j)),
                  pl.BlockSpec((1, tn), lambda j: (0, j))],
        out_specs=pl.BlockSpec((rows, tn), lambda j: (0, j)),
        out_shape=jax.ShapeDtypeStruct((rows, n), F32),
        compiler_params=_params(("parallel",), 32),
        name="modulation",
    )(cc, w_mod, b_mod)


def _norm_qkv_kernel(x_ref, ctx_ref, shift_ref, scale_ref, g_ref, wdt_ref, w_ref, cos_ref, sa_ref, sb_ref,
                     qkv_ref, xn_ref, dt_ref, ya_ref, yb_ref, *, n_lat_tiles, n_tiles, n_rope, n_q):
    i = pl.program_id(0)
    is_lat = jnp.minimum(i, n_tiles - 1) < n_lat_tiles

    @pl.when(i == 0)
    def _():
        yb_ref[...] = jnp.zeros_like(yb_ref)

    def step(y_prev_ref, y_next_ref):
        y = y_prev_ref[...]
        dt_ref[...] = jnp.dot(y, wdt_ref[...], preferred_element_type=F32)
        acc = jnp.dot(y, w_ref[...], preferred_element_type=F32)
        cos, sa, sb = cos_ref[...], sa_ref[...], sb_ref[...]
        quarter = HEAD_DIM // 4
        for jb in range(acc.shape[1] // LANE):
            blk = acc[:, jb * LANE:(jb + 1) * LANE]
            if jb < n_rope:
                blk = (blk * cos + pltpu.roll(blk, LANE - quarter, 1) * sa
                       + pltpu.roll(blk, quarter, 1) * sb)
            if jb < n_q:
                blk = blk * (HEAD_DIM ** -0.5 * LOG2E)
            qkv_ref[:, jb * LANE:(jb + 1) * LANE] = blk.astype(qkv_ref.dtype)

        v = jnp.where(is_lat, x_ref[...], ctx_ref[...])
        ms = jnp.mean(v * v, axis=-1, keepdims=True)
        gain = g_ref[...] * (1.0 + scale_ref[0])
        yn = (v * lax.rsqrt(ms + EPS) * gain + shift_ref[0]).astype(BF16)
        xn_ref[...] = yn
        y_next_ref[...] = yn

    @pl.when(i % 2 == 0)
    def _():
        step(yb_ref, ya_ref)

    @pl.when(i % 2 == 1)
    def _():
        step(ya_ref, yb_ref)


NORM_TILE = 512


def _norm_qkv(x2d, ctx2d, shift, scale, norm_g, w_dt, w_bf, cos_t, sa_t, sb_t, s):
    m_lat, d = x2d.shape
    m_ctx = ctx2d.shape[0]
    b = m_lat // s
    tr = NORM_TILE
    assert s % tr == 0 and m_ctx % tr == 0
    tiles_per_seq = s // tr
    n_lat_tiles, n_ctx_tiles = m_lat // tr, m_ctx // tr
    n_tiles = n_lat_tiles + n_ctx_tiles
    cur = lambda i: jnp.minimum(i, n_tiles - 1)
    prev = lambda i: jnp.maximum(i - 1, 0)
    mod_idx = lambda i: (jnp.where(cur(i) < n_lat_tiles, cur(i) // tiles_per_seq, b), 0, 0)
    tab = pl.BlockSpec((tr, LANE), lambda i: (jnp.where(prev(i) < n_lat_tiles, prev(i) % tiles_per_seq,
                                                        tiles_per_seq), 0))
    out_prev = lambda width: pl.BlockSpec((tr, width), lambda i: (prev(i), 0))
    const = lambda shape: pl.BlockSpec(shape, lambda i: tuple(0 for _ in shape))
    return pl.pallas_call(
        functools.partial(_norm_qkv_kernel, n_lat_tiles=n_lat_tiles, n_tiles=n_tiles,
                          n_rope=(D_ATT + D_KV) // LANE, n_q=D_ATT // LANE),
        grid=(n_tiles + 1,),
        in_specs=[pl.BlockSpec((tr, d), lambda i: (jnp.minimum(cur(i), n_lat_tiles - 1), 0)),
                  pl.BlockSpec((tr, d), lambda i: (jnp.maximum(cur(i) - n_lat_tiles, 0), 0)),
                  pl.BlockSpec((1, 1, d), mod_idx),
                  pl.BlockSpec((1, 1, d), mod_idx),
                  const((1, d)), const((d, LANE)), const((d, D_QKV)),
                  tab, tab, tab],
        out_specs=[out_prev(D_QKV),
                   pl.BlockSpec((tr, d), lambda i: (cur(i), 0)),
                   out_prev(LANE)],
        out_shape=[jax.ShapeDtypeStruct((m_lat + m_ctx, D_QKV), BF16),
                   jax.ShapeDtypeStruct((m_lat + m_ctx, d), BF16),
                   jax.ShapeDtypeStruct((m_lat + m_ctx, LANE), F32)],
        scratch_shapes=[pltpu.VMEM((tr, d), BF16), pltpu.VMEM((tr, d), BF16)],
        compiler_params=_params(("arbitrary",), 56),
        name="norm_qkv",
    )(x2d, ctx2d, shift, scale, norm_g, w_dt, w_bf, cos_t, sa_t, sb_t)


def _mm_kernel(x_ref, w_ref, o_ref):
    o_ref[...] = jnp.dot(x_ref[...], w_ref[...], preferred_element_type=F32).astype(o_ref.dtype)


def _matmul(x2d, w, col0, n, out_dtype, tn=1024):
    m, d = x2d.shape
    tm = _row_tile(m)
    assert n % tn == 0 and col0 % LANE == 0
    return pl.pallas_call(
        _mm_kernel,
        grid=(n // tn, m // tm),
        in_specs=[pl.BlockSpec((tm, d), lambda j, i: (i, 0)),
                  pl.BlockSpec((pl.Element(d), pl.Element(tn)),
                               lambda j, i: (0, pl.multiple_of(col0 + j * tn, LANE)))],
        out_specs=pl.BlockSpec((tm, tn), lambda j, i: (i, j)),
        out_shape=jax.ShapeDtypeStruct((m, n), out_dtype),
        compiler_params=_params(("parallel", "parallel"), 48),
        name="gzx_proj",
    )(x2d, w)


ATT_QBLOCKS = 4


def _attn_kernel(sink_ref, q_ref, kp_ref, ko_ref, kn_ref, kc_ref, vp_ref, vo_ref, vn_ref, vc_ref,
                 g_ref, ng_ref, o_ref, *, n_steps):
    n = pl.program_id(1)
    nctx = kc_ref.shape[0]
    nk = 3 * BLOCK + nctx
    cols_g = Q_PER_KV * BLOCK
    k_blocks = [kp_ref[...]] + [ko_ref[u * BLOCK:(u + 1) * BLOCK] for u in range(ATT_QBLOCKS)] + [kn_ref[...]]
    v_blocks = [vp_ref[...]] + [vo_ref[u * BLOCK:(u + 1) * BLOCK] for u in range(ATT_QBLOCKS)] + [vn_ref[...]]

    kj = lax.broadcasted_iota(jnp.int32, (BLOCK, BLOCK), 0)
    qi = lax.broadcasted_iota(jnp.int32, (BLOCK, BLOCK), 1)
    tile = lambda v: jnp.concatenate([v] * Q_PER_KV, axis=1)
    tri_prev = tile(jnp.where(kj >= qi, 0.0, NEG).astype(F32))
    tri_next = tile(jnp.where(kj <= qi, 0.0, NEG).astype(F32))
    edge_prev = tile(jnp.where((kj >= qi) & (n > 0), 0.0, NEG).astype(F32))
    edge_next = tile(jnp.where((kj <= qi) & (n < n_steps - 1), 0.0, NEG).astype(F32))
    head_of_col = lax.broadcasted_iota(jnp.int32, (1, cols_g), 1) // BLOCK
    vlane = lax.broadcasted_iota(jnp.int32, (nk, LANE - HEAD_DIM), 1)
    v_tail = jnp.where(vlane == 0, 1.0, 0.0).astype(BF16)

    def scores_t(u, kh):
        hs = slice(kh * HEAD_DIM, (kh + 1) * HEAD_DIM)
        k_h = jnp.concatenate([blk[:, hs] for blk in k_blocks[u:u + 3]] + [kc_ref[:, hs]], axis=0)
        qg = jnp.concatenate(
            [q_ref[u * BLOCK:(u + 1) * BLOCK, (kh * Q_PER_KV + i) * HEAD_DIM:(kh * Q_PER_KV + i + 1) * HEAD_DIM]
             for i in range(Q_PER_KV)], axis=0)
        return lax.dot_general(k_h, qg, (((1,), (1,)), ((), ())), preferred_element_type=F32)

    def attend(u, kh, st):
        bias_prev = edge_prev if u == 0 else tri_prev
        bias_next = edge_next if u == ATT_QBLOCKS - 1 else tri_next
        st = jnp.concatenate([st[:BLOCK] + bias_prev, st[BLOCK:2 * BLOCK],
                              st[2 * BLOCK:3 * BLOCK] + bias_next, st[3 * BLOCK:]], axis=0)
        sink_row = jnp.zeros((1, cols_g), F32)
        for i in range(Q_PER_KV):
            sink_row = jnp.where(head_of_col == i, sink_ref[kh * Q_PER_KV + i] * LOG2E, sink_row)
        m = jnp.maximum(jnp.max(st, axis=0, keepdims=True), sink_row)
        pt = jnp.exp2(st - m).astype(BF16)
        hs = slice(kh * HEAD_DIM, (kh + 1) * HEAD_DIM)
        v_h = jnp.concatenate([blk[:, hs] for blk in v_blocks[u:u + 3]] + [vc_ref[:, hs]], axis=0)
        v_ext = jnp.concatenate([v_h, v_tail], axis=1)
        ot = lax.dot_general(v_ext, pt, (((0,), (0,)), ((), ())), preferred_element_type=F32)
        inv = 1.0 / (ot[HEAD_DIM:HEAD_DIM + 1] + jnp.exp2(sink_row - m))
        on = ot[:HEAD_DIM] * inv
        pairs = [jnp.concatenate([on[:, (2 * j) * BLOCK:(2 * j + 1) * BLOCK],
                                  on[:, (2 * j + 1) * BLOCK:(2 * j + 2) * BLOCK]], axis=0).T
                 for j in range(Q_PER_KV // 2)]
        return jnp.concatenate(pairs, axis=1)

    def finish(u, groups):
        rows = slice(u * BLOCK, (u + 1) * BLOCK)
        a = jnp.concatenate(groups, axis=1) * _silu(g_ref[rows])
        ms = jnp.mean(a * a, axis=-1, keepdims=True)
        o_ref[rows] = (a * lax.rsqrt(ms + EPS) * ng_ref[...]).astype(o_ref.dtype)

    order = [(u, kh) for u in range(ATT_QBLOCKS) for kh in range(ATT_KV_HEADS)]
    ahead = 1
    pending = [scores_t(*order[i]) for i in range(ahead)]
    groups = []
    for idx, (u, kh) in enumerate(order):
        s_cur = pending.pop(0)
        if idx + ahead < len(order):
            pending.append(scores_t(*order[idx + ahead]))
        groups.append(attend(u, kh, s_cur))
        if kh == ATT_KV_HEADS - 1:
            finish(u, groups)
            groups = []


def _attention(sink, qkv, gzx, att_norm_g, b, s, nc):
    rows = ATT_QBLOCKS * BLOCK
    assert s % rows == 0 and (b * s) % nc == 0
    nb = s // BLOCK
    n_steps = s // rows
    kcol, vcol = D_ATT // D_KV, D_ATT // D_KV + 1
    before = lambda col: pl.BlockSpec(
        (BLOCK, D_KV), lambda bi, n: (bi * nb + jnp.maximum(ATT_QBLOCKS * n - 1, 0), col))
    own = lambda col: pl.BlockSpec((rows, D_KV), lambda bi, n: (bi * n_steps + n, col))
    after = lambda col: pl.BlockSpec(
        (BLOCK, D_KV), lambda bi, n: (bi * nb + jnp.minimum(ATT_QBLOCKS * (n + 1), nb - 1), col))
    ctx_spec = lambda col: pl.BlockSpec((nc, D_KV), lambda bi, n: (b * s // nc + bi, col))
    q_rows = pl.BlockSpec((rows, D_ATT), lambda bi, n: (bi * n_steps + n, 0))
    return pl.pallas_call(
        functools.partial(_attn_kernel, n_steps=n_steps),
        grid=(b, n_steps),
        in_specs=[pl.BlockSpec(memory_space=pltpu.SMEM),
                  q_rows,
                  before(kcol), own(kcol), after(kcol), ctx_spec(kcol),
                  before(vcol), own(vcol), after(vcol), ctx_spec(vcol),
                  q_rows,
                  pl.BlockSpec((1, D_ATT), lambda bi, n: (0, 0))],
        out_specs=q_rows,
        out_shape=jax.ShapeDtypeStruct((b * s, D_ATT), BF16),
        compiler_params=_params(("parallel", "parallel"), 40),
        name="window_attn",
    )(sink, qkv, qkv, qkv, qkv, qkv, qkv, qkv, qkv, qkv, gzx, att_norm_g)


def _ssd_chunk_index(t, n_lat, n_ctx):
    n_all = n_lat + n_ctx
    k = t - n_all
    fwd = jnp.where(k < n_ctx, n_lat + k, k - n_ctx)
    return jnp.where(t >= n_all, fwd, n_all - 1 - t)


def _ssd_kernel(xbc_ref, hp_ref, hn_ref, dt_ref, z_ref, cw_ref, cb_ref, dtb_ref, alog_ref, dsk_ref,
                ng_ref, ex_ref, o_ref, sf_ref, sb_ref, sin_ref, ux_ref, ubc_ref, lhs_ref, pack_ref, y_ref,
                *, n_lat, n_ctx):
    t = pl.program_id(1)
    n_all = n_lat + n_ctx
    cidx = _ssd_chunk_index(t, n_lat, n_ctx)
    lat_idx = jnp.minimum(cidx, n_lat - 1)
    sweep1 = t >= n_all
    is_lat = cidx < n_lat
    L = CHUNK
    gw = SSD_HPG * SSD_HEAD_DIM
    lane = lax.broadcasted_iota(jnp.int32, (1, LANE), 1)
    rows = lax.broadcasted_iota(jnp.int32, (L, 1), 0)

    @pl.when(t == 0)
    def _():
        sf_ref[...] = jnp.zeros_like(sf_ref)
        sb_ref[...] = jnp.zeros_like(sb_ref)
        ux_ref[n_all - 1] = jnp.zeros((CHUNK, D_SSD), F32)
        ubc_ref[n_all - 1] = jnp.zeros((CHUNK, 2 * D_BC), BF16)
        lhs_ref[n_all - 1] = jnp.zeros((CHUNK, 2 * LANE), BF16)

    def expand(lhs, block, g):
        c0 = block * D_SSD + g * gw
        return jnp.dot(lhs, ex_ref[:, c0:c0 + gw], preferred_element_type=F32)

    def update_state(state_ref, chunk, lhs_ex, g, wblock, eblock, end_row):
        sl = slice(g * gw, (g + 1) * gw)
        r8 = (end_row // SUBLANE) * SUBLANE
        xw = (ux_ref[chunk, :, sl] * expand(lhs_ex, wblock, g)).astype(BF16)
        bg = ubc_ref[chunk, :, g * D_STATE:(g + 1) * D_STATE]
        loc = lax.dot_general(bg, xw, (((0,), (0,)), ((), ())), preferred_element_type=F32)
        tot = expand(lhs_ex[r8:r8 + SUBLANE], eblock, g)[end_row - r8:end_row - r8 + 1]
        new = state_ref[:, sl] * tot + loc
        state_ref[:, sl] = new
        return new

    @pl.when(jnp.logical_not(sweep1))
    def _():
        done = jnp.minimum(cidx + 1, n_all - 1)
        lhs_done = lhs_ref[done]
        groups = range(SSD_GROUPS)
        gsl = [slice(g * gw, (g + 1) * gw) for g in groups]
        is_first = (cidx == 0) | (cidx == n_lat)
        is_last = (cidx == n_lat - 1) | (cidx == n_all - 1)

        def conv_strip(j):
            sl = slice(j * gw, (j + 1) * gw)
            xc = xbc_ref[:, sl]
            prev_row = jnp.where(is_first, 0.0, hp_ref[SUBLANE - 1:SUBLANE, sl])
            next_row = jnp.where(is_last, 0.0, hn_ref[0:1, sl])
            x_prev = jnp.where(rows == 0, prev_row, pltpu.roll(xc, 1, 0))
            x_next = jnp.where(rows == L - 1, next_row, pltpu.roll(xc, L - 1, 0))
            u = _silu(x_prev * cw_ref[0:1, sl] + xc * cw_ref[1:2, sl] + x_next * cw_ref[2:3, sl]
                      + cb_ref[:, sl])
            if j < D_SSD // gw:
                ux_ref[cidx, :, sl] = u
            else:
                ubc_ref[cidx, :, j * gw - D_SSD:(j + 1) * gw - D_SSD] = u.astype(BF16)

        wexp = [expand(lhs_done, 1, g) for g in groups]
        tots = [expand(lhs_done[0:SUBLANE], 3, g)[0:1] for g in groups]
        conv_strip(0)
        conv_strip(1)
        xws = [(ux_ref[done, :, gsl[g]] * wexp[g]).astype(BF16) for g in groups]
        locs = [lax.dot_general(ubc_ref[done, :, g * D_STATE:(g + 1) * D_STATE], xws[g],
                                (((0,), (0,)), ((), ())), preferred_element_type=F32) for g in groups]
        conv_strip(2)
        conv_strip(3)
        conv_strip(4)
        for g in groups:
            new = sb_ref[:, gsl[g]] * tots[g] + locs[g]
            sb_ref[:, gsl[g]] = new
            sin_ref[lat_idx, :, gsl[g]] = new.astype(BF16)
        conv_strip(5)
        conv_strip(6)
        conv_strip(7)

        dtr = dt_ref[...] + dtb_ref[...]
        dt = jnp.maximum(dtr, 0.0) + jnp.log(1.0 + jnp.exp(-jnp.abs(dtr)))
        a = dt * (-jnp.exp(alog_ref[...]))
        pre, suf = a, a
        d = 1
        while d < L:
            pre = pre + jnp.where(rows >= d, pltpu.roll(pre, d, 0), 0.0)
            suf = suf + jnp.where(rows < L - d, pltpu.roll(suf, L - d, 0), 0.0)
            d *= 2
        is_fwd_lane = lane < SSD_HEADS
        cs = jnp.where(is_fwd_lane, pre, suf)
        tot = jnp.where(is_fwd_lane, cs[L - 1:L, :], cs[0:1, :])
        e = jnp.exp(cs)
        wdt = jnp.exp(tot - cs) * dt
        pack_ref[cidx] = jnp.where(lane < 2 * SSD_HEADS, cs * LOG2E, pltpu.roll(dt, 2 * SSD_HEADS, 1))

        fac = jnp.where(lane < 2 * SSD_HEADS, wdt,
                        jnp.where(lane < 4 * SSD_HEADS, pltpu.roll(e, 2 * SSD_HEADS, 1), 0.0))
        hi = fac.astype(BF16).astype(F32)
        r1 = fac - hi
        mid = r1.astype(BF16).astype(F32)
        lo = (r1 - mid).astype(BF16)
        lhs_ref[cidx] = jnp.concatenate([(hi + pltpu.roll(mid, LANE // 2, 1)).astype(BF16), lo], axis=1)

    @pl.when(sweep1 & jnp.logical_not(is_lat))
    def _():
        lhs_ex = lhs_ref[cidx]
        for g in range(SSD_GROUPS):
            update_state(sf_ref, cidx, lhs_ex, g, 0, 2, L - 1)

    @pl.when(sweep1 & is_lat)
    def _():
        lhs_ex = lhs_ref[cidx]
        pack = pack_ref[cidx]
        pack_t = pack.T
        li = lax.broadcasted_iota(jnp.int32, (L, L), 0)
        si = lax.broadcasted_iota(jnp.int32, (L, L), 1)
        masks = [si <= li, si >= li]
        head_of_lane = lax.broadcasted_iota(jnp.int32, (1, gw), 1) // SSD_HEAD_DIM
        groups = range(SSD_GROUPS)
        gsl = [slice(g * gw, (g + 1) * gw) for g in groups]

        def early(g):
            bg = ubc_ref[cidx, :, g * D_STATE:(g + 1) * D_STATE]
            cg = ubc_ref[cidx, :, D_BC + g * D_STATE:D_BC + (g + 1) * D_STATE]
            cb = lax.dot_general(cg, bg, (((1,), (1,)), ((), ())), preferred_element_type=F32)
            cbm = [jnp.where(m, cb, 0.0) for m in masks]
            y_off = (jnp.dot(cg, sf_ref[:, gsl[g]].astype(BF16), preferred_element_type=F32)
                     * expand(lhs_ex, 2, g)
                     + jnp.dot(cg, sin_ref[lat_idx, :, gsl[g]], preferred_element_type=F32)
                     * expand(lhs_ex, 3, g))
            update_state(sf_ref, cidx, lhs_ex, g, 0, 2, L - 1)
            return cbm, y_off

        def main(g, cbm):
            xg = ux_ref[cidx, :, gsl[g]].astype(BF16)
            blockdiag = jnp.concatenate(
                [jnp.where(head_of_lane == hh, xg, jnp.zeros_like(xg)) for hh in range(SSD_HPG)], axis=0)
            y_diag = None
            for dirn in range(2):
                ws = []
                for hh in range(SSD_HPG):
                    ln = dirn * SSD_HEADS + g * SSD_HPG + hh
                    col = pack[:, ln:ln + 1]
                    row = pack_t[ln:ln + 1, :]
                    dtrow = pack_t[2 * SSD_HEADS + ln:2 * SSD_HEADS + ln + 1, :]
                    dec = jnp.exp2(jnp.minimum(col - row, 0.0))
                    ws.append((cbm[dirn] * dec * dtrow).astype(BF16))
                yd = jnp.dot(jnp.concatenate(ws, axis=1), blockdiag, preferred_element_type=F32)
                y_diag = yd if y_diag is None else y_diag + yd
            return y_diag

        def finish(g, y_off, y_diag):
            yg = ux_ref[cidx, :, gsl[g]] * dsk_ref[:, gsl[g]] + y_diag + y_off
            yz = yg * _silu(z_ref[:, gsl[g]])
            y_ref[:, gsl[g]] = yz
            return jnp.sum(yz * yz, axis=-1, keepdims=True)

        pending = {0: early(0)}
        diag = {}
        ss = jnp.zeros((L, 1), F32)
        for g in groups:
            if g + 1 < SSD_GROUPS:
                pending[g + 1] = early(g + 1)
            diag[g] = main(g, pending[g][0])
            if g >= 1:
                ss = ss + finish(g - 1, pending[g - 1][1], diag[g - 1])
        ss = ss + finish(SSD_GROUPS - 1, pending[SSD_GROUPS - 1][1], diag[SSD_GROUPS - 1])
        rs = lax.rsqrt(ss * (1.0 / D_SSD) + EPS)
        for g in groups:
            o_ref[:, gsl[g]] = (y_ref[:, gsl[g]] * rs * ng_ref[:, gsl[g]]).astype(o_ref.dtype)


def _expansion_matrix():
    r = jnp.arange(2 * LANE)[:, None]
    col = jnp.arange(4 * D_SSD)[None, :]
    src = (col // D_SSD) * SSD_HEADS + (col % D_SSD) // SSD_HEAD_DIM
    half = LANE // 2
    return ((r % half == src) & (r < 3 * half)).astype(BF16)


def _ssd(gzx, dtraw, conv_w, conv_b, dt_bias, a_log, d_skip_exp, ssd_norm_g, b, s, nc):
    t_rows = gzx.shape[0]
    assert s % CHUNK == 0 and nc % CHUNK == 0
    n_lat, n_ctx = s // CHUNK, nc // CHUNK
    n_all = n_lat + n_ctx
    halo_per_chunk = CHUNK // SUBLANE
    n_halo = t_rows // SUBLANE
    c0 = lambda t: jnp.where(t < n_all, _ssd_chunk_index(t, n_lat, n_ctx), 0)
    c1 = lambda t: jnp.where(t >= n_all + n_ctx, t - n_all - n_ctx, 0)
    blk = lambda bi, c: jnp.where(c < n_lat, bi * n_lat + c, b * n_lat + bi * n_ctx + (c - n_lat))
    xbc_col = (D_ATT + D_SSD) // D_XBC
    z_col = D_ATT // D_SSD
    const = lambda shape: pl.BlockSpec(shape, lambda bi, t: tuple(0 for _ in shape))
    return pl.pallas_call(
        functools.partial(_ssd_kernel, n_lat=n_lat, n_ctx=n_ctx),
        grid=(b, 2 * n_all),
        in_specs=[pl.BlockSpec((CHUNK, D_XBC), lambda bi, t: (blk(bi, c0(t)), xbc_col)),
                  pl.BlockSpec((SUBLANE, D_XBC),
                               lambda bi, t: (jnp.maximum(blk(bi, c0(t)) * halo_per_chunk - 1, 0), xbc_col)),
                  pl.BlockSpec((SUBLANE, D_XBC),
                               lambda bi, t: (jnp.minimum((blk(bi, c0(t)) + 1) * halo_per_chunk, n_halo - 1),
                                              xbc_col)),
                  pl.BlockSpec((CHUNK, LANE), lambda bi, t: (blk(bi, c0(t)), 0)),
                  pl.BlockSpec((CHUNK, D_SSD), lambda bi, t: (bi * n_lat + c1(t), z_col)),
                  const((CONV_W, D_XBC)), const((1, D_XBC)), const((1, LANE)), const((1, LANE)),
                  const((1, D_SSD)), const((1, D_SSD)), const((2 * LANE, 4 * D_SSD))],
        out_specs=pl.BlockSpec((CHUNK, D_SSD), lambda bi, t: (bi * n_lat + c1(t), 0)),
        out_shape=jax.ShapeDtypeStruct((b * s, D_SSD), BF16),
        scratch_shapes=[pltpu.VMEM((D_STATE, D_SSD), F32),
                        pltpu.VMEM((D_STATE, D_SSD), F32),
                        pltpu.VMEM((n_lat, D_STATE, D_SSD), BF16),
                        pltpu.VMEM((n_all, CHUNK, D_SSD), F32),
                        pltpu.VMEM((n_all, CHUNK, 2 * D_BC), BF16),
                        pltpu.VMEM((n_all, CHUNK, 2 * LANE), BF16),
                        pltpu.VMEM((n_all, CHUNK, LANE), F32),
                        pltpu.VMEM((CHUNK, D_SSD), F32)],
        compiler_params=_params(("arbitrary", "arbitrary"), 60),
        name="bi_ssd",
    )(gzx, gzx, gzx, dtraw, gzx, conv_w, conv_b, dt_bias, a_log, d_skip_exp, ssd_norm_g,
      _expansion_matrix())


def _out_kernel(ha_ref, hs_ref, wa_ref, ws_ref, x_ref, gate_ref, ng_ref, o_ref):
    upd = (jnp.dot(ha_ref[...], wa_ref[...], preferred_element_type=F32)
           + jnp.dot(hs_ref[...], ws_ref[...], preferred_element_type=F32))
    y = x_ref[...] + gate_ref[0] * upd
    ms = jnp.mean(y * y, axis=-1, keepdims=True)
    o_ref[...] = (y * lax.rsqrt(ms + EPS) * ng_ref[...]).astype(o_ref.dtype)


def _out_proj(h_att, h_ssd, w_out_att, w_out_ssd, x2d, gate, final_norm_g, s):
    m, d = x2d.shape
    tm = _row_tile(s, (512, 256, 128))
    tiles_per_batch = s // tm
    return pl.pallas_call(
        _out_kernel,
        grid=(m // tm,),
        in_specs=[pl.BlockSpec((tm, D_ATT), lambda i: (i, 0)),
                  pl.BlockSpec((tm, D_SSD), lambda i: (i, 0)),
                  pl.BlockSpec((D_ATT, d), lambda i: (0, 0)),
                  pl.BlockSpec((D_SSD, d), lambda i: (0, 0)),
                  pl.BlockSpec((tm, d), lambda i: (i, 0)),
                  pl.BlockSpec((1, 1, d), lambda i: (i // tiles_per_batch, 0, 0)),
                  pl.BlockSpec((1, d), lambda i: (0, 0))],
        out_specs=pl.BlockSpec((tm, d), lambda i: (i, 0)),
        out_shape=jax.ShapeDtypeStruct((m, d), x2d.dtype),
        compiler_params=_params(("parallel",), 48),
        name="out_proj",
    )(h_att, h_ssd, w_out_att, w_out_ssd, x2d, gate, final_norm_g)


def _rope_tables(s, pad_rows):
    pos = jnp.arange(s)
    row = (pos // GRID_W).astype(F32)
    colp = (pos % GRID_W).astype(F32)
    quarter = HEAD_DIM // 4
    freq = 1.0 / (ROPE_THETA ** (jnp.arange(quarter, dtype=F32) / quarter))
    lane = jnp.arange(LANE)
    dim = lane % HEAD_DIM
    use_col = (dim // (HEAD_DIM // 2)) == 1
    r = dim % (HEAD_DIM // 2)
    first = r < quarter
    ang = jnp.where(use_col[None, :], colp[:, None], row[:, None]) * freq[r % quarter][None, :]
    cos, sin = jnp.cos(ang), jnp.sin(ang)
    sa = jnp.where(first[None, :], -sin, 0.0)
    sb = jnp.where(first[None, :], 0.0, sin)
    ident = lambda v, fill: jnp.concatenate([v, jnp.full((pad_rows, LANE), fill, F32)], axis=0)
    return ident(cos, 1.0), ident(sa, 0.0), ident(sb, 0.0)


def kernel(x, c, ctx, c_ctx, w_mod, b_mod, norm_g, w_in, conv_w, conv_b, a_log_f, a_log_b, dt_bias_f,
           dt_bias_b, d_skip, att_norm_g, ssd_norm_g, sink, w_out, final_norm_g):
    assert w_mod.shape[0] == 1, "single-layer operation"
    b, s, d = x.shape
    nc = ctx.shape[1]

    pad_rows = (-(b + 1)) % SUBLANE
    cc = jnp.concatenate([c, c_ctx[None, :], jnp.zeros((pad_rows, d), F32)], axis=0)
    mod = _modulation(cc, w_mod[0], b_mod)
    shift = mod[:b + 1, :d].reshape(b + 1, 1, d)
    scale = mod[:b + 1, d:2 * d].reshape(b + 1, 1, d)
    gate = mod[:b, 2 * d:].reshape(b, 1, d)

    w_bf = w_in[0].astype(BF16)
    w_dt = jnp.pad(w_bf[:, D_QKV + D_GZX:], ((0, 0), (0, LANE - 2 * SSD_HEADS)))
    w_out_bf = w_out[0].astype(BF16)

    cos_t, sa_t, sb_t = _rope_tables(s, NORM_TILE)
    qkv, xn, dtraw = _norm_qkv(x.reshape(b * s, d), ctx.reshape(b * nc, d), shift, scale, norm_g, w_dt, w_bf,
                               cos_t, sa_t, sb_t, s)

    gzx = _matmul(xn, w_bf, D_QKV, D_GZX, F32)

    h_att = _attention(sink[0], qkv, gzx, att_norm_g, b, s, nc)

    lane_pad = LANE - 2 * SSD_HEADS
    dt_bias = jnp.pad(jnp.concatenate([dt_bias_f[0], dt_bias_b[0]]), (0, lane_pad))[None, :]
    a_log = jnp.pad(jnp.concatenate([a_log_f[0], a_log_b[0]]), (0, lane_pad))[None, :]
    d_skip_exp = jnp.repeat(d_skip[0], SSD_HEAD_DIM)[None, :]
    h_ssd = _ssd(gzx, dtraw, conv_w[0], conv_b, dt_bias, a_log, d_skip_exp, ssd_norm_g, b, s, nc)

    out = _out_proj(h_att, h_ssd, w_out_bf[:D_ATT], w_out_bf[D_ATT:], x.reshape(b * s, d), gate,
                    final_norm_g[None, :], s)
    return out.reshape(b, s, d)
```

```python
import functools

import jax
import jax.numpy as jnp
import numpy as np
from jax import lax
from jax.experimental import pallas as pl
from jax.experimental.pallas import tpu as pltpu

F32 = jnp.float32
BF16 = jnp.bfloat16

EPS = 1e-6
GRID_W = 64
ROPE_THETA = 10000.0
ATT_HEADS = 16
ATT_KV_HEADS = 4
HEAD_DIM = 64
D_ATT = ATT_HEADS * HEAD_DIM
D_KV = ATT_KV_HEADS * HEAD_DIM
Q_PER_KV = ATT_HEADS // ATT_KV_HEADS
BLOCK = 128
SSD_HEADS = 16
SSD_HEAD_DIM = 64
D_SSD = SSD_HEADS * SSD_HEAD_DIM
SSD_GROUPS = 4
SSD_HPG = SSD_HEADS // SSD_GROUPS
D_STATE = 128
D_BC = SSD_GROUPS * D_STATE
D_XBC = D_SSD + 2 * D_BC
CONV_W = 3
CHUNK = 128
D_QKV = D_ATT + 2 * D_KV
D_GZX = D_ATT + D_SSD + D_XBC

LANE = 128
SUBLANE = 8
V7X_VMEM_BYTES = 64 * 1024 * 1024
MIB = 1024 * 1024

NEG = -1e30
LOG2E = 1.4426950408889634


def _silu(v):
    return v / (1.0 + jnp.exp(-v))


def _dot_nt(a, w):
    return lax.dot_general(a, w, (((1,), (1,)), ((), ())), preferred_element_type=F32)


def _params(semantics, vmem_mib):
    return pltpu.CompilerParams(dimension_semantics=semantics, vmem_limit_bytes=vmem_mib * MIB)


def _row_tile(m, candidates=(1024, 512, 256, 128)):
    for t in candidates:
        if m % t == 0:
            return t
    raise ValueError(f"row count {m} not tileable")


def _mod_kernel(c_ref, w_ref, b_ref, o_ref):
    a = _silu(c_ref[...]).astype(BF16)
    o_ref[...] = jnp.dot(a, w_ref[...].astype(BF16), preferred_element_type=F32) + b_ref[...]


def _modulation(cc, w_mod, b_mod):
    rows, d = cc.shape
    n = w_mod.shape[1]
    tn = 768
    assert n % tn == 0
    return pl.pallas_call(
        _mod_kernel,
        grid=(n // tn,),
        in_specs=[pl.BlockSpec((rows, d), lambda j: (0, 0)),
                  pl.BlockSpec((d, tn), lambda j: (0, j)),
                  pl.BlockSpec((1, tn), lambda j: (0, j))],
        out_specs=pl.BlockSpec((rows, tn), lambda j: (0, j)),
        out_shape=jax.ShapeDtypeStruct((rows, n), F32),
        compiler_params=_params(("parallel",), 32),
        name="modulation",
    )(cc, w_mod, b_mod)


def _norm_qkv_kernel(x_ref, ctx_ref, shift_ref, scale_ref, g_ref, wdt_ref, w_ref, cos_ref, sa_ref, sb_ref,
                     qkv_ref, xn_ref, dt_ref, ya_ref, yb_ref, *, n_lat_tiles, n_tiles, n_rope, n_q):
    i = pl.program_id(0)
    is_lat = jnp.minimum(i, n_tiles - 1) < n_lat_tiles

    @pl.when(i == 0)
    def _():
        yb_ref[...] = jnp.zeros_like(yb_ref)

    def step(y_prev_ref, y_next_ref):
        y = y_prev_ref[...]
        dt_ref[...] = _dot_nt(y, wdt_ref[...])
        acc = _dot_nt(y, w_ref[...])
        cos, sa, sb = cos_ref[...], sa_ref[...], sb_ref[...]
        quarter = HEAD_DIM // 4
        for jb in range(acc.shape[1] // LANE):
            blk = acc[:, jb * LANE:(jb + 1) * LANE]
            if jb < n_rope:
                blk = (blk * cos + pltpu.roll(blk, LANE - quarter, 1) * sa
                       + pltpu.roll(blk, quarter, 1) * sb)
            if jb < n_q:
                blk = blk * (HEAD_DIM ** -0.5 * LOG2E)
            qkv_ref[:, jb * LANE:(jb + 1) * LANE] = blk.astype(qkv_ref.dtype)

        v = jnp.where(is_lat, x_ref[...], ctx_ref[...])
        ms = jnp.mean(v * v, axis=-1, keepdims=True)
        gain = g_ref[...] * (1.0 + scale_ref[0])
        yn = (v * lax.rsqrt(ms + EPS) * gain + shift_ref[0]).astype(BF16)
        xn_ref[...] = yn
        y_next_ref[...] = yn

    @pl.when(i % 2 == 0)
    def _():
        step(yb_ref, ya_ref)

    @pl.when(i % 2 == 1)
    def _():
        step(ya_ref, yb_ref)


NORM_TILE = 512


def _norm_qkv(x2d, ctx2d, shift, scale, norm_g, w_dt, w_t, cos_t, sa_t, sb_t, s):
    m_lat, d = x2d.shape
    m_ctx = ctx2d.shape[0]
    b = m_lat // s
    tr = NORM_TILE
    assert s % tr == 0 and m_ctx % tr == 0
    tiles_per_seq = s // tr
    n_lat_tiles, n_ctx_tiles = m_lat // tr, m_ctx // tr
    n_tiles = n_lat_tiles + n_ctx_tiles
    cur = lambda i: jnp.minimum(i, n_tiles - 1)
    prev = lambda i: jnp.maximum(i - 1, 0)
    mod_idx = lambda i: (jnp.where(cur(i) < n_lat_tiles, cur(i) // tiles_per_seq, b), 0, 0)
    tab = pl.BlockSpec((tr, LANE), lambda i: (jnp.where(prev(i) < n_lat_tiles, prev(i) % tiles_per_seq,
                                                        tiles_per_seq), 0))
    out_prev = lambda width: pl.BlockSpec((tr, width), lambda i: (prev(i), 0))
    const = lambda shape: pl.BlockSpec(shape, lambda i: tuple(0 for _ in shape))
    return pl.pallas_call(
        functools.partial(_norm_qkv_kernel, n_lat_tiles=n_lat_tiles, n_tiles=n_tiles,
                          n_rope=(D_ATT + D_KV) // LANE, n_q=D_ATT // LANE),
        grid=(n_tiles + 1,),
        in_specs=[pl.BlockSpec((tr, d), lambda i: (jnp.minimum(cur(i), n_lat_tiles - 1), 0)),
                  pl.BlockSpec((tr, d), lambda i: (jnp.maximum(cur(i) - n_lat_tiles, 0), 0)),
                  pl.BlockSpec((1, 1, d), mod_idx),
                  pl.BlockSpec((1, 1, d), mod_idx),
                  const((1, d)), const((LANE, d)), const((D_QKV, d)),
                  tab, tab, tab],
        out_specs=[out_prev(D_QKV),
                   pl.BlockSpec((tr, d), lambda i: (cur(i), 0)),
                   out_prev(LANE)],
        out_shape=[jax.ShapeDtypeStruct((m_lat + m_ctx, D_QKV), BF16),
                   jax.ShapeDtypeStruct((m_lat + m_ctx, d), BF16),
                   jax.ShapeDtypeStruct((m_lat + m_ctx, LANE), F32)],
        scratch_shapes=[pltpu.VMEM((tr, d), BF16), pltpu.VMEM((tr, d), BF16)],
        compiler_params=_params(("arbitrary",), 56),
        name="norm_qkv",
    )(x2d, ctx2d, shift, scale, norm_g, w_dt, w_t, cos_t, sa_t, sb_t)


def _mm_kernel(x_ref, w_ref, o_ref):
    o_ref[...] = _dot_nt(x_ref[...], w_ref[...]).astype(o_ref.dtype)


def _matmul(x2d, w_t, col0, n, out_dtype, tn=1024):
    m, d = x2d.shape
    tm = _row_tile(m)
    assert n % tn == 0 and col0 % LANE == 0
    return pl.pallas_call(
        _mm_kernel,
        grid=(n // tn, m // tm),
        in_specs=[pl.BlockSpec((tm, d), lambda j, i: (i, 0)),
                  pl.BlockSpec((pl.Element(tn), pl.Element(d)),
                               lambda j, i: (pl.multiple_of(col0 + j * tn, LANE), 0))],
        out_specs=pl.BlockSpec((tm, tn), lambda j, i: (i, j)),
        out_shape=jax.ShapeDtypeStruct((m, n), out_dtype),
        compiler_params=_params(("parallel", "parallel"), 48),
        name="gzx_proj",
    )(x2d, w_t)


ATT_QBLOCKS = 4


def _attn_kernel(sink_ref, q_ref, kp_ref, ko_ref, kn_ref, kc_ref, vp_ref, vo_ref, vn_ref, vc_ref,
                 g_ref, ng_ref, o_ref, *, n_steps):
    n = pl.program_id(1)
    nctx = kc_ref.shape[0]
    nk = 3 * BLOCK + nctx
    cols_g = Q_PER_KV * BLOCK
    k_blocks = [kp_ref[...]] + [ko_ref[u * BLOCK:(u + 1) * BLOCK] for u in range(ATT_QBLOCKS)] + [kn_ref[...]]
    v_blocks = [vp_ref[...]] + [vo_ref[u * BLOCK:(u + 1) * BLOCK] for u in range(ATT_QBLOCKS)] + [vn_ref[...]]

    kj = lax.broadcasted_iota(jnp.int32, (BLOCK, BLOCK), 0)
    qi = lax.broadcasted_iota(jnp.int32, (BLOCK, BLOCK), 1)
    tile = lambda v: jnp.concatenate([v] * Q_PER_KV, axis=1)
    tri_prev = tile(jnp.where(kj >= qi, 0.0, NEG).astype(F32))
    tri_next = tile(jnp.where(kj <= qi, 0.0, NEG).astype(F32))
    edge_prev = tile(jnp.where((kj >= qi) & (n > 0), 0.0, NEG).astype(F32))
    edge_next = tile(jnp.where((kj <= qi) & (n < n_steps - 1), 0.0, NEG).astype(F32))
    head_of_col = lax.broadcasted_iota(jnp.int32, (1, cols_g), 1) // BLOCK
    vlane = lax.broadcasted_iota(jnp.int32, (nk, LANE - HEAD_DIM), 1)
    v_tail = jnp.where(vlane == 0, 1.0, 0.0).astype(BF16)

    def scores_t(u, kh):
        hs = slice(kh * HEAD_DIM, (kh + 1) * HEAD_DIM)
        k_h = jnp.concatenate([blk[:, hs] for blk in k_blocks[u:u + 3]] + [kc_ref[:, hs]], axis=0)
        qg = jnp.concatenate(
            [q_ref[u * BLOCK:(u + 1) * BLOCK, (kh * Q_PER_KV + i) * HEAD_DIM:(kh * Q_PER_KV + i + 1) * HEAD_DIM]
             for i in range(Q_PER_KV)], axis=0)
        return lax.dot_general(k_h, qg, (((1,), (1,)), ((), ())), preferred_element_type=F32)

    def attend(u, kh, st):
        bias_prev = edge_prev if u == 0 else tri_prev
        bias_next = edge_next if u == ATT_QBLOCKS - 1 else tri_next
        st = jnp.concatenate([st[:BLOCK] + bias_prev, st[BLOCK:2 * BLOCK],
                              st[2 * BLOCK:3 * BLOCK] + bias_next, st[3 * BLOCK:]], axis=0)
        sink_row = jnp.zeros((1, cols_g), F32)
        for i in range(Q_PER_KV):
            sink_row = jnp.where(head_of_col == i, sink_ref[kh * Q_PER_KV + i] * LOG2E, sink_row)
        m = jnp.maximum(jnp.max(st, axis=0, keepdims=True), sink_row)
        pt = jnp.exp2(st - m).astype(BF16)
        hs = slice(kh * HEAD_DIM, (kh + 1) * HEAD_DIM)
        v_h = jnp.concatenate([blk[:, hs] for blk in v_blocks[u:u + 3]] + [vc_ref[:, hs]], axis=0)
        v_ext = jnp.concatenate([v_h, v_tail], axis=1)
        ot = lax.dot_general(v_ext, pt, (((0,), (0,)), ((), ())), preferred_element_type=F32)
        inv = 1.0 / (ot[HEAD_DIM:HEAD_DIM + 1] + jnp.exp2(sink_row - m))
        on = ot[:HEAD_DIM] * inv
        pairs = [jnp.concatenate([on[:, (2 * j) * BLOCK:(2 * j + 1) * BLOCK],
                                  on[:, (2 * j + 1) * BLOCK:(2 * j + 2) * BLOCK]], axis=0).T
                 for j in range(Q_PER_KV // 2)]
        return jnp.concatenate(pairs, axis=1)

    def finish(u, groups):
        rows = slice(u * BLOCK, (u + 1) * BLOCK)
        a = jnp.concatenate(groups, axis=1) * _silu(g_ref[rows])
        ms = jnp.mean(a * a, axis=-1, keepdims=True)
        o_ref[rows] = (a * lax.rsqrt(ms + EPS) * ng_ref[...]).astype(o_ref.dtype)

    order = [(u, kh) for u in range(ATT_QBLOCKS) for kh in range(ATT_KV_HEADS)]
    ahead = 1
    pending = [scores_t(*order[i]) for i in range(ahead)]
    groups = []
    for idx, (u, kh) in enumerate(order):
        s_cur = pending.pop(0)
        if idx + ahead < len(order):
            pending.append(scores_t(*order[idx + ahead]))
        groups.append(attend(u, kh, s_cur))
        if kh == ATT_KV_HEADS - 1:
            finish(u, groups)
            groups = []


def _attention(sink, qkv, gzx, att_norm_g, b, s, nc):
    rows = ATT_QBLOCKS * BLOCK
    assert s % rows == 0 and (b * s) % nc == 0
    nb = s // BLOCK
    n_steps = s // rows
    kcol, vcol = D_ATT // D_KV, D_ATT // D_KV + 1
    before = lambda col: pl.BlockSpec(
        (BLOCK, D_KV), lambda bi, n: (bi * nb + jnp.maximum(ATT_QBLOCKS * n - 1, 0), col))
    own = lambda col: pl.BlockSpec((rows, D_KV), lambda bi, n: (bi * n_steps + n, col))
    after = lambda col: pl.BlockSpec(
        (BLOCK, D_KV), lambda bi, n: (bi * nb + jnp.minimum(ATT_QBLOCKS * (n + 1), nb - 1), col))
    ctx_spec = lambda col: pl.BlockSpec((nc, D_KV), lambda bi, n: (b * s // nc + bi, col))
    q_rows = pl.BlockSpec((rows, D_ATT), lambda bi, n: (bi * n_steps + n, 0))
    return pl.pallas_call(
        functools.partial(_attn_kernel, n_steps=n_steps),
        grid=(b, n_steps),
        in_specs=[pl.BlockSpec(memory_space=pltpu.SMEM),
                  q_rows,
                  before(kcol), own(kcol), after(kcol), ctx_spec(kcol),
                  before(vcol), own(vcol), after(vcol), ctx_spec(vcol),
                  q_rows,
                  pl.BlockSpec((1, D_ATT), lambda bi, n: (0, 0))],
        out_specs=q_rows,
        out_shape=jax.ShapeDtypeStruct((b * s, D_ATT), BF16),
        compiler_params=_params(("parallel", "parallel"), 40),
        name="window_attn",
    )(sink, qkv, qkv, qkv, qkv, qkv, qkv, qkv, qkv, qkv, gzx, att_norm_g)


def _ssd_chunk_index(t, n_lat, n_ctx):
    n_all = n_lat + n_ctx
    k = t - n_all
    fwd = jnp.where(k < n_ctx, n_lat + k, k - n_ctx)
    return jnp.where(t >= n_all, fwd, n_all - 1 - t)


def _ssd_kernel(xbc_ref, hp_ref, hn_ref, dt_ref, z_ref, cw_ref, cb_ref, dtb_ref, alog_ref, dsk_ref,
                ng_ref, ex_ref, o_ref, sf_ref, sb_ref, sin_ref, ux_ref, ubc_ref, lhs_ref, pack_ref, y_ref, pad_ref,
                *, n_lat, n_ctx):
    t = pl.program_id(1)
    n_all = n_lat + n_ctx
    cidx = _ssd_chunk_index(t, n_lat, n_ctx)
    lat_idx = jnp.minimum(cidx, n_lat - 1)
    sweep1 = t >= n_all
    is_lat = cidx < n_lat
    L = CHUNK
    gw = SSD_HPG * SSD_HEAD_DIM
    lane = lax.broadcasted_iota(jnp.int32, (1, LANE), 1)
    rows = lax.broadcasted_iota(jnp.int32, (L, 1), 0)

    @pl.when(t == 0)
    def _():
        sf_ref[...] = jnp.zeros_like(sf_ref)
        sb_ref[...] = jnp.zeros_like(sb_ref)
        ux_ref[n_all - 1] = jnp.zeros((CHUNK, D_SSD), F32)
        ubc_ref[n_all - 1] = jnp.zeros((CHUNK, 2 * D_BC), BF16)
        lhs_ref[n_all - 1] = jnp.zeros((CHUNK, 2 * LANE), BF16)

    def expand(lhs, block, g):
        c0 = block * D_SSD + g * gw
        return jnp.dot(lhs, ex_ref[:, c0:c0 + gw], preferred_element_type=F32)

    def update_state(state_ref, chunk, lhs_ex, g, wblock, eblock, end_row):
        sl = slice(g * gw, (g + 1) * gw)
        r8 = (end_row // SUBLANE) * SUBLANE
        xw = (ux_ref[chunk, :, sl] * expand(lhs_ex, wblock, g)).astype(BF16)
        bg = ubc_ref[chunk, :, g * D_STATE:(g + 1) * D_STATE]
        loc = lax.dot_general(bg, xw, (((0,), (0,)), ((), ())), preferred_element_type=F32)
        tot = expand(lhs_ex[r8:r8 + SUBLANE], eblock, g)[end_row - r8:end_row - r8 + 1]
        new = state_ref[:, sl] * tot + loc
        state_ref[:, sl] = new
        return new

    @pl.when(jnp.logical_not(sweep1))
    def _():
        done = jnp.minimum(cidx + 1, n_all - 1)
        lhs_done = lhs_ref[done]
        groups = range(SSD_GROUPS)
        gsl = [slice(g * gw, (g + 1) * gw) for g in groups]
        is_first = (cidx == 0) | (cidx == n_lat)
        is_last = (cidx == n_lat - 1) | (cidx == n_all - 1)

        def conv_strip(j):
            for k in range(j * gw // LANE, (j + 1) * gw // LANE):
                lanes = slice(k * LANE, (k + 1) * LANE)
                xc = xbc_ref[:, lanes]
                pad_ref[k, SUBLANE:SUBLANE + L, :] = xc
                pad_ref[k, SUBLANE - 1:SUBLANE, :] = jnp.where(is_first, 0.0, hp_ref[SUBLANE - 1:SUBLANE, lanes])
                pad_ref[k, SUBLANE + L:SUBLANE + L + 1, :] = jnp.where(is_last, 0.0, hn_ref[0:1, lanes])
                x_prev = pad_ref[k, SUBLANE - 1:SUBLANE - 1 + L, :]
                x_next = pad_ref[k, SUBLANE + 1:SUBLANE + 1 + L, :]
                u = _silu(x_prev * cw_ref[0:1, lanes] + xc * cw_ref[1:2, lanes] + x_next * cw_ref[2:3, lanes]
                          + cb_ref[:, lanes])
                if k * LANE < D_SSD:
                    ux_ref[cidx, :, lanes] = u
                else:
                    ubc_ref[cidx, :, k * LANE - D_SSD:(k + 1) * LANE - D_SSD] = u.astype(BF16)

        wexp = [expand(lhs_done, 1, g) for g in groups]
        tots = [expand(lhs_done[0:SUBLANE], 3, g)[0:1] for g in groups]
        conv_strip(0)
        conv_strip(1)
        xws = [(ux_ref[done, :, gsl[g]] * wexp[g]).astype(BF16) for g in groups]
        locs = [lax.dot_general(ubc_ref[done, :, g * D_STATE:(g + 1) * D_STATE], xws[g],
                                (((0,), (0,)), ((), ())), preferred_element_type=F32) for g in groups]
        conv_strip(2)
        conv_strip(3)
        conv_strip(4)
        for g in groups:
            new = sb_ref[:, gsl[g]] * tots[g] + locs[g]
            sb_ref[:, gsl[g]] = new
            sin_ref[lat_idx, :, gsl[g]] = new.astype(BF16)
        conv_strip(5)
        conv_strip(6)
        conv_strip(7)

        dtr = dt_ref[...] + dtb_ref[...]
        dt = jnp.maximum(dtr, 0.0) + jnp.log(1.0 + jnp.exp(-jnp.abs(dtr)))
        a = dt * (-jnp.exp(alog_ref[...]))
        pre, suf = a, a
        d = 1
        while d < L:
            pre = pre + jnp.where(rows >= d, pltpu.roll(pre, d, 0), 0.0)
            suf = suf + jnp.where(rows < L - d, pltpu.roll(suf, L - d, 0), 0.0)
            d *= 2
        is_fwd_lane = lane < SSD_HEADS
        cs = jnp.where(is_fwd_lane, pre, suf)
        tot = jnp.where(is_fwd_lane, cs[L - 1:L, :], cs[0:1, :])
        e = jnp.exp(cs)
        wdt = jnp.exp(tot - cs) * dt
        pack_ref[cidx] = jnp.where(lane < 2 * SSD_HEADS, cs * LOG2E, pltpu.roll(dt, 2 * SSD_HEADS, 1))

        fac = jnp.where(lane < 2 * SSD_HEADS, wdt,
                        jnp.where(lane < 4 * SSD_HEADS, pltpu.roll(e, 2 * SSD_HEADS, 1), 0.0))
        hi = fac.astype(BF16).astype(F32)
        r1 = fac - hi
        mid = r1.astype(BF16).astype(F32)
        lo = (r1 - mid).astype(BF16)
        lhs_ref[cidx] = jnp.concatenate([(hi + pltpu.roll(mid, LANE // 2, 1)).astype(BF16), lo], axis=1)

    @pl.when(sweep1 & jnp.logical_not(is_lat))
    def _():
        lhs_ex = lhs_ref[cidx]
        for g in range(SSD_GROUPS):
            update_state(sf_ref, cidx, lhs_ex, g, 0, 2, L - 1)

    @pl.when(sweep1 & is_lat)
    def _():
        lhs_ex = lhs_ref[cidx]
        pack = pack_ref[cidx]
        pack_t = pack.T
        li = lax.broadcasted_iota(jnp.int32, (L, L), 0)
        si = lax.broadcasted_iota(jnp.int32, (L, L), 1)
        masks = [si <= li, si >= li]
        head_of_lane = lax.broadcasted_iota(jnp.int32, (1, gw), 1) // SSD_HEAD_DIM
        groups = range(SSD_GROUPS)
        gsl = [slice(g * gw, (g + 1) * gw) for g in groups]

        def early(g):
            bg = ubc_ref[cidx, :, g * D_STATE:(g + 1) * D_STATE]
            cg = ubc_ref[cidx, :, D_BC + g * D_STATE:D_BC + (g + 1) * D_STATE]
            cb = lax.dot_general(cg, bg, (((1,), (1,)), ((), ())), preferred_element_type=F32)
            cbm = [jnp.where(m, cb, 0.0) for m in masks]
            y_off = (jnp.dot(cg, sf_ref[:, gsl[g]].astype(BF16), preferred_element_type=F32)
                     * expand(lhs_ex, 2, g)
                     + jnp.dot(cg, sin_ref[lat_idx, :, gsl[g]], preferred_element_type=F32)
                     * expand(lhs_ex, 3, g))
            update_state(sf_ref, cidx, lhs_ex, g, 0, 2, L - 1)
            return cbm, y_off

        def main(g, cbm):
            xg = ux_ref[cidx, :, gsl[g]].astype(BF16)
            blockdiag = jnp.concatenate(
                [jnp.where(head_of_lane == hh, xg, jnp.zeros_like(xg)) for hh in range(SSD_HPG)], axis=0)
            y_diag = None
            for dirn in range(2):
                ws = []
                for hh in range(SSD_HPG):
                    ln = dirn * SSD_HEADS + g * SSD_HPG + hh
                    col = pack[:, ln:ln + 1]
                    row = pack_t[ln:ln + 1, :]
                    dtrow = pack_t[2 * SSD_HEADS + ln:2 * SSD_HEADS + ln + 1, :]
                    dec = jnp.exp2(jnp.minimum(col - row, 0.0))
                    ws.append((cbm[dirn] * dec * dtrow).astype(BF16))
                yd = jnp.dot(jnp.concatenate(ws, axis=1), blockdiag, preferred_element_type=F32)
                y_diag = yd if y_diag is None else y_diag + yd
            return y_diag

        def finish(g, y_off, y_diag):
            yg = ux_ref[cidx, :, gsl[g]] * dsk_ref[:, gsl[g]] + y_diag + y_off
            yz = yg * _silu(z_ref[:, gsl[g]])
            y_ref[:, gsl[g]] = yz
            return jnp.sum(yz * yz, axis=-1, keepdims=True)

        pending = {0: early(0)}
        diag = {}
        ss = jnp.zeros((L, 1), F32)
        for g in groups:
            if g + 1 < SSD_GROUPS:
                pending[g + 1] = early(g + 1)
            diag[g] = main(g, pending[g][0])
            if g >= 1:
                ss = ss + finish(g - 1, pending[g - 1][1], diag[g - 1])
        ss = ss + finish(SSD_GROUPS - 1, pending[SSD_GROUPS - 1][1], diag[SSD_GROUPS - 1])
        rs = lax.rsqrt(ss * (1.0 / D_SSD) + EPS)
        for g in groups:
            o_ref[:, gsl[g]] = (y_ref[:, gsl[g]] * rs * ng_ref[:, gsl[g]]).astype(o_ref.dtype)


def _expansion_matrix():
    r = jnp.arange(2 * LANE)[:, None]
    col = jnp.arange(4 * D_SSD)[None, :]
    src = (col // D_SSD) * SSD_HEADS + (col % D_SSD) // SSD_HEAD_DIM
    half = LANE // 2
    return ((r % half == src) & (r < 3 * half)).astype(BF16)


def _ssd(gzx, dtraw, conv_w, conv_b, dt_bias, a_log, d_skip_exp, ssd_norm_g, b, s, nc):
    t_rows = gzx.shape[0]
    assert s % CHUNK == 0 and nc % CHUNK == 0
    n_lat, n_ctx = s // CHUNK, nc // CHUNK
    n_all = n_lat + n_ctx
    halo_per_chunk = CHUNK // SUBLANE
    n_halo = t_rows // SUBLANE
    c0 = lambda t: jnp.where(t < n_all, _ssd_chunk_index(t, n_lat, n_ctx), 0)
    c1 = lambda t: jnp.where(t >= n_all + n_ctx, t - n_all - n_ctx, 0)
    blk = lambda bi, c: jnp.where(c < n_lat, bi * n_lat + c, b * n_lat + bi * n_ctx + (c - n_lat))
    xbc_col = (D_ATT + D_SSD) // D_XBC
    z_col = D_ATT // D_SSD
    const = lambda shape: pl.BlockSpec(shape, lambda bi, t: tuple(0 for _ in shape))
    return pl.pallas_call(
        functools.partial(_ssd_kernel, n_lat=n_lat, n_ctx=n_ctx),
        grid=(b, 2 * n_all),
        in_specs=[pl.BlockSpec((CHUNK, D_XBC), lambda bi, t: (blk(bi, c0(t)), xbc_col)),
                  pl.BlockSpec((SUBLANE, D_XBC),
                               lambda bi, t: (jnp.maximum(blk(bi, c0(t)) * halo_per_chunk - 1, 0), xbc_col)),
                  pl.BlockSpec((SUBLANE, D_XBC),
                               lambda bi, t: (jnp.minimum((blk(bi, c0(t)) + 1) * halo_per_chunk, n_halo - 1),
                                              xbc_col)),
                  pl.BlockSpec((CHUNK, LANE), lambda bi, t: (blk(bi, c0(t)), 0)),
                  pl.BlockSpec((CHUNK, D_SSD), lambda bi, t: (bi * n_lat + c1(t), z_col)),
                  const((CONV_W, D_XBC)), const((1, D_XBC)), const((1, LANE)), const((1, LANE)),
                  const((1, D_SSD)), const((1, D_SSD)), const((2 * LANE, 4 * D_SSD))],
        out_specs=pl.BlockSpec((CHUNK, D_SSD), lambda bi, t: (bi * n_lat + c1(t), 0)),
        out_shape=jax.ShapeDtypeStruct((b * s, D_SSD), BF16),
        scratch_shapes=[pltpu.VMEM((D_STATE, D_SSD), F32),
                        pltpu.VMEM((D_STATE, D_SSD), F32),
                        pltpu.VMEM((n_lat, D_STATE, D_SSD), BF16),
                        pltpu.VMEM((n_all, CHUNK, D_SSD), F32),
                        pltpu.VMEM((n_all, CHUNK, 2 * D_BC), BF16),
                        pltpu.VMEM((n_all, CHUNK, 2 * LANE), BF16),
                        pltpu.VMEM((n_all, CHUNK, LANE), F32),
                        pltpu.VMEM((CHUNK, D_SSD), F32),
                        pltpu.VMEM((D_XBC // LANE, CHUNK + 2 * SUBLANE, LANE), F32)],
        compiler_params=_params(("arbitrary", "arbitrary"), 60),
        name="bi_ssd",
    )(gzx, gzx, gzx, dtraw, gzx, conv_w, conv_b, dt_bias, a_log, d_skip_exp, ssd_norm_g,
      _expansion_matrix())


def _out_kernel(ha_ref, hs_ref, wa_ref, ws_ref, x_ref, gate_ref, ng_ref, o_ref):
    upd = (jnp.dot(ha_ref[...], wa_ref[...], preferred_element_type=F32)
           + jnp.dot(hs_ref[...], ws_ref[...], preferred_element_type=F32))
    y = x_ref[...] + gate_ref[0] * upd
    ms = jnp.mean(y * y, axis=-1, keepdims=True)
    o_ref[...] = (y * lax.rsqrt(ms + EPS) * ng_ref[...]).astype(o_ref.dtype)


def _out_proj(h_att, h_ssd, w_out_bf, x2d, gate, final_norm_g, s):
    m, d = x2d.shape
    tm = _row_tile(s, (512, 256, 128))
    tiles_per_batch = s // tm
    return pl.pallas_call(
        _out_kernel,
        grid=(m // tm,),
        in_specs=[pl.BlockSpec((tm, D_ATT), lambda i: (i, 0)),
                  pl.BlockSpec((tm, D_SSD), lambda i: (i, 0)),
                  pl.BlockSpec((D_ATT, d), lambda i: (0, 0)),
                  pl.BlockSpec((D_SSD, d), lambda i: (D_ATT // D_SSD, 0)),
                  pl.BlockSpec((tm, d), lambda i: (i, 0)),
                  pl.BlockSpec((1, 1, d), lambda i: (i // tiles_per_batch, 0, 0)),
                  pl.BlockSpec((1, d), lambda i: (0, 0))],
        out_specs=pl.BlockSpec((tm, d), lambda i: (i, 0)),
        out_shape=jax.ShapeDtypeStruct((m, d), x2d.dtype),
        compiler_params=_params(("parallel",), 48),
        name="out_proj",
    )(h_att, h_ssd, w_out_bf, w_out_bf, x2d, gate, final_norm_g)


def _rope_tables(s, pad_rows):
    pos = np.arange(s)
    row = (pos // GRID_W).astype(np.float32)
    colp = (pos % GRID_W).astype(np.float32)
    quarter = HEAD_DIM // 4
    freq = (1.0 / (np.float32(ROPE_THETA) ** (np.arange(quarter, dtype=np.float32) / np.float32(quarter)))
            ).astype(np.float32)
    lane = np.arange(LANE)
    dim = lane % HEAD_DIM
    use_col = (dim // (HEAD_DIM // 2)) == 1
    r = dim % (HEAD_DIM // 2)
    first = r < quarter
    ang = (np.where(use_col[None, :], colp[:, None], row[:, None]) * freq[r % quarter][None, :]).astype(np.float32)
    cos, sin = np.cos(ang), np.sin(ang)
    sa = np.where(first[None, :], -sin, 0.0)
    sb = np.where(first[None, :], 0.0, sin)
    ident = lambda v, fill: jnp.asarray(
        np.concatenate([v, np.full((pad_rows, LANE), fill)], axis=0).astype(np.float32))
    return ident(cos, 1.0), ident(sa, 0.0), ident(sb, 0.0)


def kernel(x, c, ctx, c_ctx, w_mod, b_mod, norm_g, w_in, conv_w, conv_b, a_log_f, a_log_b, dt_bias_f,
           dt_bias_b, d_skip, att_norm_g, ssd_norm_g, sink, w_out, final_norm_g):
    assert w_mod.shape[0] == 1, "single-layer operation"
    b, s, d = x.shape
    nc = ctx.shape[1]

    pad_rows = (-(b + 1)) % SUBLANE
    cc = jnp.concatenate([c, c_ctx[None, :], jnp.zeros((pad_rows, d), F32)], axis=0)
    mod = _modulation(cc, w_mod[0], b_mod)
    shift = mod[:b + 1, :d].reshape(b + 1, 1, d)
    scale = mod[:b + 1, d:2 * d].reshape(b + 1, 1, d)
    gate = mod[:b, 2 * d:].reshape(b, 1, d)

    w_t = jnp.swapaxes(w_in[0], 0, 1).astype(BF16)
    w_dt = jnp.pad(w_t[D_QKV + D_GZX:], ((0, LANE - 2 * SSD_HEADS), (0, 0)))
    w_out_bf = w_out[0].astype(BF16)

    cos_t, sa_t, sb_t = _rope_tables(s, NORM_TILE)
    qkv, xn, dtraw = _norm_qkv(x.reshape(b * s, d), ctx.reshape(b * nc, d), shift, scale, norm_g, w_dt, w_t,
                               cos_t, sa_t, sb_t, s)

    gzx = _matmul(xn, w_t, D_QKV, D_GZX, F32)

    h_att = _attention(sink[0], qkv, gzx, att_norm_g, b, s, nc)

    lane_pad = LANE - 2 * SSD_HEADS
    dt_bias = jnp.pad(jnp.concatenate([dt_bias_f[0], dt_bias_b[0]]), (0, lane_pad))[None, :]
    a_log = jnp.pad(jnp.concatenate([a_log_f[0], a_log_b[0]]), (0, lane_pad))[None, :]
    d_skip_exp = jnp.repeat(d_skip[0], SSD_HEAD_DIM)[None, :]
    h_ssd = _ssd(gzx, dtraw, conv_w[0], conv_b, dt_bias, a_log, d_skip_exp, ssd_norm_g, b, s, nc)

    out = _out_proj(h_att, h_ssd, w_out_bf, x.reshape(b * s, d), gate, final_norm_g[None, :], s)
    return out.reshape(b, s, d)
```

```python
import functools

import jax
import jax.numpy as jnp
import numpy as np
from jax import lax
from jax.experimental import pallas as pl
from jax.experimental.pallas import tpu as pltpu

F32 = jnp.float32
BF16 = jnp.bfloat16

EPS = 1e-6
GRID_W = 64
ROPE_THETA = 10000.0
ATT_HEADS = 16
ATT_KV_HEADS = 4
HEAD_DIM = 64
D_ATT = ATT_HEADS * HEAD_DIM
D_KV = ATT_KV_HEADS * HEAD_DIM
Q_PER_KV = ATT_HEADS // ATT_KV_HEADS
BLOCK = 128
SSD_HEADS = 16
SSD_HEAD_DIM = 64
D_SSD = SSD_HEADS * SSD_HEAD_DIM
SSD_GROUPS = 4
SSD_HPG = SSD_HEADS // SSD_GROUPS
D_STATE = 128
D_BC = SSD_GROUPS * D_STATE
D_XBC = D_SSD + 2 * D_BC
CONV_W = 3
CHUNK = 128
D_QKV = D_ATT + 2 * D_KV
D_GZX = D_ATT + D_SSD + D_XBC

LANE = 128
SUBLANE = 8
V7X_VMEM_BYTES = 64 * 1024 * 1024
MIB = 1024 * 1024

NEG = -1e30
LOG2E = 1.4426950408889634


def _silu(v):
    return v / (1.0 + jnp.exp(-v))


def _dot_nt(a, w):
    return lax.dot_general(a, w, (((1,), (1,)), ((), ())), preferred_element_type=F32)


def _params(semantics, vmem_mib):
    return pltpu.CompilerParams(dimension_semantics=semantics, vmem_limit_bytes=vmem_mib * MIB)


def _row_tile(m, candidates=(1024, 512, 256, 128)):
    for t in candidates:
        if m % t == 0:
            return t
    raise ValueError(f"row count {m} not tileable")


def _mod_kernel(c_ref, w_ref, b_ref, o_ref):
    a = _silu(c_ref[...]).astype(BF16)
    o_ref[...] = jnp.dot(a, w_ref[...].astype(BF16), preferred_element_type=F32) + b_ref[...]


def _modulation(cc, w_mod, b_mod):
    rows, d = cc.shape
    n = w_mod.shape[1]
    tn = 768
    assert n % tn == 0
    return pl.pallas_call(
        _mod_kernel,
        grid=(n // tn,),
        in_specs=[pl.BlockSpec((rows, d), lambda j: (0, 0)),
                  pl.BlockSpec((d, tn), lambda j: (0, j)),
                  pl.BlockSpec((1, tn), lambda j: (0, j))],
        out_specs=pl.BlockSpec((rows, tn), lambda j: (0, j)),
        out_shape=jax.ShapeDtypeStruct((rows, n), F32),
        compiler_params=_params(("parallel",), 32),
        name="modulation",
    )(cc, w_mod, b_mod)


def _norm_qkv_kernel(x_ref, ctx_ref, shift_ref, scale_ref, g_ref, wdt_ref, w_ref, cos_ref, sa_ref, sb_ref,
                     qkv_ref, xn_ref, dt_ref, ya_ref, yb_ref, *, n_lat_tiles, n_tiles, n_rope, n_q):
    i = pl.program_id(0)
    is_lat = jnp.minimum(i, n_tiles - 1) < n_lat_tiles

    @pl.when(i == 0)
    def _():
        yb_ref[...] = jnp.zeros_like(yb_ref)

    def step(y_prev_ref, y_next_ref):
        y = y_prev_ref[...]
        dt_ref[...] = _dot_nt(y, wdt_ref[...])
        acc = _dot_nt(y, w_ref[...])
        cos, sa, sb = cos_ref[...], sa_ref[...], sb_ref[...]
        quarter = HEAD_DIM // 4
        for jb in range(acc.shape[1] // LANE):
            blk = acc[:, jb * LANE:(jb + 1) * LANE]
            if jb < n_rope:
                blk = (blk * cos + pltpu.roll(blk, LANE - quarter, 1) * sa
                       + pltpu.roll(blk, quarter, 1) * sb)
            if jb < n_q:
                blk = blk * (HEAD_DIM ** -0.5 * LOG2E)
            qkv_ref[:, jb * LANE:(jb + 1) * LANE] = blk.astype(qkv_ref.dtype)

        v = jnp.where(is_lat, x_ref[...], ctx_ref[...])
        ms = jnp.mean(v * v, axis=-1, keepdims=True)
        gain = g_ref[...] * (1.0 + scale_ref[0])
        yn = (v * lax.rsqrt(ms + EPS) * gain + shift_ref[0]).astype(BF16)
        xn_ref[...] = yn
        y_next_ref[...] = yn

    @pl.when(i % 2 == 0)
    def _():
        step(yb_ref, ya_ref)

    @pl.when(i % 2 == 1)
    def _():
        step(ya_ref, yb_ref)


NORM_TILE = 512


def _norm_qkv(x2d, ctx2d, shift, scale, norm_g, w_dt, w_t, cos_t, sa_t, sb_t, s):
    m_lat, d = x2d.shape
    m_ctx = ctx2d.shape[0]
    b = m_lat // s
    tr = NORM_TILE
    assert s % tr == 0 and m_ctx % tr == 0
    tiles_per_seq = s // tr
    n_lat_tiles, n_ctx_tiles = m_lat // tr, m_ctx // tr
    n_tiles = n_lat_tiles + n_ctx_tiles
    cur = lambda i: jnp.minimum(i, n_tiles - 1)
    prev = lambda i: jnp.maximum(i - 1, 0)
    mod_idx = lambda i: (jnp.where(cur(i) < n_lat_tiles, cur(i) // tiles_per_seq, b), 0, 0)
    tab = pl.BlockSpec((tr, LANE), lambda i: (jnp.where(prev(i) < n_lat_tiles, prev(i) % tiles_per_seq,
                                                        tiles_per_seq), 0))
    out_prev = lambda width: pl.BlockSpec((tr, width), lambda i: (prev(i), 0))
    const = lambda shape: pl.BlockSpec(shape, lambda i: tuple(0 for _ in shape))
    return pl.pallas_call(
        functools.partial(_norm_qkv_kernel, n_lat_tiles=n_lat_tiles, n_tiles=n_tiles,
                          n_rope=(D_ATT + D_KV) // LANE, n_q=D_ATT // LANE),
        grid=(n_tiles + 1,),
        in_specs=[pl.BlockSpec((tr, d), lambda i: (jnp.minimum(cur(i), n_lat_tiles - 1), 0)),
                  pl.BlockSpec((tr, d), lambda i: (jnp.maximum(cur(i) - n_lat_tiles, 0), 0)),
                  pl.BlockSpec((1, 1, d), mod_idx),
                  pl.BlockSpec((1, 1, d), mod_idx),
                  const((1, d)), const((LANE, d)), const((D_QKV, d)),
                  tab, tab, tab],
        out_specs=[out_prev(D_QKV),
                   pl.BlockSpec((tr, d), lambda i: (cur(i), 0)),
                   out_prev(LANE)],
        out_shape=[jax.ShapeDtypeStruct((m_lat + m_ctx, D_QKV), BF16),
                   jax.ShapeDtypeStruct((m_lat + m_ctx, d), BF16),
                   jax.ShapeDtypeStruct((m_lat + m_ctx, LANE), F32)],
        scratch_shapes=[pltpu.VMEM((tr, d), BF16), pltpu.VMEM((tr, d), BF16)],
        compiler_params=_params(("arbitrary",), 56),
        name="norm_qkv",
    )(x2d, ctx2d, shift, scale, norm_g, w_dt, w_t, cos_t, sa_t, sb_t)


def _mm_kernel(x_ref, w_ref, o_ref):
    o_ref[...] = _dot_nt(x_ref[...], w_ref[...]).astype(o_ref.dtype)


def _matmul(x2d, w_t, col0, n, out_dtype, tn=1024):
    m, d = x2d.shape
    tm = _row_tile(m)
    assert n % tn == 0 and col0 % LANE == 0
    return pl.pallas_call(
        _mm_kernel,
        grid=(n // tn, m // tm),
        in_specs=[pl.BlockSpec((tm, d), lambda j, i: (i, 0)),
                  pl.BlockSpec((pl.Element(tn), pl.Element(d)),
                               lambda j, i: (pl.multiple_of(col0 + j * tn, LANE), 0))],
        out_specs=pl.BlockSpec((tm, tn), lambda j, i: (i, j)),
        out_shape=jax.ShapeDtypeStruct((m, n), out_dtype),
        compiler_params=_params(("parallel", "parallel"), 48),
        name="gzx_proj",
    )(x2d, w_t)


ATT_QBLOCKS = 4


def _attn_kernel(sink_ref, q_ref, kp_ref, ko_ref, kn_ref, kc_ref, vp_ref, vo_ref, vn_ref, vc_ref,
                 g_ref, ng_ref, o_ref, *, n_steps):
    n = pl.program_id(1)
    nctx = kc_ref.shape[0]
    nk = 3 * BLOCK + nctx
    cols_g = Q_PER_KV * BLOCK
    k_blocks = [kp_ref[...]] + [ko_ref[u * BLOCK:(u + 1) * BLOCK] for u in range(ATT_QBLOCKS)] + [kn_ref[...]]
    v_blocks = [vp_ref[...]] + [vo_ref[u * BLOCK:(u + 1) * BLOCK] for u in range(ATT_QBLOCKS)] + [vn_ref[...]]

    kj = lax.broadcasted_iota(jnp.int32, (BLOCK, BLOCK), 0)
    qi = lax.broadcasted_iota(jnp.int32, (BLOCK, BLOCK), 1)
    tile = lambda v: jnp.concatenate([v] * Q_PER_KV, axis=1)
    tri_prev = tile(jnp.where(kj >= qi, 0.0, NEG).astype(F32))
    tri_next = tile(jnp.where(kj <= qi, 0.0, NEG).astype(F32))
    edge_prev = tile(jnp.where((kj >= qi) & (n > 0), 0.0, NEG).astype(F32))
    edge_next = tile(jnp.where((kj <= qi) & (n < n_steps - 1), 0.0, NEG).astype(F32))
    head_of_col = lax.broadcasted_iota(jnp.int32, (1, cols_g), 1) // BLOCK
    vlane = lax.broadcasted_iota(jnp.int32, (nk, LANE - HEAD_DIM), 1)
    v_tail = jnp.where(vlane == 0, 1.0, 0.0).astype(BF16)

    def scores_t(u, kh):
        hs = slice(kh * HEAD_DIM, (kh + 1) * HEAD_DIM)
        k_h = jnp.concatenate([blk[:, hs] for blk in k_blocks[u:u + 3]] + [kc_ref[:, hs]], axis=0)
        qg = jnp.concatenate(
            [q_ref[u * BLOCK:(u + 1) * BLOCK, (kh * Q_PER_KV + i) * HEAD_DIM:(kh * Q_PER_KV + i + 1) * HEAD_DIM]
             for i in range(Q_PER_KV)], axis=0)
        return lax.dot_general(k_h, qg, (((1,), (1,)), ((), ())), preferred_element_type=F32)

    def attend(u, kh, st):
        bias_prev = edge_prev if u == 0 else tri_prev
        bias_next = edge_next if u == ATT_QBLOCKS - 1 else tri_next
        st = jnp.concatenate([st[:BLOCK] + bias_prev, st[BLOCK:2 * BLOCK],
                              st[2 * BLOCK:3 * BLOCK] + bias_next, st[3 * BLOCK:]], axis=0)
        sink_row = jnp.zeros((1, cols_g), F32)
        for i in range(Q_PER_KV):
            sink_row = jnp.where(head_of_col == i, sink_ref[kh * Q_PER_KV + i] * LOG2E, sink_row)
        m = jnp.maximum(jnp.max(st, axis=0, keepdims=True), sink_row)
        pt = jnp.exp2(st - m).astype(BF16)
        hs = slice(kh * HEAD_DIM, (kh + 1) * HEAD_DIM)
        v_h = jnp.concatenate([blk[:, hs] for blk in v_blocks[u:u + 3]] + [vc_ref[:, hs]], axis=0)
        v_ext = jnp.concatenate([v_h, v_tail], axis=1)
        ot = lax.dot_general(v_ext, pt, (((0,), (0,)), ((), ())), preferred_element_type=F32)
        inv = 1.0 / (ot[HEAD_DIM:HEAD_DIM + 1] + jnp.exp2(sink_row - m))
        on = ot[:HEAD_DIM] * inv
        pairs = [jnp.concatenate([on[:, (2 * j) * BLOCK:(2 * j + 1) * BLOCK],
                                  on[:, (2 * j + 1) * BLOCK:(2 * j + 2) * BLOCK]], axis=0).T
                 for j in range(Q_PER_KV // 2)]
        return jnp.concatenate(pairs, axis=1)

    def finish(u, groups):
        rows = slice(u * BLOCK, (u + 1) * BLOCK)
        a = jnp.concatenate(groups, axis=1) * _silu(g_ref[rows])
        ms = jnp.mean(a * a, axis=-1, keepdims=True)
        o_ref[rows] = (a * lax.rsqrt(ms + EPS) * ng_ref[...]).astype(o_ref.dtype)

    order = [(u, kh) for u in range(ATT_QBLOCKS) for kh in range(ATT_KV_HEADS)]
    ahead = 1
    pending = [scores_t(*order[i]) for i in range(ahead)]
    groups = []
    for idx, (u, kh) in enumerate(order):
        s_cur = pending.pop(0)
        if idx + ahead < len(order):
            pending.append(scores_t(*order[idx + ahead]))
        groups.append(attend(u, kh, s_cur))
        if kh == ATT_KV_HEADS - 1:
            finish(u, groups)
            groups = []


def _attention(sink, qkv, gzx, att_norm_g, b, s, nc):
    rows = ATT_QBLOCKS * BLOCK
    assert s % rows == 0 and (b * s) % nc == 0
    nb = s // BLOCK
    n_steps = s // rows
    kcol, vcol = D_ATT // D_KV, D_ATT // D_KV + 1
    before = lambda col: pl.BlockSpec(
        (BLOCK, D_KV), lambda bi, n: (bi * nb + jnp.maximum(ATT_QBLOCKS * n - 1, 0), col))
    own = lambda col: pl.BlockSpec((rows, D_KV), lambda bi, n: (bi * n_steps + n, col))
    after = lambda col: pl.BlockSpec(
        (BLOCK, D_KV), lambda bi, n: (bi * nb + jnp.minimum(ATT_QBLOCKS * (n + 1), nb - 1), col))
    ctx_spec = lambda col: pl.BlockSpec((nc, D_KV), lambda bi, n: (b * s // nc + bi, col))
    q_rows = pl.BlockSpec((rows, D_ATT), lambda bi, n: (bi * n_steps + n, 0))
    return pl.pallas_call(
        functools.partial(_attn_kernel, n_steps=n_steps),
        grid=(b, n_steps),
        in_specs=[pl.BlockSpec(memory_space=pltpu.SMEM),
                  q_rows,
                  before(kcol), own(kcol), after(kcol), ctx_spec(kcol),
                  before(vcol), own(vcol), after(vcol), ctx_spec(vcol),
                  q_rows,
                  pl.BlockSpec((1, D_ATT), lambda bi, n: (0, 0))],
        out_specs=q_rows,
        out_shape=jax.ShapeDtypeStruct((b * s, D_ATT), BF16),
        compiler_params=_params(("parallel", "parallel"), 40),
        name="window_attn",
    )(sink, qkv, qkv, qkv, qkv, qkv, qkv, qkv, qkv, qkv, gzx, att_norm_g)


SSD_CPS = 2


def _ssd_visit_index(t, n_lat, n_ctx):
    n_all = n_lat + n_ctx
    k = t - n_all
    fwd = jnp.where(k < n_ctx, n_lat + k, k - n_ctx)
    return jnp.where(t >= n_all, fwd, n_all - 1 - t)


def _ssd_kernel(xbc_ref, hp_ref, hn_ref, dt_ref, z_ref, cw_ref, cb_ref, dtb_ref, alog_ref, dsk_ref,
                ng_ref, ex_ref, o_ref, sf_ref, sb_ref, sin_ref, ux_ref, ubc_ref, lhs_ref, pack_ref, y_ref, pad_ref,
                *, n_lat, n_ctx):
    t = pl.program_id(1)
    n_all = n_lat + n_ctx
    g_lat, g_ctx = n_lat // SSD_CPS, n_ctx // SSD_CPS
    g_all = g_lat + g_ctx
    gidx = _ssd_visit_index(t, g_lat, g_ctx)
    c_base = gidx * SSD_CPS
    sweep1 = t >= g_all
    is_lat = gidx < g_lat
    L = CHUNK
    gw = SSD_HPG * SSD_HEAD_DIM
    lane = lax.broadcasted_iota(jnp.int32, (1, LANE), 1)
    rows = lax.broadcasted_iota(jnp.int32, (L, 1), 0)
    groups = range(SSD_GROUPS)
    gsl = [slice(g * gw, (g + 1) * gw) for g in groups]

    @pl.when(t == 0)
    def _():
        sf_ref[...] = jnp.zeros_like(sf_ref)
        sb_ref[...] = jnp.zeros_like(sb_ref)
        ux_ref[n_all - 1] = jnp.zeros((CHUNK, D_SSD), F32)
        ubc_ref[n_all - 1] = jnp.zeros((CHUNK, 2 * D_BC), BF16)
        lhs_ref[n_all - 1] = jnp.zeros((CHUNK, 2 * LANE), BF16)

    def expand(lhs, block, g):
        c0 = block * D_SSD + g * gw
        return jnp.dot(lhs, ex_ref[:, c0:c0 + gw], preferred_element_type=F32)

    def update_state(state_ref, chunk, lhs_ex, g, wblock, eblock, end_row):
        r8 = (end_row // SUBLANE) * SUBLANE
        xw = (ux_ref[chunk, :, gsl[g]] * expand(lhs_ex, wblock, g)).astype(BF16)
        bg = ubc_ref[chunk, :, g * D_STATE:(g + 1) * D_STATE]
        loc = lax.dot_general(bg, xw, (((0,), (0,)), ((), ())), preferred_element_type=F32)
        tot = expand(lhs_ex[r8:r8 + SUBLANE], eblock, g)[end_row - r8:end_row - r8 + 1]
        new = state_ref[:, gsl[g]] * tot + loc
        state_ref[:, gsl[g]] = new
        return new

    @pl.when(jnp.logical_not(sweep1))
    def _():
        is_first = (gidx == 0) | (gidx == g_lat)
        is_last = (gidx == g_lat - 1) | (gidx == g_all - 1)
        span = SSD_CPS * L

        for k in range(D_XBC // LANE):
            lanes = slice(k * LANE, (k + 1) * LANE)
            pad_ref[k, SUBLANE:SUBLANE + span, :] = xbc_ref[:, lanes]
            pad_ref[k, SUBLANE - 1:SUBLANE, :] = jnp.where(is_first, 0.0, hp_ref[SUBLANE - 1:SUBLANE, lanes])
            pad_ref[k, SUBLANE + span:SUBLANE + span + 1, :] = jnp.where(is_last, 0.0, hn_ref[0:1, lanes])

        def conv_strip(u, j):
            r0 = SUBLANE + u * L
            for k in range(j * gw // LANE, (j + 1) * gw // LANE):
                lanes = slice(k * LANE, (k + 1) * LANE)
                v = (pad_ref[k, r0 - 1:r0 - 1 + L, :] * cw_ref[0:1, lanes] + pad_ref[k, r0:r0 + L, :] * cw_ref[1:2, lanes]
                     + pad_ref[k, r0 + 1:r0 + 1 + L, :] * cw_ref[2:3, lanes] + cb_ref[:, lanes])
                v = _silu(v)
                if k * LANE < D_SSD:
                    ux_ref[c_base + u, :, lanes] = v
                else:
                    ubc_ref[c_base + u, :, k * LANE - D_SSD:(k + 1) * LANE - D_SSD] = v.astype(BF16)

        def decay_tables(u):
            dtr = dt_ref[u * L:(u + 1) * L, :] + dtb_ref[...]
            dt = jnp.maximum(dtr, 0.0) + jnp.log(1.0 + jnp.exp(-jnp.abs(dtr)))
            a = dt * (-jnp.exp(alog_ref[...]))
            pre, suf = a, a
            d = 1
            while d < L:
                pre = pre + jnp.where(rows >= d, pltpu.roll(pre, d, 0), 0.0)
                suf = suf + jnp.where(rows < L - d, pltpu.roll(suf, L - d, 0), 0.0)
                d *= 2
            is_fwd_lane = lane < SSD_HEADS
            cs = jnp.where(is_fwd_lane, pre, suf)
            tot = jnp.where(is_fwd_lane, cs[L - 1:L, :], cs[0:1, :])
            e = jnp.exp(cs)
            wdt = jnp.exp(tot - cs) * dt
            pack_ref[c_base + u] = jnp.where(lane < 2 * SSD_HEADS, cs * LOG2E, pltpu.roll(dt, 2 * SSD_HEADS, 1))
            fac = jnp.where(lane < 2 * SSD_HEADS, wdt,
                            jnp.where(lane < 4 * SSD_HEADS, pltpu.roll(e, 2 * SSD_HEADS, 1), 0.0))
            hi = fac.astype(BF16).astype(F32)
            r1 = fac - hi
            mid = r1.astype(BF16).astype(F32)
            lo = (r1 - mid).astype(BF16)
            lhs_ref[c_base + u] = jnp.concatenate([(hi + pltpu.roll(mid, LANE // 2, 1)).astype(BF16), lo], axis=1)

        for u in reversed(range(SSD_CPS)):
            cidx = c_base + u
            done = jnp.minimum(cidx + 1, n_all - 1)
            lhs_done = lhs_ref[done]
            wexp = [expand(lhs_done, 1, g) for g in groups]
            tots = [expand(lhs_done[0:SUBLANE], 3, g)[0:1] for g in groups]
            conv_strip(u, 0)
            conv_strip(u, 1)
            xws = [(ux_ref[done, :, gsl[g]] * wexp[g]).astype(BF16) for g in groups]
            locs = [lax.dot_general(ubc_ref[done, :, g * D_STATE:(g + 1) * D_STATE], xws[g],
                                    (((0,), (0,)), ((), ())), preferred_element_type=F32) for g in groups]
            conv_strip(u, 2)
            conv_strip(u, 3)
            conv_strip(u, 4)
            for g in groups:
                new = sb_ref[:, gsl[g]] * tots[g] + locs[g]
                sb_ref[:, gsl[g]] = new
                sin_ref[jnp.minimum(cidx, n_lat - 1), :, gsl[g]] = new.astype(BF16)
            conv_strip(u, 5)
            conv_strip(u, 6)
            conv_strip(u, 7)
            decay_tables(u)

    @pl.when(sweep1 & jnp.logical_not(is_lat))
    def _():
        for u in range(SSD_CPS):
            lhs_ex = lhs_ref[c_base + u]
            for g in groups:
                update_state(sf_ref, c_base + u, lhs_ex, g, 0, 2, L - 1)

    @pl.when(sweep1 & is_lat)
    def _():
        li = lax.broadcasted_iota(jnp.int32, (L, L), 0)
        si = lax.broadcasted_iota(jnp.int32, (L, L), 1)
        masks = [si <= li, si >= li]
        head_of_lane = lax.broadcasted_iota(jnp.int32, (1, gw), 1) // SSD_HEAD_DIM
        lhs_exs = [lhs_ref[c_base + u] for u in range(SSD_CPS)]
        packs = [pack_ref[c_base + u] for u in range(SSD_CPS)]
        pack_ts = [p.T for p in packs]

        def early(u, g):
            cidx = c_base + u
            bg = ubc_ref[cidx, :, g * D_STATE:(g + 1) * D_STATE]
            cg = ubc_ref[cidx, :, D_BC + g * D_STATE:D_BC + (g + 1) * D_STATE]
            cb = lax.dot_general(cg, bg, (((1,), (1,)), ((), ())), preferred_element_type=F32)
            cbm = [jnp.where(m, cb, 0.0) for m in masks]
            y_off = (jnp.dot(cg, sf_ref[:, gsl[g]].astype(BF16), preferred_element_type=F32)
                     * expand(lhs_exs[u], 2, g)
                     + jnp.dot(cg, sin_ref[cidx, :, gsl[g]], preferred_element_type=F32)
                     * expand(lhs_exs[u], 3, g))
            update_state(sf_ref, cidx, lhs_exs[u], g, 0, 2, L - 1)
            return cbm, y_off

        def main(u, g, cbm):
            xg = ux_ref[c_base + u, :, gsl[g]].astype(BF16)
            blockdiag = jnp.concatenate(
                [jnp.where(head_of_lane == hh, xg, jnp.zeros_like(xg)) for hh in range(SSD_HPG)], axis=0)
            y_diag = None
            for dirn in range(2):
                ws = []
                for hh in range(SSD_HPG):
                    ln = dirn * SSD_HEADS + g * SSD_HPG + hh
                    col = packs[u][:, ln:ln + 1]
                    row = pack_ts[u][ln:ln + 1, :]
                    dtrow = pack_ts[u][2 * SSD_HEADS + ln:2 * SSD_HEADS + ln + 1, :]
                    dec = jnp.exp2(jnp.minimum(col - row, 0.0))
                    ws.append((cbm[dirn] * dec * dtrow).astype(BF16))
                yd = jnp.dot(jnp.concatenate(ws, axis=1), blockdiag, preferred_element_type=F32)
                y_diag = yd if y_diag is None else y_diag + yd
            return y_diag

        def finish(u, g, y_off, y_diag):
            yg = ux_ref[c_base + u, :, gsl[g]] * dsk_ref[:, gsl[g]] + y_diag + y_off
            yz = yg * _silu(z_ref[u * L:(u + 1) * L, gsl[g]])
            y_ref[u * L:(u + 1) * L, gsl[g]] = yz
            return jnp.sum(yz * yz, axis=-1, keepdims=True)

        units = [(u, g) for u in range(SSD_CPS) for g in groups]
        pending = {units[0]: early(*units[0])}
        diag = {}
        ss = [jnp.zeros((L, 1), F32) for _ in range(SSD_CPS)]
        for i, (u, g) in enumerate(units):
            if i + 1 < len(units):
                pending[units[i + 1]] = early(*units[i + 1])
            diag[(u, g)] = main(u, g, pending[(u, g)][0])
            if i >= 1:
                pu, pg = units[i - 1]
                ss[pu] = ss[pu] + finish(pu, pg, pending[(pu, pg)][1], diag[(pu, pg)])
        pu, pg = units[-1]
        ss[pu] = ss[pu] + finish(pu, pg, pending[(pu, pg)][1], diag[(pu, pg)])
        for u in range(SSD_CPS):
            rs = lax.rsqrt(ss[u] * (1.0 / D_SSD) + EPS)
            for g in groups:
                o_ref[u * L:(u + 1) * L, gsl[g]] = (
                    y_ref[u * L:(u + 1) * L, gsl[g]] * rs * ng_ref[:, gsl[g]]).astype(o_ref.dtype)


def _expansion_matrix():
    r = jnp.arange(2 * LANE)[:, None]
    col = jnp.arange(4 * D_SSD)[None, :]
    src = (col // D_SSD) * SSD_HEADS + (col % D_SSD) // SSD_HEAD_DIM
    half = LANE // 2
    return ((r % half == src) & (r < 3 * half)).astype(BF16)


def _ssd(gzx, dtraw, conv_w, conv_b, dt_bias, a_log, d_skip_exp, ssd_norm_g, b, s, nc):
    t_rows = gzx.shape[0]
    span = SSD_CPS * CHUNK
    assert s % span == 0 and nc % span == 0
    n_lat, n_ctx = s // CHUNK, nc // CHUNK
    g_lat, g_ctx = s // span, nc // span
    g_all = g_lat + g_ctx
    halo_per_span = span // SUBLANE
    n_halo = t_rows // SUBLANE
    c0 = lambda t: jnp.where(t < g_all, _ssd_visit_index(t, g_lat, g_ctx), 0)
    c1 = lambda t: jnp.where(t >= g_all + g_ctx, t - g_all - g_ctx, 0)
    blk = lambda bi, c: jnp.where(c < g_lat, bi * g_lat + c, b * g_lat + bi * g_ctx + (c - g_lat))
    xbc_col = (D_ATT + D_SSD) // D_XBC
    z_col = D_ATT // D_SSD
    const = lambda shape: pl.BlockSpec(shape, lambda bi, t: tuple(0 for _ in shape))
    return pl.pallas_call(
        functools.partial(_ssd_kernel, n_lat=n_lat, n_ctx=n_ctx),
        grid=(b, 2 * g_all),
        in_specs=[pl.BlockSpec((span, D_XBC), lambda bi, t: (blk(bi, c0(t)), xbc_col)),
                  pl.BlockSpec((SUBLANE, D_XBC),
                               lambda bi, t: (jnp.maximum(blk(bi, c0(t)) * halo_per_span - 1, 0), xbc_col)),
                  pl.BlockSpec((SUBLANE, D_XBC),
                               lambda bi, t: (jnp.minimum((blk(bi, c0(t)) + 1) * halo_per_span, n_halo - 1),
                                              xbc_col)),
                  pl.BlockSpec((span, LANE), lambda bi, t: (blk(bi, c0(t)), 0)),
                  pl.BlockSpec((span, D_SSD), lambda bi, t: (bi * g_lat + c1(t), z_col)),
                  const((CONV_W, D_XBC)), const((1, D_XBC)), const((1, LANE)), const((1, LANE)),
                  const((1, D_SSD)), const((1, D_SSD)), const((2 * LANE, 4 * D_SSD))],
        out_specs=pl.BlockSpec((span, D_SSD), lambda bi, t: (bi * g_lat + c1(t), 0)),
        out_shape=jax.ShapeDtypeStruct((b * s, D_SSD), BF16),
        scratch_shapes=[pltpu.VMEM((D_STATE, D_SSD), F32),
                        pltpu.VMEM((D_STATE, D_SSD), F32),
                        pltpu.VMEM((n_lat, D_STATE, D_SSD), BF16),
                        pltpu.VMEM((n_lat + n_ctx, CHUNK, D_SSD), F32),
                        pltpu.VMEM((n_lat + n_ctx, CHUNK, 2 * D_BC), BF16),
                        pltpu.VMEM((n_lat + n_ctx, CHUNK, 2 * LANE), BF16),
                        pltpu.VMEM((n_lat + n_ctx, CHUNK, LANE), F32),
                        pltpu.VMEM((span, D_SSD), F32),
                        pltpu.VMEM((D_XBC // LANE, span + 2 * SUBLANE, LANE), F32)],
        compiler_params=_params(("arbitrary", "arbitrary"), 60),
        name="bi_ssd",
    )(gzx, gzx, gzx, dtraw, gzx, conv_w, conv_b, dt_bias, a_log, d_skip_exp, ssd_norm_g,
      _expansion_matrix())


def _out_kernel(ha_ref, hs_ref, wa_ref, ws_ref, x_ref, gate_ref, ng_ref, o_ref):
    upd = (jnp.dot(ha_ref[...], wa_ref[...], preferred_element_type=F32)
           + jnp.dot(hs_ref[...], ws_ref[...], preferred_element_type=F32))
    y = x_ref[...] + gate_ref[0] * upd
    ms = jnp.mean(y * y, axis=-1, keepdims=True)
    o_ref[...] = (y * lax.rsqrt(ms + EPS) * ng_ref[...]).astype(o_ref.dtype)


def _out_proj(h_att, h_ssd, w_out_bf, x2d, gate, final_norm_g, s):
    m, d = x2d.shape
    tm = _row_tile(s, (512, 256, 128))
    tiles_per_batch = s // tm
    return pl.pallas_call(
        _out_kernel,
        grid=(m // tm,),
        in_specs=[pl.BlockSpec((tm, D_ATT), lambda i: (i, 0)),
                  pl.BlockSpec((tm, D_SSD), lambda i: (i, 0)),
                  pl.BlockSpec((D_ATT, d), lambda i: (0, 0)),
                  pl.BlockSpec((D_SSD, d), lambda i: (D_ATT // D_SSD, 0)),
                  pl.BlockSpec((tm, d), lambda i: (i, 0)),
                  pl.BlockSpec((1, 1, d), lambda i: (i // tiles_per_batch, 0, 0)),
                  pl.BlockSpec((1, d), lambda i: (0, 0))],
        out_specs=pl.BlockSpec((tm, d), lambda i: (i, 0)),
        out_shape=jax.ShapeDtypeStruct((m, d), x2d.dtype),
        compiler_params=_params(("parallel",), 48),
        name="out_proj",
    )(h_att, h_ssd, w_out_bf, w_out_bf, x2d, gate, final_norm_g)


def _rope_tables(s, pad_rows):
    pos = np.arange(s)
    row = (pos // GRID_W).astype(np.float32)
    colp = (pos % GRID_W).astype(np.float32)
    quarter = HEAD_DIM // 4
    freq = (1.0 / (np.float32(ROPE_THETA) ** (np.arange(quarter, dtype=np.float32) / np.float32(quarter)))
            ).astype(np.float32)
    lane = np.arange(LANE)
    dim = lane % HEAD_DIM
    use_col = (dim // (HEAD_DIM // 2)) == 1
    r = dim % (HEAD_DIM // 2)
    first = r < quarter
    ang = (np.where(use_col[None, :], colp[:, None], row[:, None]) * freq[r % quarter][None, :]).astype(np.float32)
    cos, sin = np.cos(ang), np.sin(ang)
    sa = np.where(first[None, :], -sin, 0.0)
    sb = np.where(first[None, :], 0.0, sin)
    ident = lambda v, fill: jnp.asarray(
        np.concatenate([v, np.full((pad_rows, LANE), fill)], axis=0).astype(np.float32))
    return ident(cos, 1.0), ident(sa, 0.0), ident(sb, 0.0)


def kernel(x, c, ctx, c_ctx, w_mod, b_mod, norm_g, w_in, conv_w, conv_b, a_log_f, a_log_b, dt_bias_f,
           dt_bias_b, d_skip, att_norm_g, ssd_norm_g, sink, w_out, final_norm_g):
    assert w_mod.shape[0] == 1, "single-layer operation"
    b, s, d = x.shape
    nc = ctx.shape[1]

    pad_rows = (-(b + 1)) % SUBLANE
    cc = jnp.concatenate([c, c_ctx[None, :], jnp.zeros((pad_rows, d), F32)], axis=0)
    mod = _modulation(cc, w_mod[0], b_mod)
    shift = mod[:b + 1, :d].reshape(b + 1, 1, d)
    scale = mod[:b + 1, d:2 * d].reshape(b + 1, 1, d)
    gate = mod[:b, 2 * d:].reshape(b, 1, d)

    w_t = jnp.swapaxes(w_in[0], 0, 1).astype(BF16)
    w_dt = jnp.pad(w_t[D_QKV + D_GZX:], ((0, LANE - 2 * SSD_HEADS), (0, 0)))
    w_out_bf = w_out[0].astype(BF16)

    cos_t, sa_t, sb_t = _rope_tables(s, NORM_TILE)
    qkv, xn, dtraw = _norm_qkv(x.reshape(b * s, d), ctx.reshape(b * nc, d), shift, scale, norm_g, w_dt, w_t,
                               cos_t, sa_t, sb_t, s)

    gzx = _matmul(xn, w_t, D_QKV, D_GZX, F32)

    h_att = _attention(sink[0], qkv, gzx, att_norm_g, b, s, nc)

    lane_pad = LANE - 2 * SSD_HEADS
    dt_bias = jnp.pad(jnp.concatenate([dt_bias_f[0], dt_bias_b[0]]), (0, lane_pad))[None, :]
    a_log = jnp.pad(jnp.concatenate([a_log_f[0], a_log_b[0]]), (0, lane_pad))[None, :]
    d_skip_exp = jnp.repeat(d_skip[0], SSD_HEAD_DIM)[None, :]
    h_ssd = _ssd(gzx, dtraw, conv_w[0], conv_b, dt_bias, a_log, d_skip_exp, ssd_norm_g, b, s, nc)

    out = _out_proj(h_att, h_ssd, w_out_bf, x.reshape(b * s, d), gate, final_norm_g[None, :], s)
    return out.reshape(b, s, d)
```

```python
import functools

import jax
import jax.numpy as jnp
import numpy as np
from jax import lax
from jax.experimental import pallas as pl
from jax.experimental.pallas import tpu as pltpu

F32 = jnp.float32
BF16 = jnp.bfloat16

EPS = 1e-6
GRID_W = 64
ROPE_THETA = 10000.0
ATT_HEADS = 16
ATT_KV_HEADS = 4
HEAD_DIM = 64
D_ATT = ATT_HEADS * HEAD_DIM
D_KV = ATT_KV_HEADS * HEAD_DIM
Q_PER_KV = ATT_HEADS // ATT_KV_HEADS
BLOCK = 128
SSD_HEADS = 16
SSD_HEAD_DIM = 64
D_SSD = SSD_HEADS * SSD_HEAD_DIM
SSD_GROUPS = 4
SSD_HPG = SSD_HEADS // SSD_GROUPS
D_STATE = 128
D_BC = SSD_GROUPS * D_STATE
D_XBC = D_SSD + 2 * D_BC
CONV_W = 3
CHUNK = 128
D_QKV = D_ATT + 2 * D_KV
D_GZX = D_ATT + D_SSD + D_XBC

LANE = 128
SUBLANE = 8
V7X_VMEM_BYTES = 64 * 1024 * 1024
MIB = 1024 * 1024

NEG = -1e30
LOG2E = 1.4426950408889634


def _silu(v):
    return v / (1.0 + jnp.exp(-v))


def _dot_nt(a, w):
    return lax.dot_general(a, w, (((1,), (1,)), ((), ())), preferred_element_type=F32)


def _params(semantics, vmem_mib):
    return pltpu.CompilerParams(dimension_semantics=semantics, vmem_limit_bytes=vmem_mib * MIB)


def _row_tile(m, candidates=(1024, 512, 256, 128)):
    for t in candidates:
        if m % t == 0:
            return t
    raise ValueError(f"row count {m} not tileable")


def _mod_kernel(c_ref, w_ref, b_ref, o_ref):
    a = _silu(c_ref[...]).astype(BF16)
    o_ref[...] = jnp.dot(a, w_ref[...].astype(BF16), preferred_element_type=F32) + b_ref[...]


def _modulation(cc, w_mod, b_mod):
    rows, d = cc.shape
    n = w_mod.shape[1]
    tn = 768
    assert n % tn == 0
    return pl.pallas_call(
        _mod_kernel,
        grid=(n // tn,),
        in_specs=[pl.BlockSpec((rows, d), lambda j: (0, 0)),
                  pl.BlockSpec((d, tn), lambda j: (0, j)),
                  pl.BlockSpec((1, tn), lambda j: (0, j))],
        out_specs=pl.BlockSpec((rows, tn), lambda j: (0, j)),
        out_shape=jax.ShapeDtypeStruct((rows, n), F32),
        compiler_params=_params(("parallel",), 32),
        name="modulation",
    )(cc, w_mod, b_mod)


def _norm_qkv_kernel(x_ref, ctx_ref, shift_ref, scale_ref, g_ref, wdt_ref, w_ref, cos_ref, sa_ref, sb_ref,
                     qkv_ref, xn_ref, dt_ref, ya_ref, yb_ref, *, n_lat_tiles, n_tiles, n_rope, n_q):
    i = pl.program_id(0)
    is_lat = jnp.minimum(i, n_tiles - 1) < n_lat_tiles

    @pl.when(i == 0)
    def _():
        yb_ref[...] = jnp.zeros_like(yb_ref)

    def step(y_prev_ref, y_next_ref):
        y = y_prev_ref[...]
        dt_ref[...] = _dot_nt(y, wdt_ref[...])
        acc = _dot_nt(y, w_ref[...])
        cos, sa, sb = cos_ref[...], sa_ref[...], sb_ref[...]
        quarter = HEAD_DIM // 4
        for jb in range(acc.shape[1] // LANE):
            blk = acc[:, jb * LANE:(jb + 1) * LANE]
            if jb < n_rope:
                blk = (blk * cos + pltpu.roll(blk, LANE - quarter, 1) * sa
                       + pltpu.roll(blk, quarter, 1) * sb)
            if jb < n_q:
                blk = blk * (HEAD_DIM ** -0.5 * LOG2E)
            qkv_ref[:, jb * LANE:(jb + 1) * LANE] = blk.astype(qkv_ref.dtype)

        v = jnp.where(is_lat, x_ref[...], ctx_ref[...])
        ms = jnp.mean(v * v, axis=-1, keepdims=True)
        gain = g_ref[...] * (1.0 + scale_ref[0])
        yn = (v * lax.rsqrt(ms + EPS) * gain + shift_ref[0]).astype(BF16)
        xn_ref[...] = yn
        y_next_ref[...] = yn

    @pl.when(i % 2 == 0)
    def _():
        step(yb_ref, ya_ref)

    @pl.when(i % 2 == 1)
    def _():
        step(ya_ref, yb_ref)


NORM_TILE = 512


def _norm_qkv(x2d, ctx2d, shift, scale, norm_g, w_dt, w_t, cos_t, sa_t, sb_t, s):
    m_lat, d = x2d.shape
    m_ctx = ctx2d.shape[0]
    b = m_lat // s
    tr = NORM_TILE
    assert s % tr == 0 and m_ctx % tr == 0
    tiles_per_seq = s // tr
    n_lat_tiles, n_ctx_tiles = m_lat // tr, m_ctx // tr
    n_tiles = n_lat_tiles + n_ctx_tiles
    cur = lambda i: jnp.minimum(i, n_tiles - 1)
    prev = lambda i: jnp.maximum(i - 1, 0)
    mod_idx = lambda i: (jnp.where(cur(i) < n_lat_tiles, cur(i) // tiles_per_seq, b), 0, 0)
    tab = pl.BlockSpec((tr, LANE), lambda i: (jnp.where(prev(i) < n_lat_tiles, prev(i) % tiles_per_seq,
                                                        tiles_per_seq), 0))
    out_prev = lambda width: pl.BlockSpec((tr, width), lambda i: (prev(i), 0))
    const = lambda shape: pl.BlockSpec(shape, lambda i: tuple(0 for _ in shape))
    return pl.pallas_call(
        functools.partial(_norm_qkv_kernel, n_lat_tiles=n_lat_tiles, n_tiles=n_tiles,
                          n_rope=(D_ATT + D_KV) // LANE, n_q=D_ATT // LANE),
        grid=(n_tiles + 1,),
        in_specs=[pl.BlockSpec((tr, d), lambda i: (jnp.minimum(cur(i), n_lat_tiles - 1), 0)),
                  pl.BlockSpec((tr, d), lambda i: (jnp.maximum(cur(i) - n_lat_tiles, 0), 0)),
                  pl.BlockSpec((1, 1, d), mod_idx),
                  pl.BlockSpec((1, 1, d), mod_idx),
                  const((1, d)), const((LANE, d)), const((D_QKV, d)),
                  tab, tab, tab],
        out_specs=[out_prev(D_QKV),
                   pl.BlockSpec((tr, d), lambda i: (cur(i), 0)),
                   out_prev(LANE)],
        out_shape=[jax.ShapeDtypeStruct((m_lat + m_ctx, D_QKV), BF16),
                   jax.ShapeDtypeStruct((m_lat + m_ctx, d), BF16),
                   jax.ShapeDtypeStruct((m_lat + m_ctx, LANE), F32)],
        scratch_shapes=[pltpu.VMEM((tr, d), BF16), pltpu.VMEM((tr, d), BF16)],
        compiler_params=_params(("arbitrary",), 56),
        name="norm_qkv",
    )(x2d, ctx2d, shift, scale, norm_g, w_dt, w_t, cos_t, sa_t, sb_t)


def _mm_kernel(x_ref, w_ref, o_ref):
    o_ref[...] = _dot_nt(x_ref[...], w_ref[...]).astype(o_ref.dtype)


def _matmul(x2d, w_t, col0, n, out_dtype, tn=2048):
    m, d = x2d.shape
    tm = _row_tile(m)
    assert n % tn == 0 and col0 % LANE == 0
    return pl.pallas_call(
        _mm_kernel,
        grid=(n // tn, m // tm),
        in_specs=[pl.BlockSpec((tm, d), lambda j, i: (i, 0)),
                  pl.BlockSpec((pl.Element(tn), pl.Element(d)),
                               lambda j, i: (pl.multiple_of(col0 + j * tn, LANE), 0))],
        out_specs=pl.BlockSpec((tm, tn), lambda j, i: (i, j)),
        out_shape=jax.ShapeDtypeStruct((m, n), out_dtype),
        compiler_params=_params(("parallel", "parallel"), 56),
        name="gzx_proj",
    )(x2d, w_t)


ATT_QBLOCKS = 8


def _attn_kernel(sink_ref, q_ref, kp_ref, ko_ref, kn_ref, kc_ref, vp_ref, vo_ref, vn_ref, vc_ref,
                 g_ref, ng_ref, o_ref, *, n_steps):
    n = pl.program_id(1)
    nctx = kc_ref.shape[0]
    nk = 3 * BLOCK + nctx
    cols_g = Q_PER_KV * BLOCK
    k_blocks = [kp_ref[...]] + [ko_ref[u * BLOCK:(u + 1) * BLOCK] for u in range(ATT_QBLOCKS)] + [kn_ref[...]]
    v_blocks = [vp_ref[...]] + [vo_ref[u * BLOCK:(u + 1) * BLOCK] for u in range(ATT_QBLOCKS)] + [vn_ref[...]]

    kj = lax.broadcasted_iota(jnp.int32, (BLOCK, BLOCK), 0)
    qi = lax.broadcasted_iota(jnp.int32, (BLOCK, BLOCK), 1)
    tile = lambda v: jnp.concatenate([v] * Q_PER_KV, axis=1)
    tri_prev = tile(jnp.where(kj >= qi, 0.0, NEG).astype(F32))
    tri_next = tile(jnp.where(kj <= qi, 0.0, NEG).astype(F32))
    edge_prev = tile(jnp.where((kj >= qi) & (n > 0), 0.0, NEG).astype(F32))
    edge_next = tile(jnp.where((kj <= qi) & (n < n_steps - 1), 0.0, NEG).astype(F32))
    head_of_col = lax.broadcasted_iota(jnp.int32, (1, cols_g), 1) // BLOCK
    vlane = lax.broadcasted_iota(jnp.int32, (nk, LANE - HEAD_DIM), 1)
    v_tail = jnp.where(vlane == 0, 1.0, 0.0).astype(BF16)

    def scores_t(u, kh):
        hs = slice(kh * HEAD_DIM, (kh + 1) * HEAD_DIM)
        k_h = jnp.concatenate([blk[:, hs] for blk in k_blocks[u:u + 3]] + [kc_ref[:, hs]], axis=0)
        qg = jnp.concatenate(
            [q_ref[u * BLOCK:(u + 1) * BLOCK, (kh * Q_PER_KV + i) * HEAD_DIM:(kh * Q_PER_KV + i + 1) * HEAD_DIM]
             for i in range(Q_PER_KV)], axis=0)
        return lax.dot_general(k_h, qg, (((1,), (1,)), ((), ())), preferred_element_type=F32)

    def attend(u, kh, st):
        bias_prev = edge_prev if u == 0 else tri_prev
        bias_next = edge_next if u == ATT_QBLOCKS - 1 else tri_next
        st = jnp.concatenate([st[:BLOCK] + bias_prev, st[BLOCK:2 * BLOCK],
                              st[2 * BLOCK:3 * BLOCK] + bias_next, st[3 * BLOCK:]], axis=0)
        sink_row = jnp.zeros((1, cols_g), F32)
        for i in range(Q_PER_KV):
            sink_row = jnp.where(head_of_col == i, sink_ref[kh * Q_PER_KV + i] * LOG2E, sink_row)
        m = jnp.maximum(jnp.max(st, axis=0, keepdims=True), sink_row)
        pt = jnp.exp2(st - m).astype(BF16)
        hs = slice(kh * HEAD_DIM, (kh + 1) * HEAD_DIM)
        v_h = jnp.concatenate([blk[:, hs] for blk in v_blocks[u:u + 3]] + [vc_ref[:, hs]], axis=0)
        v_ext = jnp.concatenate([v_h, v_tail], axis=1)
        ot = lax.dot_general(v_ext, pt, (((0,), (0,)), ((), ())), preferred_element_type=F32)
        inv = 1.0 / (ot[HEAD_DIM:HEAD_DIM + 1] + jnp.exp2(sink_row - m))
        on = ot[:HEAD_DIM] * inv
        pairs = [jnp.concatenate([on[:, (2 * j) * BLOCK:(2 * j + 1) * BLOCK],
                                  on[:, (2 * j + 1) * BLOCK:(2 * j + 2) * BLOCK]], axis=0).T
                 for j in range(Q_PER_KV // 2)]
        return jnp.concatenate(pairs, axis=1)

    def finish(u, groups):
        rows = slice(u * BLOCK, (u + 1) * BLOCK)
        a = jnp.concatenate(groups, axis=1) * _silu(g_ref[rows])
        ms = jnp.mean(a * a, axis=-1, keepdims=True)
        o_ref[rows] = (a * lax.rsqrt(ms + EPS) * ng_ref[...]).astype(o_ref.dtype)

    order = [(u, kh) for u in range(ATT_QBLOCKS) for kh in range(ATT_KV_HEADS)]
    ahead = 1
    pending = [scores_t(*order[i]) for i in range(ahead)]
    groups = []
    for idx, (u, kh) in enumerate(order):
        s_cur = pending.pop(0)
        if idx + ahead < len(order):
            pending.append(scores_t(*order[idx + ahead]))
        groups.append(attend(u, kh, s_cur))
        if kh == ATT_KV_HEADS - 1:
            finish(u, groups)
            groups = []


def _attention(sink, qkv, gzx, att_norm_g, b, s, nc):
    rows = ATT_QBLOCKS * BLOCK
    assert s % rows == 0 and (b * s) % nc == 0
    nb = s // BLOCK
    n_steps = s // rows
    kcol, vcol = D_ATT // D_KV, D_ATT // D_KV + 1
    before = lambda col: pl.BlockSpec(
        (BLOCK, D_KV), lambda bi, n: (bi * nb + jnp.maximum(ATT_QBLOCKS * n - 1, 0), col))
    own = lambda col: pl.BlockSpec((rows, D_KV), lambda bi, n: (bi * n_steps + n, col))
    after = lambda col: pl.BlockSpec(
        (BLOCK, D_KV), lambda bi, n: (bi * nb + jnp.minimum(ATT_QBLOCKS * (n + 1), nb - 1), col))
    ctx_spec = lambda col: pl.BlockSpec((nc, D_KV), lambda bi, n: (b * s // nc + bi, col))
    q_rows = pl.BlockSpec((rows, D_ATT), lambda bi, n: (bi * n_steps + n, 0))
    return pl.pallas_call(
        functools.partial(_attn_kernel, n_steps=n_steps),
        grid=(b, n_steps),
        in_specs=[pl.BlockSpec(memory_space=pltpu.SMEM),
                  q_rows,
                  before(kcol), own(kcol), after(kcol), ctx_spec(kcol),
                  before(vcol), own(vcol), after(vcol), ctx_spec(vcol),
                  q_rows,
                  pl.BlockSpec((1, D_ATT), lambda bi, n: (0, 0))],
        out_specs=q_rows,
        out_shape=jax.ShapeDtypeStruct((b * s, D_ATT), BF16),
        compiler_params=_params(("parallel", "parallel"), 40),
        name="window_attn",
    )(sink, qkv, qkv, qkv, qkv, qkv, qkv, qkv, qkv, qkv, gzx, att_norm_g)


SSD_CPS = 2


def _ssd_visit_index(t, n_lat, n_ctx):
    n_all = n_lat + n_ctx
    k = t - n_all
    fwd = jnp.where(k < n_ctx, n_lat + k, k - n_ctx)
    return jnp.where(t >= n_all, fwd, n_all - 1 - t)


def _ssd_kernel(xbc_ref, hp_ref, hn_ref, dt_ref, z_ref, cw_ref, cb_ref, dtb_ref, alog_ref, dsk_ref,
                ng_ref, ex_ref, o_ref, sf_ref, sb_ref, sin_ref, ux_ref, ubc_ref, lhs_ref, pack_ref, y_ref, pad_ref,
                *, n_lat, n_ctx):
    t = pl.program_id(1)
    n_all = n_lat + n_ctx
    g_lat, g_ctx = n_lat // SSD_CPS, n_ctx // SSD_CPS
    g_all = g_lat + g_ctx
    gidx = _ssd_visit_index(t, g_lat, g_ctx)
    c_base = gidx * SSD_CPS
    sweep1 = t >= g_all
    is_lat = gidx < g_lat
    L = CHUNK
    gw = SSD_HPG * SSD_HEAD_DIM
    lane = lax.broadcasted_iota(jnp.int32, (1, LANE), 1)
    rows = lax.broadcasted_iota(jnp.int32, (L, 1), 0)
    groups = range(SSD_GROUPS)
    gsl = [slice(g * gw, (g + 1) * gw) for g in groups]

    @pl.when(t == 0)
    def _():
        sf_ref[...] = jnp.zeros_like(sf_ref)
        sb_ref[...] = jnp.zeros_like(sb_ref)
        ux_ref[n_all - 1] = jnp.zeros((CHUNK, D_SSD), F32)
        ubc_ref[n_all - 1] = jnp.zeros((CHUNK, 2 * D_BC), BF16)
        lhs_ref[n_all - 1] = jnp.zeros((CHUNK, 2 * LANE), BF16)

    def expand(lhs, block, g):
        c0 = block * D_SSD + g * gw
        return jnp.dot(lhs, ex_ref[:, c0:c0 + gw], preferred_element_type=F32)

    def update_state(state_ref, chunk, lhs_ex, g, wblock, eblock, end_row):
        r8 = (end_row // SUBLANE) * SUBLANE
        xw = (ux_ref[chunk, :, gsl[g]] * expand(lhs_ex, wblock, g)).astype(BF16)
        bg = ubc_ref[chunk, :, g * D_STATE:(g + 1) * D_STATE]
        loc = lax.dot_general(bg, xw, (((0,), (0,)), ((), ())), preferred_element_type=F32)
        tot = expand(lhs_ex[r8:r8 + SUBLANE], eblock, g)[end_row - r8:end_row - r8 + 1]
        new = state_ref[:, gsl[g]] * tot + loc
        state_ref[:, gsl[g]] = new
        return new

    @pl.when(jnp.logical_not(sweep1))
    def _():
        is_first = (gidx == 0) | (gidx == g_lat)
        is_last = (gidx == g_lat - 1) | (gidx == g_all - 1)
        span = SSD_CPS * L

        for k in range(D_XBC // LANE):
            lanes = slice(k * LANE, (k + 1) * LANE)
            pad_ref[k, SUBLANE:SUBLANE + span, :] = xbc_ref[:, lanes]
            pad_ref[k, SUBLANE - 1:SUBLANE, :] = jnp.where(is_first, 0.0, hp_ref[SUBLANE - 1:SUBLANE, lanes])
            pad_ref[k, SUBLANE + span:SUBLANE + span + 1, :] = jnp.where(is_last, 0.0, hn_ref[0:1, lanes])

        def conv_strip(u, j):
            r0 = SUBLANE + u * L
            for k in range(j * gw // LANE, (j + 1) * gw // LANE):
                lanes = slice(k * LANE, (k + 1) * LANE)
                v = (pad_ref[k, r0 - 1:r0 - 1 + L, :] * cw_ref[0:1, lanes] + pad_ref[k, r0:r0 + L, :] * cw_ref[1:2, lanes]
                     + pad_ref[k, r0 + 1:r0 + 1 + L, :] * cw_ref[2:3, lanes] + cb_ref[:, lanes])
                v = _silu(v)
                if k * LANE < D_SSD:
                    ux_ref[c_base + u, :, lanes] = v
                else:
                    ubc_ref[c_base + u, :, k * LANE - D_SSD:(k + 1) * LANE - D_SSD] = v.astype(BF16)

        def decay_tables(u):
            dtr = dt_ref[u * L:(u + 1) * L, :] + dtb_ref[...]
            dt = jnp.maximum(dtr, 0.0) + jnp.log(1.0 + jnp.exp(-jnp.abs(dtr)))
            a = dt * (-jnp.exp(alog_ref[...]))
            pre, suf = a, a
            d = 1
            while d < L:
                pre = pre + jnp.where(rows >= d, pltpu.roll(pre, d, 0), 0.0)
                suf = suf + jnp.where(rows < L - d, pltpu.roll(suf, L - d, 0), 0.0)
                d *= 2
            is_fwd_lane = lane < SSD_HEADS
            cs = jnp.where(is_fwd_lane, pre, suf)
            tot = jnp.where(is_fwd_lane, cs[L - 1:L, :], cs[0:1, :])
            e = jnp.exp(cs)
            wdt = jnp.exp(tot - cs) * dt
            pack_ref[c_base + u] = jnp.where(lane < 2 * SSD_HEADS, cs * LOG2E, pltpu.roll(dt, 2 * SSD_HEADS, 1))
            fac = jnp.where(lane < 2 * SSD_HEADS, wdt,
                            jnp.where(lane < 4 * SSD_HEADS, pltpu.roll(e, 2 * SSD_HEADS, 1), 0.0))
            hi = fac.astype(BF16).astype(F32)
            r1 = fac - hi
            mid = r1.astype(BF16).astype(F32)
            lo = (r1 - mid).astype(BF16)
            lhs_ref[c_base + u] = jnp.concatenate([(hi + pltpu.roll(mid, LANE // 2, 1)).astype(BF16), lo], axis=1)

        for u in reversed(range(SSD_CPS)):
            cidx = c_base + u
            done = jnp.minimum(cidx + 1, n_all - 1)
            lhs_done = lhs_ref[done]
            wexp = [expand(lhs_done, 1, g) for g in groups]
            tots = [expand(lhs_done[0:SUBLANE], 3, g)[0:1] for g in groups]
            conv_strip(u, 0)
            conv_strip(u, 1)
            xws = [(ux_ref[done, :, gsl[g]] * wexp[g]).astype(BF16) for g in groups]
            locs = [lax.dot_general(ubc_ref[done, :, g * D_STATE:(g + 1) * D_STATE], xws[g],
                                    (((0,), (0,)), ((), ())), preferred_element_type=F32) for g in groups]
            conv_strip(u, 2)
            conv_strip(u, 3)
            conv_strip(u, 4)
            for g in groups:
                new = sb_ref[:, gsl[g]] * tots[g] + locs[g]
                sb_ref[:, gsl[g]] = new
                sin_ref[jnp.minimum(cidx, n_lat - 1), :, gsl[g]] = new.astype(BF16)
            conv_strip(u, 5)
            conv_strip(u, 6)
            conv_strip(u, 7)
            decay_tables(u)

    @pl.when(sweep1 & jnp.logical_not(is_lat))
    def _():
        for u in range(SSD_CPS):
            lhs_ex = lhs_ref[c_base + u]
            for g in groups:
                update_state(sf_ref, c_base + u, lhs_ex, g, 0, 2, L - 1)

    @pl.when(sweep1 & is_lat)
    def _():
        li = lax.broadcasted_iota(jnp.int32, (L, L), 0)
        si = lax.broadcasted_iota(jnp.int32, (L, L), 1)
        masks = [si <= li, si >= li]
        head_of_lane = lax.broadcasted_iota(jnp.int32, (1, gw), 1) // SSD_HEAD_DIM
        lhs_exs = [lhs_ref[c_base + u] for u in range(SSD_CPS)]
        packs = [pack_ref[c_base + u] for u in range(SSD_CPS)]
        pack_ts = [p.T for p in packs]

        def early(u, g):
            cidx = c_base + u
            bg = ubc_ref[cidx, :, g * D_STATE:(g + 1) * D_STATE]
            cg = ubc_ref[cidx, :, D_BC + g * D_STATE:D_BC + (g + 1) * D_STATE]
            cb = lax.dot_general(cg, bg, (((1,), (1,)), ((), ())), preferred_element_type=F32)
            cbm = [jnp.where(m, cb, 0.0) for m in masks]
            y_off = (jnp.dot(cg, sf_ref[:, gsl[g]].astype(BF16), preferred_element_type=F32)
                     * expand(lhs_exs[u], 2, g)
                     + jnp.dot(cg, sin_ref[cidx, :, gsl[g]], preferred_element_type=F32)
                     * expand(lhs_exs[u], 3, g))
            update_state(sf_ref, cidx, lhs_exs[u], g, 0, 2, L - 1)
            return cbm, y_off

        def main(u, g, cbm):
            xg = ux_ref[c_base + u, :, gsl[g]].astype(BF16)
            blockdiag = jnp.concatenate(
                [jnp.where(head_of_lane == hh, xg, jnp.zeros_like(xg)) for hh in range(SSD_HPG)], axis=0)
            y_diag = None
            for dirn in range(2):
                ws = []
                for hh in range(SSD_HPG):
                    ln = dirn * SSD_HEADS + g * SSD_HPG + hh
                    col = packs[u][:, ln:ln + 1]
                    row = pack_ts[u][ln:ln + 1, :]
                    dtrow = pack_ts[u][2 * SSD_HEADS + ln:2 * SSD_HEADS + ln + 1, :]
                    dec = jnp.exp2(jnp.minimum(col - row, 0.0))
                    ws.append((cbm[dirn] * dec * dtrow).astype(BF16))
                yd = jnp.dot(jnp.concatenate(ws, axis=1), blockdiag, preferred_element_type=F32)
                y_diag = yd if y_diag is None else y_diag + yd
            return y_diag

        def finish(u, g, y_off, y_diag):
            yg = ux_ref[c_base + u, :, gsl[g]] * dsk_ref[:, gsl[g]] + y_diag + y_off
            yz = yg * _silu(z_ref[u * L:(u + 1) * L, gsl[g]])
            y_ref[u * L:(u + 1) * L, gsl[g]] = yz
            return jnp.sum(yz * yz, axis=-1, keepdims=True)

        units = [(u, g) for u in range(SSD_CPS) for g in groups]
        pending = {units[0]: early(*units[0])}
        diag = {}
        ss = [jnp.zeros((L, 1), F32) for _ in range(SSD_CPS)]
        for i, (u, g) in enumerate(units):
            if i + 1 < len(units):
                pending[units[i + 1]] = early(*units[i + 1])
            diag[(u, g)] = main(u, g, pending[(u, g)][0])
            if i >= 1:
                pu, pg = units[i - 1]
                ss[pu] = ss[pu] + finish(pu, pg, pending[(pu, pg)][1], diag[(pu, pg)])
        pu, pg = units[-1]
        ss[pu] = ss[pu] + finish(pu, pg, pending[(pu, pg)][1], diag[(pu, pg)])
        for u in range(SSD_CPS):
            rs = lax.rsqrt(ss[u] * (1.0 / D_SSD) + EPS)
            for g in groups:
                o_ref[u * L:(u + 1) * L, gsl[g]] = (
                    y_ref[u * L:(u + 1) * L, gsl[g]] * rs * ng_ref[:, gsl[g]]).astype(o_ref.dtype)


def _expansion_matrix():
    r = jnp.arange(2 * LANE)[:, None]
    col = jnp.arange(4 * D_SSD)[None, :]
    src = (col // D_SSD) * SSD_HEADS + (col % D_SSD) // SSD_HEAD_DIM
    half = LANE // 2
    return ((r % half == src) & (r < 3 * half)).astype(BF16)


def _ssd(gzx, dtraw, conv_w, conv_b, dt_bias, a_log, d_skip_exp, ssd_norm_g, b, s, nc):
    t_rows = gzx.shape[0]
    span = SSD_CPS * CHUNK
    assert s % span == 0 and nc % span == 0
    n_lat, n_ctx = s // CHUNK, nc // CHUNK
    g_lat, g_ctx = s // span, nc // span
    g_all = g_lat + g_ctx
    halo_per_span = span // SUBLANE
    n_halo = t_rows // SUBLANE
    c0 = lambda t: jnp.where(t < g_all, _ssd_visit_index(t, g_lat, g_ctx), 0)
    c1 = lambda t: jnp.where(t >= g_all + g_ctx, t - g_all - g_ctx, 0)
    blk = lambda bi, c: jnp.where(c < g_lat, bi * g_lat + c, b * g_lat + bi * g_ctx + (c - g_lat))
    xbc_col = (D_ATT + D_SSD) // D_XBC
    z_col = D_ATT // D_SSD
    const = lambda shape: pl.BlockSpec(shape, lambda bi, t: tuple(0 for _ in shape))
    return pl.pallas_call(
        functools.partial(_ssd_kernel, n_lat=n_lat, n_ctx=n_ctx),
        grid=(b, 2 * g_all),
        in_specs=[pl.BlockSpec((span, D_XBC), lambda bi, t: (blk(bi, c0(t)), xbc_col)),
                  pl.BlockSpec((SUBLANE, D_XBC),
                               lambda bi, t: (jnp.maximum(blk(bi, c0(t)) * halo_per_span - 1, 0), xbc_col)),
                  pl.BlockSpec((SUBLANE, D_XBC),
                               lambda bi, t: (jnp.minimum((blk(bi, c0(t)) + 1) * halo_per_span, n_halo - 1),
                                              xbc_col)),
                  pl.BlockSpec((span, LANE), lambda bi, t: (blk(bi, c0(t)), 0)),
                  pl.BlockSpec((span, D_SSD), lambda bi, t: (bi * g_lat + c1(t), z_col)),
                  const((CONV_W, D_XBC)), const((1, D_XBC)), const((1, LANE)), const((1, LANE)),
                  const((1, D_SSD)), const((1, D_SSD)), const((2 * LANE, 4 * D_SSD))],
        out_specs=pl.BlockSpec((span, D_SSD), lambda bi, t: (bi * g_lat + c1(t), 0)),
        out_shape=jax.ShapeDtypeStruct((b * s, D_SSD), BF16),
        scratch_shapes=[pltpu.VMEM((D_STATE, D_SSD), F32),
                        pltpu.VMEM((D_STATE, D_SSD), F32),
                        pltpu.VMEM((n_lat, D_STATE, D_SSD), BF16),
                        pltpu.VMEM((n_lat + n_ctx, CHUNK, D_SSD), F32),
                        pltpu.VMEM((n_lat + n_ctx, CHUNK, 2 * D_BC), BF16),
                        pltpu.VMEM((n_lat + n_ctx, CHUNK, 2 * LANE), BF16),
                        pltpu.VMEM((n_lat + n_ctx, CHUNK, LANE), F32),
                        pltpu.VMEM((span, D_SSD), F32),
                        pltpu.VMEM((D_XBC // LANE, span + 2 * SUBLANE, LANE), F32)],
        compiler_params=_params(("arbitrary", "arbitrary"), 60),
        name="bi_ssd",
    )(gzx, gzx, gzx, dtraw, gzx, conv_w, conv_b, dt_bias, a_log, d_skip_exp, ssd_norm_g,
      _expansion_matrix())


def _out_kernel(ha_ref, hs_ref, wa_ref, ws_ref, x_ref, gate_ref, ng_ref, o_ref):
    upd = (jnp.dot(ha_ref[...], wa_ref[...], preferred_element_type=F32)
           + jnp.dot(hs_ref[...], ws_ref[...], preferred_element_type=F32))
    y = x_ref[...] + gate_ref[0] * upd
    ms = jnp.mean(y * y, axis=-1, keepdims=True)
    o_ref[...] = (y * lax.rsqrt(ms + EPS) * ng_ref[...]).astype(o_ref.dtype)


def _out_proj(h_att, h_ssd, w_out_bf, x2d, gate, final_norm_g, s):
    m, d = x2d.shape
    tm = _row_tile(s, (512, 256, 128))
    tiles_per_batch = s // tm
    return pl.pallas_call(
        _out_kernel,
        grid=(m // tm,),
        in_specs=[pl.BlockSpec((tm, D_ATT), lambda i: (i, 0)),
                  pl.BlockSpec((tm, D_SSD), lambda i: (i, 0)),
                  pl.BlockSpec((D_ATT, d), lambda i: (0, 0)),
                  pl.BlockSpec((D_SSD, d), lambda i: (D_ATT // D_SSD, 0)),
                  pl.BlockSpec((tm, d), lambda i: (i, 0)),
                  pl.BlockSpec((1, 1, d), lambda i: (i // tiles_per_batch, 0, 0)),
                  pl.BlockSpec((1, d), lambda i: (0, 0))],
        out_specs=pl.BlockSpec((tm, d), lambda i: (i, 0)),
        out_shape=jax.ShapeDtypeStruct((m, d), x2d.dtype),
        compiler_params=_params(("parallel",), 48),
        name="out_proj",
    )(h_att, h_ssd, w_out_bf, w_out_bf, x2d, gate, final_norm_g)


def _rope_tables(s, pad_rows):
    pos = np.arange(s)
    row = (pos // GRID_W).astype(np.float32)
    colp = (pos % GRID_W).astype(np.float32)
    quarter = HEAD_DIM // 4
    freq = (1.0 / (np.float32(ROPE_THETA) ** (np.arange(quarter, dtype=np.float32) / np.float32(quarter)))
            ).astype(np.float32)
    lane = np.arange(LANE)
    dim = lane % HEAD_DIM
    use_col = (dim // (HEAD_DIM // 2)) == 1
    r = dim % (HEAD_DIM // 2)
    first = r < quarter
    ang = (np.where(use_col[None, :], colp[:, None], row[:, None]) * freq[r % quarter][None, :]).astype(np.float32)
    cos, sin = np.cos(ang), np.sin(ang)
    sa = np.where(first[None, :], -sin, 0.0)
    sb = np.where(first[None, :], 0.0, sin)
    ident = lambda v, fill: jnp.asarray(
        np.concatenate([v, np.full((pad_rows, LANE), fill)], axis=0).astype(np.float32))
    return ident(cos, 1.0), ident(sa, 0.0), ident(sb, 0.0)


def kernel(x, c, ctx, c_ctx, w_mod, b_mod, norm_g, w_in, conv_w, conv_b, a_log_f, a_log_b, dt_bias_f,
           dt_bias_b, d_skip, att_norm_g, ssd_norm_g, sink, w_out, final_norm_g):
    assert w_mod.shape[0] == 1, "single-layer operation"
    b, s, d = x.shape
    nc = ctx.shape[1]

    pad_rows = (-(b + 1)) % SUBLANE
    cc = jnp.concatenate([c, c_ctx[None, :], jnp.zeros((pad_rows, d), F32)], axis=0)
    mod = _modulation(cc, w_mod[0], b_mod)
    shift = mod[:b + 1, :d].reshape(b + 1, 1, d)
    scale = mod[:b + 1, d:2 * d].reshape(b + 1, 1, d)
    gate = mod[:b, 2 * d:].reshape(b, 1, d)

    w_t = jnp.swapaxes(w_in[0], 0, 1).astype(BF16)
    w_dt = jnp.pad(w_t[D_QKV + D_GZX:], ((0, LANE - 2 * SSD_HEADS), (0, 0)))
    w_out_bf = w_out[0].astype(BF16)

    cos_t, sa_t, sb_t = _rope_tables(s, NORM_TILE)
    qkv, xn, dtraw = _norm_qkv(x.reshape(b * s, d), ctx.reshape(b * nc, d), shift, scale, norm_g, w_dt, w_t,
                               cos_t, sa_t, sb_t, s)

    gzx = _matmul(xn, w_t, D_QKV, D_GZX, F32)

    h_att = _attention(sink[0], qkv, gzx, att_norm_g, b, s, nc)

    lane_pad = LANE - 2 * SSD_HEADS
    dt_bias = jnp.pad(jnp.concatenate([dt_bias_f[0], dt_bias_b[0]]), (0, lane_pad))[None, :]
    a_log = jnp.pad(jnp.concatenate([a_log_f[0], a_log_b[0]]), (0, lane_pad))[None, :]
    d_skip_exp = jnp.repeat(d_skip[0], SSD_HEAD_DIM)[None, :]
    h_ssd = _ssd(gzx, dtraw, conv_w[0], conv_b, dt_bias, a_log, d_skip_exp, ssd_norm_g, b, s, nc)

    out = _out_proj(h_att, h_ssd, w_out_bf, x.reshape(b * s, d), gate, final_norm_g[None, :], s)
    return out.reshape(b, s, d)
```

```python
import functools

import jax
import jax.numpy as jnp
import numpy as np
from jax import lax
from jax.experimental import pallas as pl
from jax.experimental.pallas import tpu as pltpu

F32 = jnp.float32
BF16 = jnp.bfloat16

EPS = 1e-6
GRID_W = 64
ROPE_THETA = 10000.0
ATT_HEADS = 16
ATT_KV_HEADS = 4
HEAD_DIM = 64
D_ATT = ATT_HEADS * HEAD_DIM
D_KV = ATT_KV_HEADS * HEAD_DIM
Q_PER_KV = ATT_HEADS // ATT_KV_HEADS
BLOCK = 128
SSD_HEADS = 16
SSD_HEAD_DIM = 64
D_SSD = SSD_HEADS * SSD_HEAD_DIM
SSD_GROUPS = 4
SSD_HPG = SSD_HEADS // SSD_GROUPS
D_STATE = 128
D_BC = SSD_GROUPS * D_STATE
D_XBC = D_SSD + 2 * D_BC
CONV_W = 3
CHUNK = 128
D_QKV = D_ATT + 2 * D_KV
D_GZX = D_ATT + D_SSD + D_XBC

LANE = 128
SUBLANE = 8
V7X_VMEM_BYTES = 64 * 1024 * 1024
MIB = 1024 * 1024

NEG = -1e30
LOG2E = 1.4426950408889634


def _silu(v):
    return v / (1.0 + jnp.exp(-v))


def _dot_nt(a, w):
    return lax.dot_general(a, w, (((1,), (1,)), ((), ())), preferred_element_type=F32)


def _params(semantics, vmem_mib):
    return pltpu.CompilerParams(dimension_semantics=semantics, vmem_limit_bytes=vmem_mib * MIB)


def _row_tile(m, candidates=(1024, 512, 256, 128)):
    for t in candidates:
        if m % t == 0:
            return t
    raise ValueError(f"row count {m} not tileable")


def _mod_kernel(c_ref, w_ref, b_ref, o_ref):
    a = _silu(c_ref[...]).astype(BF16)
    o_ref[...] = jnp.dot(a, w_ref[...].astype(BF16), preferred_element_type=F32) + b_ref[...]


def _modulation(cc, w_mod, b_mod):
    rows, d = cc.shape
    n = w_mod.shape[1]
    tn = 768
    assert n % tn == 0
    return pl.pallas_call(
        _mod_kernel,
        grid=(n // tn,),
        in_specs=[pl.BlockSpec((rows, d), lambda j: (0, 0)),
                  pl.BlockSpec((d, tn), lambda j: (0, j)),
                  pl.BlockSpec((1, tn), lambda j: (0, j))],
        out_specs=pl.BlockSpec((rows, tn), lambda j: (0, j)),
        out_shape=jax.ShapeDtypeStruct((rows, n), F32),
        compiler_params=_params(("parallel",), 32),
        name="modulation",
    )(cc, w_mod, b_mod)


def _norm_qkv_kernel(x_ref, ctx_ref, shift_ref, scale_ref, g_ref, wdt_ref, w_ref, cos_ref, sa_ref, sb_ref,
                     qkv_ref, xn_ref, dt_ref, ya_ref, yb_ref, *, n_lat_tiles, n_tiles, n_rope, n_q):
    i = pl.program_id(0)
    is_lat = jnp.minimum(i, n_tiles - 1) < n_lat_tiles

    @pl.when(i == 0)
    def _():
        yb_ref[...] = jnp.zeros_like(yb_ref)

    def step(y_prev_ref, y_next_ref):
        y = y_prev_ref[...]
        dt_ref[...] = _dot_nt(y, wdt_ref[...])
        acc = _dot_nt(y, w_ref[...])
        cos, sa, sb = cos_ref[...], sa_ref[...], sb_ref[...]
        quarter = HEAD_DIM // 4
        for jb in range(acc.shape[1] // LANE):
            blk = acc[:, jb * LANE:(jb + 1) * LANE]
            if jb < n_rope:
                blk = (blk * cos + pltpu.roll(blk, LANE - quarter, 1) * sa
                       + pltpu.roll(blk, quarter, 1) * sb)
            if jb < n_q:
                blk = blk * (HEAD_DIM ** -0.5 * LOG2E)
            qkv_ref[:, jb * LANE:(jb + 1) * LANE] = blk.astype(qkv_ref.dtype)

        v = jnp.where(is_lat, x_ref[...], ctx_ref[...])
        ms = jnp.mean(v * v, axis=-1, keepdims=True)
        gain = g_ref[...] * (1.0 + scale_ref[0])
        yn = (v * lax.rsqrt(ms + EPS) * gain + shift_ref[0]).astype(BF16)
        xn_ref[...] = yn
        y_next_ref[...] = yn

    @pl.when(i % 2 == 0)
    def _():
        step(yb_ref, ya_ref)

    @pl.when(i % 2 == 1)
    def _():
        step(ya_ref, yb_ref)


NORM_TILE = 512


def _norm_qkv(x2d, ctx2d, shift, scale, norm_g, w_dt, w_t, cos_t, sa_t, sb_t, s):
    m_lat, d = x2d.shape
    m_ctx = ctx2d.shape[0]
    b = m_lat // s
    tr = NORM_TILE
    assert s % tr == 0 and m_ctx % tr == 0
    tiles_per_seq = s // tr
    n_lat_tiles, n_ctx_tiles = m_lat // tr, m_ctx // tr
    n_tiles = n_lat_tiles + n_ctx_tiles
    cur = lambda i: jnp.minimum(i, n_tiles - 1)
    prev = lambda i: jnp.maximum(i - 1, 0)
    mod_idx = lambda i: (jnp.where(cur(i) < n_lat_tiles, cur(i) // tiles_per_seq, b), 0, 0)
    tab = pl.BlockSpec((tr, LANE), lambda i: (jnp.where(prev(i) < n_lat_tiles, prev(i) % tiles_per_seq,
                                                        tiles_per_seq), 0))
    out_prev = lambda width: pl.BlockSpec((tr, width), lambda i: (prev(i), 0))
    const = lambda shape: pl.BlockSpec(shape, lambda i: tuple(0 for _ in shape))
    return pl.pallas_call(
        functools.partial(_norm_qkv_kernel, n_lat_tiles=n_lat_tiles, n_tiles=n_tiles,
                          n_rope=(D_ATT + D_KV) // LANE, n_q=D_ATT // LANE),
        grid=(n_tiles + 1,),
        in_specs=[pl.BlockSpec((tr, d), lambda i: (jnp.minimum(cur(i), n_lat_tiles - 1), 0)),
                  pl.BlockSpec((tr, d), lambda i: (jnp.maximum(cur(i) - n_lat_tiles, 0), 0)),
                  pl.BlockSpec((1, 1, d), mod_idx),
                  pl.BlockSpec((1, 1, d), mod_idx),
                  const((1, d)), const((LANE, d)), const((D_QKV, d)),
                  tab, tab, tab],
        out_specs=[out_prev(D_QKV),
                   pl.BlockSpec((tr, d), lambda i: (cur(i), 0)),
                   out_prev(LANE)],
        out_shape=[jax.ShapeDtypeStruct((m_lat + m_ctx, D_QKV), BF16),
                   jax.ShapeDtypeStruct((m_lat + m_ctx, d), BF16),
                   jax.ShapeDtypeStruct((m_lat + m_ctx, LANE), F32)],
        scratch_shapes=[pltpu.VMEM((tr, d), BF16), pltpu.VMEM((tr, d), BF16)],
        compiler_params=_params(("arbitrary",), 56),
        name="norm_qkv",
    )(x2d, ctx2d, shift, scale, norm_g, w_dt, w_t, cos_t, sa_t, sb_t)


def _mm_kernel(x_ref, w_ref, o_ref):
    o_ref[...] = _dot_nt(x_ref[...], w_ref[...]).astype(o_ref.dtype)


def _matmul(x2d, w_t, col0, n, out_dtype, tn=2048):
    m, d = x2d.shape
    tm = _row_tile(m)
    assert n % tn == 0 and col0 % LANE == 0
    return pl.pallas_call(
        _mm_kernel,
        grid=(n // tn, m // tm),
        in_specs=[pl.BlockSpec((tm, d), lambda j, i: (i, 0)),
                  pl.BlockSpec((pl.Element(tn), pl.Element(d)),
                               lambda j, i: (pl.multiple_of(col0 + j * tn, LANE), 0))],
        out_specs=pl.BlockSpec((tm, tn), lambda j, i: (i, j)),
        out_shape=jax.ShapeDtypeStruct((m, n), out_dtype),
        compiler_params=_params(("parallel", "parallel"), 56),
        name="gzx_proj",
    )(x2d, w_t)


ATT_QBLOCKS = 4


def _attn_kernel(sink_ref, q_ref, kp_ref, ko_ref, kn_ref, kc_ref, vp_ref, vo_ref, vn_ref, vc_ref,
                 g_ref, ng_ref, o_ref, *, n_steps):
    n = pl.program_id(1)
    nctx = kc_ref.shape[0]
    nk = 3 * BLOCK + nctx
    cols_g = Q_PER_KV * BLOCK
    k_blocks = [kp_ref[...]] + [ko_ref[u * BLOCK:(u + 1) * BLOCK] for u in range(ATT_QBLOCKS)] + [kn_ref[...]]
    v_blocks = [vp_ref[...]] + [vo_ref[u * BLOCK:(u + 1) * BLOCK] for u in range(ATT_QBLOCKS)] + [vn_ref[...]]

    kj = lax.broadcasted_iota(jnp.int32, (BLOCK, BLOCK), 0)
    qi = lax.broadcasted_iota(jnp.int32, (BLOCK, BLOCK), 1)
    tile = lambda v: jnp.concatenate([v] * Q_PER_KV, axis=1)
    tri_prev = tile(jnp.where(kj >= qi, 0.0, NEG).astype(F32))
    tri_next = tile(jnp.where(kj <= qi, 0.0, NEG).astype(F32))
    edge_prev = tile(jnp.where((kj >= qi) & (n > 0), 0.0, NEG).astype(F32))
    edge_next = tile(jnp.where((kj <= qi) & (n < n_steps - 1), 0.0, NEG).astype(F32))
    head_of_col = lax.broadcasted_iota(jnp.int32, (1, cols_g), 1) // BLOCK
    vlane = lax.broadcasted_iota(jnp.int32, (nk, LANE - HEAD_DIM), 1)
    v_tail = jnp.where(vlane == 0, 1.0, 0.0).astype(BF16)

    def scores_t(u, kh):
        hs = slice(kh * HEAD_DIM, (kh + 1) * HEAD_DIM)
        k_h = jnp.concatenate([blk[:, hs] for blk in k_blocks[u:u + 3]] + [kc_ref[:, hs]], axis=0)
        qg = jnp.concatenate(
            [q_ref[u * BLOCK:(u + 1) * BLOCK, (kh * Q_PER_KV + i) * HEAD_DIM:(kh * Q_PER_KV + i + 1) * HEAD_DIM]
             for i in range(Q_PER_KV)], axis=0)
        return lax.dot_general(k_h, qg, (((1,), (1,)), ((), ())), preferred_element_type=F32)

    def attend(u, kh, st):
        bias_prev = edge_prev if u == 0 else tri_prev
        bias_next = edge_next if u == ATT_QBLOCKS - 1 else tri_next
        st = jnp.concatenate([st[:BLOCK] + bias_prev, st[BLOCK:2 * BLOCK],
                              st[2 * BLOCK:3 * BLOCK] + bias_next, st[3 * BLOCK:]], axis=0)
        sink_row = jnp.zeros((1, cols_g), F32)
        for i in range(Q_PER_KV):
            sink_row = jnp.where(head_of_col == i, sink_ref[kh * Q_PER_KV + i] * LOG2E, sink_row)
        m = jnp.maximum(jnp.max(st, axis=0, keepdims=True), sink_row)
        pt = jnp.exp2(st - m).astype(BF16)
        hs = slice(kh * HEAD_DIM, (kh + 1) * HEAD_DIM)
        v_h = jnp.concatenate([blk[:, hs] for blk in v_blocks[u:u + 3]] + [vc_ref[:, hs]], axis=0)
        v_ext = jnp.concatenate([v_h, v_tail], axis=1)
        ot = lax.dot_general(v_ext, pt, (((0,), (0,)), ((), ())), preferred_element_type=F32)
        inv = 1.0 / (ot[HEAD_DIM:HEAD_DIM + 1] + jnp.exp2(sink_row - m))
        on = ot[:HEAD_DIM] * inv
        pairs = [jnp.concatenate([on[:, (2 * j) * BLOCK:(2 * j + 1) * BLOCK],
                                  on[:, (2 * j + 1) * BLOCK:(2 * j + 2) * BLOCK]], axis=0).T
                 for j in range(Q_PER_KV // 2)]
        return jnp.concatenate(pairs, axis=1)

    def finish(u, groups):
        rows = slice(u * BLOCK, (u + 1) * BLOCK)
        a = jnp.concatenate(groups, axis=1) * _silu(g_ref[rows])
        ms = jnp.mean(a * a, axis=-1, keepdims=True)
        o_ref[rows] = (a * lax.rsqrt(ms + EPS) * ng_ref[...]).astype(o_ref.dtype)

    order = [(u, kh) for u in range(ATT_QBLOCKS) for kh in range(ATT_KV_HEADS)]
    ahead = 1
    pending = [scores_t(*order[i]) for i in range(ahead)]
    groups = []
    for idx, (u, kh) in enumerate(order):
        s_cur = pending.pop(0)
        if idx + ahead < len(order):
            pending.append(scores_t(*order[idx + ahead]))
        groups.append(attend(u, kh, s_cur))
        if kh == ATT_KV_HEADS - 1:
            finish(u, groups)
            groups = []


def _attention(sink, qkv, gzx, att_norm_g, b, s, nc):
    rows = ATT_QBLOCKS * BLOCK
    assert s % rows == 0 and (b * s) % nc == 0
    nb = s // BLOCK
    n_steps = s // rows
    kcol, vcol = D_ATT // D_KV, D_ATT // D_KV + 1
    before = lambda col: pl.BlockSpec(
        (BLOCK, D_KV), lambda bi, n: (bi * nb + jnp.maximum(ATT_QBLOCKS * n - 1, 0), col))
    own = lambda col: pl.BlockSpec((rows, D_KV), lambda bi, n: (bi * n_steps + n, col))
    after = lambda col: pl.BlockSpec(
        (BLOCK, D_KV), lambda bi, n: (bi * nb + jnp.minimum(ATT_QBLOCKS * (n + 1), nb - 1), col))
    ctx_spec = lambda col: pl.BlockSpec((nc, D_KV), lambda bi, n: (b * s // nc + bi, col))
    q_rows = pl.BlockSpec((rows, D_ATT), lambda bi, n: (bi * n_steps + n, 0))
    return pl.pallas_call(
        functools.partial(_attn_kernel, n_steps=n_steps),
        grid=(b, n_steps),
        in_specs=[pl.BlockSpec(memory_space=pltpu.SMEM),
                  q_rows,
                  before(kcol), own(kcol), after(kcol), ctx_spec(kcol),
                  before(vcol), own(vcol), after(vcol), ctx_spec(vcol),
                  q_rows,
                  pl.BlockSpec((1, D_ATT), lambda bi, n: (0, 0))],
        out_specs=q_rows,
        out_shape=jax.ShapeDtypeStruct((b * s, D_ATT), BF16),
        compiler_params=_params(("parallel", "parallel"), 40),
        name="window_attn",
    )(sink, qkv, qkv, qkv, qkv, qkv, qkv, qkv, qkv, qkv, gzx, att_norm_g)


SSD_CPS = 2


def _ssd_visit_index(t, n_lat, n_ctx):
    n_all = n_lat + n_ctx
    k = t - n_all
    fwd = jnp.where(k < n_ctx, n_lat + k, k - n_ctx)
    return jnp.where(t >= n_all, fwd, n_all - 1 - t)


def _ssd_kernel(xbc_ref, hp_ref, hn_ref, dt_ref, z_ref, cw_ref, cb_ref, dtb_ref, alog_ref, dsk_ref,
                ng_ref, ex_ref, o_ref, sf_ref, sb_ref, sin_ref, ux_ref, ubc_ref, lhs_ref, pack_ref, y_ref, pad_ref,
                *, n_lat, n_ctx):
    t = pl.program_id(1)
    n_all = n_lat + n_ctx
    g_lat, g_ctx = n_lat // SSD_CPS, n_ctx // SSD_CPS
    g_all = g_lat + g_ctx
    gidx = _ssd_visit_index(t, g_lat, g_ctx)
    c_base = gidx * SSD_CPS
    sweep1 = t >= g_all
    is_lat = gidx < g_lat
    L = CHUNK
    gw = SSD_HPG * SSD_HEAD_DIM
    lane = lax.broadcasted_iota(jnp.int32, (1, LANE), 1)
    rows = lax.broadcasted_iota(jnp.int32, (L, 1), 0)
    groups = range(SSD_GROUPS)
    gsl = [slice(g * gw, (g + 1) * gw) for g in groups]

    @pl.when(t == 0)
    def _():
        sf_ref[...] = jnp.zeros_like(sf_ref)
        sb_ref[...] = jnp.zeros_like(sb_ref)
        ux_ref[n_all - 1] = jnp.zeros((CHUNK, D_SSD), F32)
        ubc_ref[n_all - 1] = jnp.zeros((CHUNK, 2 * D_BC), BF16)
        lhs_ref[n_all - 1] = jnp.zeros((CHUNK, 2 * LANE), BF16)

    def expand(lhs, block, g):
        c0 = block * D_SSD + g * gw
        return jnp.dot(lhs, ex_ref[:, c0:c0 + gw], preferred_element_type=F32)

    def update_state(state_ref, chunk, lhs_ex, g, wblock, eblock, end_row):
        r8 = (end_row // SUBLANE) * SUBLANE
        xw = (ux_ref[chunk, :, gsl[g]] * expand(lhs_ex, wblock, g)).astype(BF16)
        bg = ubc_ref[chunk, :, g * D_STATE:(g + 1) * D_STATE]
        loc = lax.dot_general(bg, xw, (((0,), (0,)), ((), ())), preferred_element_type=F32)
        tot = expand(lhs_ex[r8:r8 + SUBLANE], eblock, g)[end_row - r8:end_row - r8 + 1]
        new = state_ref[:, gsl[g]] * tot + loc
        state_ref[:, gsl[g]] = new
        return new

    @pl.when(jnp.logical_not(sweep1))
    def _():
        is_first = (gidx == 0) | (gidx == g_lat)
        is_last = (gidx == g_lat - 1) | (gidx == g_all - 1)
        span = SSD_CPS * L

        for k in range(D_XBC // LANE):
            lanes = slice(k * LANE, (k + 1) * LANE)
            pad_ref[k, SUBLANE:SUBLANE + span, :] = xbc_ref[:, lanes]
            pad_ref[k, SUBLANE - 1:SUBLANE, :] = jnp.where(is_first, 0.0, hp_ref[SUBLANE - 1:SUBLANE, lanes])
            pad_ref[k, SUBLANE + span:SUBLANE + span + 1, :] = jnp.where(is_last, 0.0, hn_ref[0:1, lanes])

        def conv_strip(u, j):
            r0 = SUBLANE + u * L
            for k in range(j * gw // LANE, (j + 1) * gw // LANE):
                lanes = slice(k * LANE, (k + 1) * LANE)
                v = (pad_ref[k, r0 - 1:r0 - 1 + L, :] * cw_ref[0:1, lanes] + pad_ref[k, r0:r0 + L, :] * cw_ref[1:2, lanes]
                     + pad_ref[k, r0 + 1:r0 + 1 + L, :] * cw_ref[2:3, lanes] + cb_ref[:, lanes])
                v = _silu(v)
                if k * LANE < D_SSD:
                    ux_ref[c_base + u, :, lanes] = v
                else:
                    ubc_ref[c_base + u, :, k * LANE - D_SSD:(k + 1) * LANE - D_SSD] = v.astype(BF16)

        def decay_tables(u):
            dtr = dt_ref[u * L:(u + 1) * L, :] + dtb_ref[...]
            dt = jnp.maximum(dtr, 0.0) + jnp.log(1.0 + jnp.exp(-jnp.abs(dtr)))
            a = dt * (-jnp.exp(alog_ref[...]))
            pre, suf = a, a
            d = 1
            while d < L:
                pre = pre + jnp.where(rows >= d, pltpu.roll(pre, d, 0), 0.0)
                suf = suf + jnp.where(rows < L - d, pltpu.roll(suf, L - d, 0), 0.0)
                d *= 2
            is_fwd_lane = lane < SSD_HEADS
            cs = jnp.where(is_fwd_lane, pre, suf)
            tot = jnp.where(is_fwd_lane, cs[L - 1:L, :], cs[0:1, :])
            e = jnp.exp(cs)
            wdt = jnp.exp(tot - cs) * dt
            pack_ref[c_base + u] = jnp.where(lane < 2 * SSD_HEADS, cs * LOG2E, pltpu.roll(dt, 2 * SSD_HEADS, 1))
            fac = jnp.where(lane < 2 * SSD_HEADS, wdt,
                            jnp.where(lane < 4 * SSD_HEADS, pltpu.roll(e, 2 * SSD_HEADS, 1), 0.0))
            hi = fac.astype(BF16).astype(F32)
            r1 = fac - hi
            mid = r1.astype(BF16).astype(F32)
            lo = (r1 - mid).astype(BF16)
            lhs_ref[c_base + u] = jnp.concatenate([(hi + pltpu.roll(mid, LANE // 2, 1)).astype(BF16), lo], axis=1)

        for u in reversed(range(SSD_CPS)):
            cidx = c_base + u
            done = jnp.minimum(cidx + 1, n_all - 1)
            lhs_done = lhs_ref[done]
            wexp = [expand(lhs_done, 1, g) for g in groups]
            tots = [expand(lhs_done[0:SUBLANE], 3, g)[0:1] for g in groups]
            conv_strip(u, 0)
            conv_strip(u, 1)
            xws = [(ux_ref[done, :, gsl[g]] * wexp[g]).astype(BF16) for g in groups]
            locs = [lax.dot_general(ubc_ref[done, :, g * D_STATE:(g + 1) * D_STATE], xws[g],
                                    (((0,), (0,)), ((), ())), preferred_element_type=F32) for g in groups]
            conv_strip(u, 2)
            conv_strip(u, 3)
            conv_strip(u, 4)
            for g in groups:
                new = sb_ref[:, gsl[g]] * tots[g] + locs[g]
                sb_ref[:, gsl[g]] = new
                sin_ref[jnp.minimum(cidx, n_lat - 1), :, gsl[g]] = new.astype(BF16)
            conv_strip(u, 5)
            conv_strip(u, 6)
            conv_strip(u, 7)
            decay_tables(u)

    @pl.when(sweep1 & jnp.logical_not(is_lat))
    def _():
        for u in range(SSD_CPS):
            lhs_ex = lhs_ref[c_base + u]
            for g in groups:
                update_state(sf_ref, c_base + u, lhs_ex, g, 0, 2, L - 1)

    @pl.when(sweep1 & is_lat)
    def _():
        li = lax.broadcasted_iota(jnp.int32, (L, L), 0)
        si = lax.broadcasted_iota(jnp.int32, (L, L), 1)
        masks = [si <= li, si >= li]
        head_of_lane = lax.broadcasted_iota(jnp.int32, (1, gw), 1) // SSD_HEAD_DIM
        lhs_exs = [lhs_ref[c_base + u] for u in range(SSD_CPS)]
        packs = [pack_ref[c_base + u] for u in range(SSD_CPS)]
        pack_ts = [p.T for p in packs]

        def early(u, g):
            cidx = c_base + u
            bg = ubc_ref[cidx, :, g * D_STATE:(g + 1) * D_STATE]
            cg = ubc_ref[cidx, :, D_BC + g * D_STATE:D_BC + (g + 1) * D_STATE]
            cb = lax.dot_general(cg, bg, (((1,), (1,)), ((), ())), preferred_element_type=F32)
            cbm = [jnp.where(m, cb, 0.0) for m in masks]
            y_off = (jnp.dot(cg, sf_ref[:, gsl[g]].astype(BF16), preferred_element_type=F32)
                     * expand(lhs_exs[u], 2, g)
                     + jnp.dot(cg, sin_ref[cidx, :, gsl[g]], preferred_element_type=F32)
                     * expand(lhs_exs[u], 3, g))
            update_state(sf_ref, cidx, lhs_exs[u], g, 0, 2, L - 1)
            return cbm, y_off

        def main(u, g, cbm):
            xg = ux_ref[c_base + u, :, gsl[g]].astype(BF16)
            blockdiag = jnp.concatenate(
                [jnp.where(head_of_lane == hh, xg, jnp.zeros_like(xg)) for hh in range(SSD_HPG)], axis=0)
            y_diag = None
            for dirn in range(2):
                ws = []
                for hh in range(SSD_HPG):
                    ln = dirn * SSD_HEADS + g * SSD_HPG + hh
                    col = packs[u][:, ln:ln + 1]
                    row = pack_ts[u][ln:ln + 1, :]
                    dtrow = pack_ts[u][2 * SSD_HEADS + ln:2 * SSD_HEADS + ln + 1, :]
                    dec = jnp.exp2(jnp.minimum(col - row, 0.0))
                    ws.append((cbm[dirn] * dec * dtrow).astype(BF16))
                yd = jnp.dot(jnp.concatenate(ws, axis=1), blockdiag, preferred_element_type=F32)
                y_diag = yd if y_diag is None else y_diag + yd
            return y_diag

        def finish(u, g, y_off, y_diag):
            yg = ux_ref[c_base + u, :, gsl[g]] * dsk_ref[:, gsl[g]] + y_diag + y_off
            yz = yg * _silu(z_ref[u * L:(u + 1) * L, gsl[g]])
            y_ref[u * L:(u + 1) * L, gsl[g]] = yz
            return jnp.sum(yz * yz, axis=-1, keepdims=True)

        units = [(u, g) for u in range(SSD_CPS) for g in groups]
        pending = {units[0]: early(*units[0])}
        diag = {}
        ss = [jnp.zeros((L, 1), F32) for _ in range(SSD_CPS)]
        for i, (u, g) in enumerate(units):
            if i + 1 < len(units):
                pending[units[i + 1]] = early(*units[i + 1])
            diag[(u, g)] = main(u, g, pending[(u, g)][0])
            if i >= 1:
                pu, pg = units[i - 1]
                ss[pu] = ss[pu] + finish(pu, pg, pending[(pu, pg)][1], diag[(pu, pg)])
        pu, pg = units[-1]
        ss[pu] = ss[pu] + finish(pu, pg, pending[(pu, pg)][1], diag[(pu, pg)])
        for u in range(SSD_CPS):
            rs = lax.rsqrt(ss[u] * (1.0 / D_SSD) + EPS)
            for g in groups:
                o_ref[u * L:(u + 1) * L, gsl[g]] = (
                    y_ref[u * L:(u + 1) * L, gsl[g]] * rs * ng_ref[:, gsl[g]]).astype(o_ref.dtype)


def _expansion_matrix():
    r = jnp.arange(2 * LANE)[:, None]
    col = jnp.arange(4 * D_SSD)[None, :]
    src = (col // D_SSD) * SSD_HEADS + (col % D_SSD) // SSD_HEAD_DIM
    half = LANE // 2
    return ((r % half == src) & (r < 3 * half)).astype(BF16)


def _ssd(gzx, dtraw, conv_w, conv_b, dt_bias, a_log, d_skip_exp, ssd_norm_g, b, s, nc):
    t_rows = gzx.shape[0]
    span = SSD_CPS * CHUNK
    assert s % span == 0 and nc % span == 0
    n_lat, n_ctx = s // CHUNK, nc // CHUNK
    g_lat, g_ctx = s // span, nc // span
    g_all = g_lat + g_ctx
    halo_per_span = span // SUBLANE
    n_halo = t_rows // SUBLANE
    c0 = lambda t: jnp.where(t < g_all, _ssd_visit_index(t, g_lat, g_ctx), 0)
    c1 = lambda t: jnp.where(t >= g_all + g_ctx, t - g_all - g_ctx, 0)
    blk = lambda bi, c: jnp.where(c < g_lat, bi * g_lat + c, b * g_lat + bi * g_ctx + (c - g_lat))
    xbc_col = (D_ATT + D_SSD) // D_XBC
    z_col = D_ATT // D_SSD
    const = lambda shape: pl.BlockSpec(shape, lambda bi, t: tuple(0 for _ in shape))
    return pl.pallas_call(
        functools.partial(_ssd_kernel, n_lat=n_lat, n_ctx=n_ctx),
        grid=(b, 2 * g_all),
        in_specs=[pl.BlockSpec((span, D_XBC), lambda bi, t: (blk(bi, c0(t)), xbc_col)),
                  pl.BlockSpec((SUBLANE, D_XBC),
                               lambda bi, t: (jnp.maximum(blk(bi, c0(t)) * halo_per_span - 1, 0), xbc_col)),
                  pl.BlockSpec((SUBLANE, D_XBC),
                               lambda bi, t: (jnp.minimum((blk(bi, c0(t)) + 1) * halo_per_span, n_halo - 1),
                                              xbc_col)),
                  pl.BlockSpec((span, LANE), lambda bi, t: (blk(bi, c0(t)), 0)),
                  pl.BlockSpec((span, D_SSD), lambda bi, t: (bi * g_lat + c1(t), z_col)),
                  const((CONV_W, D_XBC)), const((1, D_XBC)), const((1, LANE)), const((1, LANE)),
                  const((1, D_SSD)), const((1, D_SSD)), const((2 * LANE, 4 * D_SSD))],
        out_specs=pl.BlockSpec((span, D_SSD), lambda bi, t: (bi * g_lat + c1(t), 0)),
        out_shape=jax.ShapeDtypeStruct((b * s, D_SSD), BF16),
        scratch_shapes=[pltpu.VMEM((D_STATE, D_SSD), F32),
                        pltpu.VMEM((D_STATE, D_SSD), F32),
                        pltpu.VMEM((n_lat, D_STATE, D_SSD), BF16),
                        pltpu.VMEM((n_lat + n_ctx, CHUNK, D_SSD), F32),
                        pltpu.VMEM((n_lat + n_ctx, CHUNK, 2 * D_BC), BF16),
                        pltpu.VMEM((n_lat + n_ctx, CHUNK, 2 * LANE), BF16),
                        pltpu.VMEM((n_lat + n_ctx, CHUNK, LANE), F32),
                        pltpu.VMEM((span, D_SSD), F32),
                        pltpu.VMEM((D_XBC // LANE, span + 2 * SUBLANE, LANE), F32)],
        compiler_params=_params(("arbitrary", "arbitrary"), 60),
        name="bi_ssd",
    )(gzx, gzx, gzx, dtraw, gzx, conv_w, conv_b, dt_bias, a_log, d_skip_exp, ssd_norm_g,
      _expansion_matrix())


def _out_kernel(ha_ref, hs_ref, wa_ref, ws_ref, x_ref, gate_ref, ng_ref, o_ref):
    upd = (jnp.dot(ha_ref[...], wa_ref[...], preferred_element_type=F32)
           + jnp.dot(hs_ref[...], ws_ref[...], preferred_element_type=F32))
    y = x_ref[...] + gate_ref[0] * upd
    ms = jnp.mean(y * y, axis=-1, keepdims=True)
    o_ref[...] = (y * lax.rsqrt(ms + EPS) * ng_ref[...]).astype(o_ref.dtype)


def _out_proj(h_att, h_ssd, w_out_bf, x2d, gate, final_norm_g, s):
    m, d = x2d.shape
    tm = _row_tile(s, (512, 256, 128))
    tiles_per_batch = s // tm
    return pl.pallas_call(
        _out_kernel,
        grid=(m // tm,),
        in_specs=[pl.BlockSpec((tm, D_ATT), lambda i: (i, 0)),
                  pl.BlockSpec((tm, D_SSD), lambda i: (i, 0)),
                  pl.BlockSpec((D_ATT, d), lambda i: (0, 0)),
                  pl.BlockSpec((D_SSD, d), lambda i: (D_ATT // D_SSD, 0)),
                  pl.BlockSpec((tm, d), lambda i: (i, 0)),
                  pl.BlockSpec((1, 1, d), lambda i: (i // tiles_per_batch, 0, 0)),
                  pl.BlockSpec((1, d), lambda i: (0, 0))],
        out_specs=pl.BlockSpec((tm, d), lambda i: (i, 0)),
        out_shape=jax.ShapeDtypeStruct((m, d), x2d.dtype),
        compiler_params=_params(("parallel",), 48),
        name="out_proj",
    )(h_att, h_ssd, w_out_bf, w_out_bf, x2d, gate, final_norm_g)


def _rope_tables(s, pad_rows):
    pos = np.arange(s)
    row = (pos // GRID_W).astype(np.float32)
    colp = (pos % GRID_W).astype(np.float32)
    quarter = HEAD_DIM // 4
    freq = (1.0 / (np.float32(ROPE_THETA) ** (np.arange(quarter, dtype=np.float32) / np.float32(quarter)))
            ).astype(np.float32)
    lane = np.arange(LANE)
    dim = lane % HEAD_DIM
    use_col = (dim // (HEAD_DIM // 2)) == 1
    r = dim % (HEAD_DIM // 2)
    first = r < quarter
    ang = (np.where(use_col[None, :], colp[:, None], row[:, None]) * freq[r % quarter][None, :]).astype(np.float32)
    cos, sin = np.cos(ang), np.sin(ang)
    sa = np.where(first[None, :], -sin, 0.0)
    sb = np.where(first[None, :], 0.0, sin)
    ident = lambda v, fill: jnp.asarray(
        np.concatenate([v, np.full((pad_rows, LANE), fill)], axis=0).astype(np.float32))
    return ident(cos, 1.0), ident(sa, 0.0), ident(sb, 0.0)


def kernel(x, c, ctx, c_ctx, w_mod, b_mod, norm_g, w_in, conv_w, conv_b, a_log_f, a_log_b, dt_bias_f,
           dt_bias_b, d_skip, att_norm_g, ssd_norm_g, sink, w_out, final_norm_g):
    assert w_mod.shape[0] == 1, "single-layer operation"
    b, s, d = x.shape
    nc = ctx.shape[1]

    pad_rows = (-(b + 1)) % SUBLANE
    cc = jnp.concatenate([c, c_ctx[None, :], jnp.zeros((pad_rows, d), F32)], axis=0)
    mod = _modulation(cc, w_mod[0], b_mod)
    shift = mod[:b + 1, :d].reshape(b + 1, 1, d)
    scale = mod[:b + 1, d:2 * d].reshape(b + 1, 1, d)
    gate = mod[:b, 2 * d:].reshape(b, 1, d)

    w_t = jnp.swapaxes(w_in[0], 0, 1).astype(BF16)
    w_dt = jnp.pad(w_t[D_QKV + D_GZX:], ((0, LANE - 2 * SSD_HEADS), (0, 0)))
    w_out_bf = w_out[0].astype(BF16)

    cos_t, sa_t, sb_t = _rope_tables(s, NORM_TILE)
    qkv, xn, dtraw = _norm_qkv(x.reshape(b * s, d), ctx.reshape(b * nc, d), shift, scale, norm_g, w_dt, w_t,
                               cos_t, sa_t, sb_t, s)

    gzx = _matmul(xn, w_t, D_QKV, D_GZX, F32)

    h_att = _attention(sink[0], qkv, gzx, att_norm_g, b, s, nc)

    lane_pad = LANE - 2 * SSD_HEADS
    dt_bias = jnp.pad(jnp.concatenate([dt_bias_f[0], dt_bias_b[0]]), (0, lane_pad))[None, :]
    a_log = jnp.pad(jnp.concatenate([a_log_f[0], a_log_b[0]]), (0, lane_pad))[None, :]
    d_skip_exp = jnp.repeat(d_skip[0], SSD_HEAD_DIM)[None, :]
    h_ssd = _ssd(gzx, dtraw, conv_w[0], conv_b, dt_bias, a_log, d_skip_exp, ssd_norm_g, b, s, nc)

    out = _out_proj(h_att, h_ssd, w_out_bf, x.reshape(b * s, d), gate, final_norm_g[None, :], s)
    return out.reshape(b, s, d)
```

```python
import functools

import jax
import jax.numpy as jnp
import numpy as np
from jax import lax
from jax.experimental import pallas as pl
from jax.experimental.pallas import tpu as pltpu

F32 = jnp.float32
BF16 = jnp.bfloat16

EPS = 1e-6
GRID_W = 64
ROPE_THETA = 10000.0
ATT_HEADS = 16
ATT_KV_HEADS = 4
HEAD_DIM = 64
D_ATT = ATT_HEADS * HEAD_DIM
D_KV = ATT_KV_HEADS * HEAD_DIM
Q_PER_KV = ATT_HEADS // ATT_KV_HEADS
BLOCK = 128
SSD_HEADS = 16
SSD_HEAD_DIM = 64
D_SSD = SSD_HEADS * SSD_HEAD_DIM
SSD_GROUPS = 4
SSD_HPG = SSD_HEADS // SSD_GROUPS
D_STATE = 128
D_BC = SSD_GROUPS * D_STATE
D_XBC = D_SSD + 2 * D_BC
CONV_W = 3
CHUNK = 128
D_QKV = D_ATT + 2 * D_KV
D_GZX = D_ATT + D_SSD + D_XBC

LANE = 128
SUBLANE = 8
V7X_VMEM_BYTES = 64 * 1024 * 1024
MIB = 1024 * 1024

NEG = -1e30
LOG2E = 1.4426950408889634


def _silu(v):
    return v / (1.0 + jnp.exp(-v))


def _dot_nt(a, w):
    return lax.dot_general(a, w, (((1,), (1,)), ((), ())), preferred_element_type=F32)


def _params(semantics, vmem_mib):
    return pltpu.CompilerParams(dimension_semantics=semantics, vmem_limit_bytes=vmem_mib * MIB)


def _row_tile(m, candidates=(1024, 512, 256, 128)):
    for t in candidates:
        if m % t == 0:
            return t
    raise ValueError(f"row count {m} not tileable")


def _mod_kernel(c_ref, w_ref, b_ref, o_ref):
    a = _silu(c_ref[...]).astype(BF16)
    o_ref[...] = jnp.dot(a, w_ref[...].astype(BF16), preferred_element_type=F32) + b_ref[...]


def _modulation(cc, w_mod, b_mod):
    rows, d = cc.shape
    n = w_mod.shape[1]
    tn = 768
    assert n % tn == 0
    return pl.pallas_call(
        _mod_kernel,
        grid=(n // tn,),
        in_specs=[pl.BlockSpec((rows, d), lambda j: (0, 0)),
                  pl.BlockSpec((d, tn), lambda j: (0, j)),
                  pl.BlockSpec((1, tn), lambda j: (0, j))],
        out_specs=pl.BlockSpec((rows, tn), lambda j: (0, j)),
        out_shape=jax.ShapeDtypeStruct((rows, n), F32),
        compiler_params=_params(("parallel",), 32),
        name="modulation",
    )(cc, w_mod, b_mod)


def _norm_qkv_kernel(x_ref, ctx_ref, shift_ref, scale_ref, g_ref, wdt_ref, w_ref, cos_ref, sa_ref, sb_ref,
                     qkv_ref, xn_ref, dt_ref, ya_ref, yb_ref, *, n_lat_tiles, n_tiles, n_rope, n_q):
    i = pl.program_id(0)
    is_lat = jnp.minimum(i, n_tiles - 1) < n_lat_tiles

    @pl.when(i == 0)
    def _():
        yb_ref[...] = jnp.zeros_like(yb_ref)

    def step(y_prev_ref, y_next_ref):
        y = y_prev_ref[...]
        dt_ref[...] = _dot_nt(y, wdt_ref[...])
        acc = _dot_nt(y, w_ref[...])
        cos, sa, sb = cos_ref[...], sa_ref[...], sb_ref[...]
        quarter = HEAD_DIM // 4
        for jb in range(acc.shape[1] // LANE):
            blk = acc[:, jb * LANE:(jb + 1) * LANE]
            if jb < n_rope:
                blk = (blk * cos + pltpu.roll(blk, LANE - quarter, 1) * sa
                       + pltpu.roll(blk, quarter, 1) * sb)
            if jb < n_q:
                blk = blk * (HEAD_DIM ** -0.5 * LOG2E)
            qkv_ref[:, jb * LANE:(jb + 1) * LANE] = blk.astype(qkv_ref.dtype)

        v = jnp.where(is_lat, x_ref[...], ctx_ref[...])
        ms = jnp.mean(v * v, axis=-1, keepdims=True)
        gain = g_ref[...] * (1.0 + scale_ref[0])
        yn = (v * lax.rsqrt(ms + EPS) * gain + shift_ref[0]).astype(BF16)
        xn_ref[...] = yn
        y_next_ref[...] = yn

    @pl.when(i % 2 == 0)
    def _():
        step(yb_ref, ya_ref)

    @pl.when(i % 2 == 1)
    def _():
        step(ya_ref, yb_ref)


NORM_TILE = 512


def _norm_qkv(x2d, ctx2d, shift, scale, norm_g, w_dt, w_t, cos_t, sa_t, sb_t, s):
    m_lat, d = x2d.shape
    m_ctx = ctx2d.shape[0]
    b = m_lat // s
    tr = NORM_TILE
    assert s % tr == 0 and m_ctx % tr == 0
    tiles_per_seq = s // tr
    n_lat_tiles, n_ctx_tiles = m_lat // tr, m_ctx // tr
    n_tiles = n_lat_tiles + n_ctx_tiles
    cur = lambda i: jnp.minimum(i, n_tiles - 1)
    prev = lambda i: jnp.maximum(i - 1, 0)
    mod_idx = lambda i: (jnp.where(cur(i) < n_lat_tiles, cur(i) // tiles_per_seq, b), 0, 0)
    tab = pl.BlockSpec((tr, LANE), lambda i: (jnp.where(prev(i) < n_lat_tiles, prev(i) % tiles_per_seq,
                                                        tiles_per_seq), 0))
    out_prev = lambda width: pl.BlockSpec((tr, width), lambda i: (prev(i), 0))
    const = lambda shape: pl.BlockSpec(shape, lambda i: tuple(0 for _ in shape))
    return pl.pallas_call(
        functools.partial(_norm_qkv_kernel, n_lat_tiles=n_lat_tiles, n_tiles=n_tiles,
                          n_rope=(D_ATT + D_KV) // LANE, n_q=D_ATT // LANE),
        grid=(n_tiles + 1,),
        in_specs=[pl.BlockSpec((tr, d), lambda i: (jnp.minimum(cur(i), n_lat_tiles - 1), 0)),
                  pl.BlockSpec((tr, d), lambda i: (jnp.maximum(cur(i) - n_lat_tiles, 0), 0)),
                  pl.BlockSpec((1, 1, d), mod_idx),
                  pl.BlockSpec((1, 1, d), mod_idx),
                  const((1, d)), const((LANE, d)), const((D_QKV, d)),
                  tab, tab, tab],
        out_specs=[out_prev(D_QKV),
                   pl.BlockSpec((tr, d), lambda i: (cur(i), 0)),
                   out_prev(LANE)],
        out_shape=[jax.ShapeDtypeStruct((m_lat + m_ctx, D_QKV), BF16),
                   jax.ShapeDtypeStruct((m_lat + m_ctx, d), BF16),
                   jax.ShapeDtypeStruct((m_lat + m_ctx, LANE), F32)],
        scratch_shapes=[pltpu.VMEM((tr, d), BF16), pltpu.VMEM((tr, d), BF16)],
        compiler_params=_params(("arbitrary",), 56),
        name="norm_qkv",
    )(x2d, ctx2d, shift, scale, norm_g, w_dt, w_t, cos_t, sa_t, sb_t)


def _mm_kernel(x_ref, w_ref, o_ref):
    o_ref[...] = _dot_nt(x_ref[...], w_ref[...]).astype(o_ref.dtype)


def _matmul(x2d, w_t, col0, n, out_dtype, tn=2048):
    m, d = x2d.shape
    tm = _row_tile(m)
    assert n % tn == 0 and col0 % LANE == 0
    return pl.pallas_call(
        _mm_kernel,
        grid=(n // tn, m // tm),
        in_specs=[pl.BlockSpec((tm, d), lambda j, i: (i, 0)),
                  pl.BlockSpec((pl.Element(tn), pl.Element(d)),
                               lambda j, i: (pl.multiple_of(col0 + j * tn, LANE), 0))],
        out_specs=pl.BlockSpec((tm, tn), lambda j, i: (i, j)),
        out_shape=jax.ShapeDtypeStruct((m, n), out_dtype),
        compiler_params=_params(("parallel", "parallel"), 56),
        name="gzx_proj",
    )(x2d, w_t)


ATT_QBLOCKS = 4


def _attn_kernel(sink_ref, q_ref, kp_ref, ko_ref, kn_ref, kc_ref, vp_ref, vo_ref, vn_ref, vc_ref,
                 g_ref, ng_ref, o_ref, *, n_steps):
    n = pl.program_id(1)
    nctx = kc_ref.shape[0]
    nk = 3 * BLOCK + nctx
    cols_g = Q_PER_KV * BLOCK
    k_blocks = [kp_ref[...]] + [ko_ref[u * BLOCK:(u + 1) * BLOCK] for u in range(ATT_QBLOCKS)] + [kn_ref[...]]
    v_blocks = [vp_ref[...]] + [vo_ref[u * BLOCK:(u + 1) * BLOCK] for u in range(ATT_QBLOCKS)] + [vn_ref[...]]

    kj = lax.broadcasted_iota(jnp.int32, (BLOCK, BLOCK), 0)
    qi = lax.broadcasted_iota(jnp.int32, (BLOCK, BLOCK), 1)
    tile = lambda v: jnp.concatenate([v] * Q_PER_KV, axis=1)
    tri_prev = tile(jnp.where(kj >= qi, 0.0, NEG).astype(F32))
    tri_next = tile(jnp.where(kj <= qi, 0.0, NEG).astype(F32))
    edge_prev = tile(jnp.where((kj >= qi) & (n > 0), 0.0, NEG).astype(F32))
    edge_next = tile(jnp.where((kj <= qi) & (n < n_steps - 1), 0.0, NEG).astype(F32))
    head_of_col = lax.broadcasted_iota(jnp.int32, (1, cols_g), 1) // BLOCK
    n_loc = (ATT_QBLOCKS + 2) * BLOCK
    v_t = jnp.concatenate(v_blocks + [vc_ref[...]], axis=0).T
    tail_rows = 2 * SUBLANE
    ones_tail = jnp.where(lax.broadcasted_iota(jnp.int32, (tail_rows, nk), 0) == 0, 1.0, 0.0).astype(BF16)

    def scores_t(u, kh):
        hs = slice(kh * HEAD_DIM, (kh + 1) * HEAD_DIM)
        k_h = jnp.concatenate([blk[:, hs] for blk in k_blocks[u:u + 3]] + [kc_ref[:, hs]], axis=0)
        qg = jnp.concatenate(
            [q_ref[u * BLOCK:(u + 1) * BLOCK, (kh * Q_PER_KV + i) * HEAD_DIM:(kh * Q_PER_KV + i + 1) * HEAD_DIM]
             for i in range(Q_PER_KV)], axis=0)
        return lax.dot_general(k_h, qg, (((1,), (1,)), ((), ())), preferred_element_type=F32)

    def attend(u, kh, st):
        bias_prev = edge_prev if u == 0 else tri_prev
        bias_next = edge_next if u == ATT_QBLOCKS - 1 else tri_next
        st = jnp.concatenate([st[:BLOCK] + bias_prev, st[BLOCK:2 * BLOCK],
                              st[2 * BLOCK:3 * BLOCK] + bias_next, st[3 * BLOCK:]], axis=0)
        sink_row = jnp.zeros((1, cols_g), F32)
        for i in range(Q_PER_KV):
            sink_row = jnp.where(head_of_col == i, sink_ref[kh * Q_PER_KV + i] * LOG2E, sink_row)
        m = jnp.maximum(jnp.max(st, axis=0, keepdims=True), sink_row)
        pt = jnp.exp2(st - m).astype(BF16)
        hs = slice(kh * HEAD_DIM, (kh + 1) * HEAD_DIM)
        vt_h = jnp.concatenate([v_t[hs, u * BLOCK:(u + 3) * BLOCK], v_t[hs, n_loc:]], axis=1)
        ot = jnp.dot(jnp.concatenate([vt_h, ones_tail], axis=0), pt, preferred_element_type=F32)
        inv = 1.0 / (ot[HEAD_DIM:HEAD_DIM + 1] + jnp.exp2(sink_row - m))
        on = ot[:HEAD_DIM] * inv
        pairs = [jnp.concatenate([on[:, (2 * j) * BLOCK:(2 * j + 1) * BLOCK],
                                  on[:, (2 * j + 1) * BLOCK:(2 * j + 2) * BLOCK]], axis=0).T
                 for j in range(Q_PER_KV // 2)]
        return jnp.concatenate(pairs, axis=1)

    def finish(u, groups):
        rows = slice(u * BLOCK, (u + 1) * BLOCK)
        a = jnp.concatenate(groups, axis=1) * _silu(g_ref[rows])
        ms = jnp.mean(a * a, axis=-1, keepdims=True)
        o_ref[rows] = (a * lax.rsqrt(ms + EPS) * ng_ref[...]).astype(o_ref.dtype)

    order = [(u, kh) for u in range(ATT_QBLOCKS) for kh in range(ATT_KV_HEADS)]
    ahead = 1
    pending = [scores_t(*order[i]) for i in range(ahead)]
    groups = []
    for idx, (u, kh) in enumerate(order):
        s_cur = pending.pop(0)
        if idx + ahead < len(order):
            pending.append(scores_t(*order[idx + ahead]))
        groups.append(attend(u, kh, s_cur))
        if kh == ATT_KV_HEADS - 1:
            finish(u, groups)
            groups = []


def _attention(sink, qkv, gzx, att_norm_g, b, s, nc):
    rows = ATT_QBLOCKS * BLOCK
    assert s % rows == 0 and (b * s) % nc == 0
    nb = s // BLOCK
    n_steps = s // rows
    kcol, vcol = D_ATT // D_KV, D_ATT // D_KV + 1
    before = lambda col: pl.BlockSpec(
        (BLOCK, D_KV), lambda bi, n: (bi * nb + jnp.maximum(ATT_QBLOCKS * n - 1, 0), col))
    own = lambda col: pl.BlockSpec((rows, D_KV), lambda bi, n: (bi * n_steps + n, col))
    after = lambda col: pl.BlockSpec(
        (BLOCK, D_KV), lambda bi, n: (bi * nb + jnp.minimum(ATT_QBLOCKS * (n + 1), nb - 1), col))
    ctx_spec = lambda col: pl.BlockSpec((nc, D_KV), lambda bi, n: (b * s // nc + bi, col))
    q_rows = pl.BlockSpec((rows, D_ATT), lambda bi, n: (bi * n_steps + n, 0))
    return pl.pallas_call(
        functools.partial(_attn_kernel, n_steps=n_steps),
        grid=(b, n_steps),
        in_specs=[pl.BlockSpec(memory_space=pltpu.SMEM),
                  q_rows,
                  before(kcol), own(kcol), after(kcol), ctx_spec(kcol),
                  before(vcol), own(vcol), after(vcol), ctx_spec(vcol),
                  q_rows,
                  pl.BlockSpec((1, D_ATT), lambda bi, n: (0, 0))],
        out_specs=q_rows,
        out_shape=jax.ShapeDtypeStruct((b * s, D_ATT), BF16),
        compiler_params=_params(("parallel", "parallel"), 40),
        name="window_attn",
    )(sink, qkv, qkv, qkv, qkv, qkv, qkv, qkv, qkv, qkv, gzx, att_norm_g)


SSD_CPS = 2


def _ssd_visit_index(t, n_lat, n_ctx):
    n_all = n_lat + n_ctx
    k = t - n_all
    fwd = jnp.where(k < n_ctx, n_lat + k, k - n_ctx)
    return jnp.where(t >= n_all, fwd, n_all - 1 - t)


def _ssd_kernel(xbc_ref, hp_ref, hn_ref, dt_ref, z_ref, cw_ref, cb_ref, dtb_ref, alog_ref, dsk_ref,
                ng_ref, ex_ref, o_ref, sf_ref, sb_ref, sin_ref, ux_ref, ubc_ref, lhs_ref, pack_ref, y_ref, pad_ref,
                *, n_lat, n_ctx):
    t = pl.program_id(1)
    n_all = n_lat + n_ctx
    g_lat, g_ctx = n_lat // SSD_CPS, n_ctx // SSD_CPS
    g_all = g_lat + g_ctx
    gidx = _ssd_visit_index(t, g_lat, g_ctx)
    c_base = gidx * SSD_CPS
    sweep1 = t >= g_all
    is_lat = gidx < g_lat
    L = CHUNK
    gw = SSD_HPG * SSD_HEAD_DIM
    lane = lax.broadcasted_iota(jnp.int32, (1, LANE), 1)
    rows = lax.broadcasted_iota(jnp.int32, (L, 1), 0)
    groups = range(SSD_GROUPS)
    gsl = [slice(g * gw, (g + 1) * gw) for g in groups]

    @pl.when(t == 0)
    def _():
        sf_ref[...] = jnp.zeros_like(sf_ref)
        sb_ref[...] = jnp.zeros_like(sb_ref)
        ux_ref[n_all - 1] = jnp.zeros((CHUNK, D_SSD), F32)
        ubc_ref[n_all - 1] = jnp.zeros((CHUNK, 2 * D_BC), BF16)
        lhs_ref[n_all - 1] = jnp.zeros((CHUNK, 2 * LANE), BF16)

    def expand(lhs, block, g):
        c0 = block * D_SSD + g * gw
        return jnp.dot(lhs, ex_ref[:, c0:c0 + gw], preferred_element_type=F32)

    def update_state(state_ref, chunk, lhs_ex, g, wblock, eblock, end_row):
        r8 = (end_row // SUBLANE) * SUBLANE
        xw = (ux_ref[chunk, :, gsl[g]] * expand(lhs_ex, wblock, g)).astype(BF16)
        bg = ubc_ref[chunk, :, g * D_STATE:(g + 1) * D_STATE]
        loc = lax.dot_general(bg, xw, (((0,), (0,)), ((), ())), preferred_element_type=F32)
        tot = expand(lhs_ex[r8:r8 + SUBLANE], eblock, g)[end_row - r8:end_row - r8 + 1]
        new = state_ref[:, gsl[g]] * tot + loc
        state_ref[:, gsl[g]] = new
        return new

    @pl.when(jnp.logical_not(sweep1))
    def _():
        is_first = (gidx == 0) | (gidx == g_lat)
        is_last = (gidx == g_lat - 1) | (gidx == g_all - 1)
        span = SSD_CPS * L

        for k in range(D_XBC // LANE):
            lanes = slice(k * LANE, (k + 1) * LANE)
            pad_ref[k, SUBLANE:SUBLANE + span, :] = xbc_ref[:, lanes]
            pad_ref[k, SUBLANE - 1:SUBLANE, :] = jnp.where(is_first, 0.0, hp_ref[SUBLANE - 1:SUBLANE, lanes])
            pad_ref[k, SUBLANE + span:SUBLANE + span + 1, :] = jnp.where(is_last, 0.0, hn_ref[0:1, lanes])

        def conv_strip(u, j):
            r0 = SUBLANE + u * L
            for k in range(j * gw // LANE, (j + 1) * gw // LANE):
                lanes = slice(k * LANE, (k + 1) * LANE)
                v = (pad_ref[k, r0 - 1:r0 - 1 + L, :] * cw_ref[0:1, lanes] + pad_ref[k, r0:r0 + L, :] * cw_ref[1:2, lanes]
                     + pad_ref[k, r0 + 1:r0 + 1 + L, :] * cw_ref[2:3, lanes] + cb_ref[:, lanes])
                v = _silu(v)
                if k * LANE < D_SSD:
                    ux_ref[c_base + u, :, lanes] = v
                else:
                    ubc_ref[c_base + u, :, k * LANE - D_SSD:(k + 1) * LANE - D_SSD] = v.astype(BF16)

        def decay_tables(u):
            dtr = dt_ref[u * L:(u + 1) * L, :] + dtb_ref[...]
            dt = jnp.maximum(dtr, 0.0) + jnp.log(1.0 + jnp.exp(-jnp.abs(dtr)))
            a = dt * (-jnp.exp(alog_ref[...]))
            pre, suf = a, a
            d = 1
            while d < L:
                pre = pre + jnp.where(rows >= d, pltpu.roll(pre, d, 0), 0.0)
                suf = suf + jnp.where(rows < L - d, pltpu.roll(suf, L - d, 0), 0.0)
                d *= 2
            is_fwd_lane = lane < SSD_HEADS
            cs = jnp.where(is_fwd_lane, pre, suf)
            tot = jnp.where(is_fwd_lane, cs[L - 1:L, :], cs[0:1, :])
            e = jnp.exp(cs)
            wdt = jnp.exp(tot - cs) * dt
            pack_ref[c_base + u] = jnp.where(lane < 2 * SSD_HEADS, cs * LOG2E, pltpu.roll(dt, 2 * SSD_HEADS, 1))
            fac = jnp.where(lane < 2 * SSD_HEADS, wdt,
                            jnp.where(lane < 4 * SSD_HEADS, pltpu.roll(e, 2 * SSD_HEADS, 1), 0.0))
            hi = fac.astype(BF16).astype(F32)
            r1 = fac - hi
            mid = r1.astype(BF16).astype(F32)
            lo = (r1 - mid).astype(BF16)
            lhs_ref[c_base + u] = jnp.concatenate([(hi + pltpu.roll(mid, LANE // 2, 1)).astype(BF16), lo], axis=1)

        for u in reversed(range(SSD_CPS)):
            cidx = c_base + u
            done = jnp.minimum(cidx + 1, n_all - 1)
            lhs_done = lhs_ref[done]
            wexp = [expand(lhs_done, 1, g) for g in groups]
            tots = [expand(lhs_done[0:SUBLANE], 3, g)[0:1] for g in groups]
            conv_strip(u, 0)
            conv_strip(u, 1)
            xws = [(ux_ref[done, :, gsl[g]] * wexp[g]).astype(BF16) for g in groups]
            locs = [lax.dot_general(ubc_ref[done, :, g * D_STATE:(g + 1) * D_STATE], xws[g],
                                    (((0,), (0,)), ((), ())), preferred_element_type=F32) for g in groups]
            conv_strip(u, 2)
            conv_strip(u, 3)
            conv_strip(u, 4)
            for g in groups:
                new = sb_ref[:, gsl[g]] * tots[g] + locs[g]
                sb_ref[:, gsl[g]] = new
                sin_ref[jnp.minimum(cidx, n_lat - 1), :, gsl[g]] = new.astype(BF16)
            conv_strip(u, 5)
            conv_strip(u, 6)
            conv_strip(u, 7)
            decay_tables(u)

    @pl.when(sweep1 & jnp.logical_not(is_lat))
    def _():
        for u in range(SSD_CPS):
            lhs_ex = lhs_ref[c_base + u]
            for g in groups:
                update_state(sf_ref, c_base + u, lhs_ex, g, 0, 2, L - 1)

    @pl.when(sweep1 & is_lat)
    def _():
        li = lax.broadcasted_iota(jnp.int32, (L, L), 0)
        si = lax.broadcasted_iota(jnp.int32, (L, L), 1)
        masks = [si <= li, si >= li]
        head_of_lane = lax.broadcasted_iota(jnp.int32, (1, gw), 1) // SSD_HEAD_DIM
        lhs_exs = [lhs_ref[c_base + u] for u in range(SSD_CPS)]
        packs = [pack_ref[c_base + u] for u in range(SSD_CPS)]
        pack_ts = [p.T for p in packs]

        def early(u, g):
            cidx = c_base + u
            bg = ubc_ref[cidx, :, g * D_STATE:(g + 1) * D_STATE]
            cg = ubc_ref[cidx, :, D_BC + g * D_STATE:D_BC + (g + 1) * D_STATE]
            cb = lax.dot_general(cg, bg, (((1,), (1,)), ((), ())), preferred_element_type=F32)
            cbm = [jnp.where(m, cb, 0.0) for m in masks]
            y_off = (jnp.dot(cg, sf_ref[:, gsl[g]].astype(BF16), preferred_element_type=F32)
                     * expand(lhs_exs[u], 2, g)
                     + jnp.dot(cg, sin_ref[cidx, :, gsl[g]], preferred_element_type=F32)
                     * expand(lhs_exs[u], 3, g))
            update_state(sf_ref, cidx, lhs_exs[u], g, 0, 2, L - 1)
            return cbm, y_off

        def main(u, g, cbm):
            xg = ux_ref[c_base + u, :, gsl[g]].astype(BF16)
            blockdiag = jnp.concatenate(
                [jnp.where(head_of_lane == hh, xg, jnp.zeros_like(xg)) for hh in range(SSD_HPG)], axis=0)
            y_diag = None
            for dirn in range(2):
                ws = []
                for hh in range(SSD_HPG):
                    ln = dirn * SSD_HEADS + g * SSD_HPG + hh
                    col = packs[u][:, ln:ln + 1]
                    row = pack_ts[u][ln:ln + 1, :]
                    dtrow = pack_ts[u][2 * SSD_HEADS + ln:2 * SSD_HEADS + ln + 1, :]
                    dec = jnp.exp2(jnp.minimum(col - row, 0.0))
                    ws.append((cbm[dirn] * dec * dtrow).astype(BF16))
                yd = jnp.dot(jnp.concatenate(ws, axis=1), blockdiag, preferred_element_type=F32)
                y_diag = yd if y_diag is None else y_diag + yd
            return y_diag

        def finish(u, g, y_off, y_diag):
            yg = ux_ref[c_base + u, :, gsl[g]] * dsk_ref[:, gsl[g]] + y_diag + y_off
            yz = yg * _silu(z_ref[u * L:(u + 1) * L, gsl[g]])
            y_ref[u * L:(u + 1) * L, gsl[g]] = yz
            return jnp.sum(yz * yz, axis=-1, keepdims=True)

        units = [(u, g) for u in range(SSD_CPS) for g in groups]
        pending = {units[0]: early(*units[0])}
        diag = {}
        ss = [jnp.zeros((L, 1), F32) for _ in range(SSD_CPS)]
        for i, (u, g) in enumerate(units):
            if i + 1 < len(units):
                pending[units[i + 1]] = early(*units[i + 1])
            diag[(u, g)] = main(u, g, pending[(u, g)][0])
            if i >= 1:
                pu, pg = units[i - 1]
                ss[pu] = ss[pu] + finish(pu, pg, pending[(pu, pg)][1], diag[(pu, pg)])
        pu, pg = units[-1]
        ss[pu] = ss[pu] + finish(pu, pg, pending[(pu, pg)][1], diag[(pu, pg)])
        for u in range(SSD_CPS):
            rs = lax.rsqrt(ss[u] * (1.0 / D_SSD) + EPS)
            for g in groups:
                o_ref[u * L:(u + 1) * L, gsl[g]] = (
                    y_ref[u * L:(u + 1) * L, gsl[g]] * rs * ng_ref[:, gsl[g]]).astype(o_ref.dtype)


def _expansion_matrix():
    r = jnp.arange(2 * LANE)[:, None]
    col = jnp.arange(4 * D_SSD)[None, :]
    src = (col // D_SSD) * SSD_HEADS + (col % D_SSD) // SSD_HEAD_DIM
    half = LANE // 2
    return ((r % half == src) & (r < 3 * half)).astype(BF16)


def _ssd(gzx, dtraw, conv_w, conv_b, dt_bias, a_log, d_skip_exp, ssd_norm_g, b, s, nc):
    t_rows = gzx.shape[0]
    span = SSD_CPS * CHUNK
    assert s % span == 0 and nc % span == 0
    n_lat, n_ctx = s // CHUNK, nc // CHUNK
    g_lat, g_ctx = s // span, nc // span
    g_all = g_lat + g_ctx
    halo_per_span = span // SUBLANE
    n_halo = t_rows // SUBLANE
    c0 = lambda t: jnp.where(t < g_all, _ssd_visit_index(t, g_lat, g_ctx), 0)
    c1 = lambda t: jnp.where(t >= g_all + g_ctx, t - g_all - g_ctx, 0)
    blk = lambda bi, c: jnp.where(c < g_lat, bi * g_lat + c, b * g_lat + bi * g_ctx + (c - g_lat))
    xbc_col = (D_ATT + D_SSD) // D_XBC
    z_col = D_ATT // D_SSD
    const = lambda shape: pl.BlockSpec(shape, lambda bi, t: tuple(0 for _ in shape))
    return pl.pallas_call(
        functools.partial(_ssd_kernel, n_lat=n_lat, n_ctx=n_ctx),
        grid=(b, 2 * g_all),
        in_specs=[pl.BlockSpec((span, D_XBC), lambda bi, t: (blk(bi, c0(t)), xbc_col)),
                  pl.BlockSpec((SUBLANE, D_XBC),
                               lambda bi, t: (jnp.maximum(blk(bi, c0(t)) * halo_per_span - 1, 0), xbc_col)),
                  pl.BlockSpec((SUBLANE, D_XBC),
                               lambda bi, t: (jnp.minimum((blk(bi, c0(t)) + 1) * halo_per_span, n_halo - 1),
                                              xbc_col)),
                  pl.BlockSpec((span, LANE), lambda bi, t: (blk(bi, c0(t)), 0)),
                  pl.BlockSpec((span, D_SSD), lambda bi, t: (bi * g_lat + c1(t), z_col)),
                  const((CONV_W, D_XBC)), const((1, D_XBC)), const((1, LANE)), const((1, LANE)),
                  const((1, D_SSD)), const((1, D_SSD)), const((2 * LANE, 4 * D_SSD))],
        out_specs=pl.BlockSpec((span, D_SSD), lambda bi, t: (bi * g_lat + c1(t), 0)),
        out_shape=jax.ShapeDtypeStruct((b * s, D_SSD), BF16),
        scratch_shapes=[pltpu.VMEM((D_STATE, D_SSD), F32),
                        pltpu.VMEM((D_STATE, D_SSD), F32),
                        pltpu.VMEM((n_lat, D_STATE, D_SSD), BF16),
                        pltpu.VMEM((n_lat + n_ctx, CHUNK, D_SSD), F32),
                        pltpu.VMEM((n_lat + n_ctx, CHUNK, 2 * D_BC), BF16),
                        pltpu.VMEM((n_lat + n_ctx, CHUNK, 2 * LANE), BF16),
                        pltpu.VMEM((n_lat + n_ctx, CHUNK, LANE), F32),
                        pltpu.VMEM((span, D_SSD), F32),
                        pltpu.VMEM((D_XBC // LANE, span + 2 * SUBLANE, LANE), F32)],
        compiler_params=_params(("arbitrary", "arbitrary"), 60),
        name="bi_ssd",
    )(gzx, gzx, gzx, dtraw, gzx, conv_w, conv_b, dt_bias, a_log, d_skip_exp, ssd_norm_g,
      _expansion_matrix())


def _out_kernel(ha_ref, hs_ref, wa_ref, ws_ref, x_ref, gate_ref, ng_ref, o_ref):
    upd = (jnp.dot(ha_ref[...], wa_ref[...], preferred_element_type=F32)
           + jnp.dot(hs_ref[...], ws_ref[...], preferred_element_type=F32))
    y = x_ref[...] + gate_ref[0] * upd
    ms = jnp.mean(y * y, axis=-1, keepdims=True)
    o_ref[...] = (y * lax.rsqrt(ms + EPS) * ng_ref[...]).astype(o_ref.dtype)


def _out_proj(h_att, h_ssd, w_out_bf, x2d, gate, final_norm_g, s):
    m, d = x2d.shape
    tm = _row_tile(s, (512, 256, 128))
    tiles_per_batch = s // tm
    return pl.pallas_call(
        _out_kernel,
        grid=(m // tm,),
        in_specs=[pl.BlockSpec((tm, D_ATT), lambda i: (i, 0)),
                  pl.BlockSpec((tm, D_SSD), lambda i: (i, 0)),
                  pl.BlockSpec((D_ATT, d), lambda i: (0, 0)),
                  pl.BlockSpec((D_SSD, d), lambda i: (D_ATT // D_SSD, 0)),
                  pl.BlockSpec((tm, d), lambda i: (i, 0)),
                  pl.BlockSpec((1, 1, d), lambda i: (i // tiles_per_batch, 0, 0)),
                  pl.BlockSpec((1, d), lambda i: (0, 0))],
        out_specs=pl.BlockSpec((tm, d), lambda i: (i, 0)),
        out_shape=jax.ShapeDtypeStruct((m, d), x2d.dtype),
        compiler_params=_params(("parallel",), 48),
        name="out_proj",
    )(h_att, h_ssd, w_out_bf, w_out_bf, x2d, gate, final_norm_g)


def _rope_tables(s, pad_rows):
    pos = np.arange(s)
    row = (pos // GRID_W).astype(np.float32)
    colp = (pos % GRID_W).astype(np.float32)
    quarter = HEAD_DIM // 4
    freq = (1.0 / (np.float32(ROPE_THETA) ** (np.arange(quarter, dtype=np.float32) / np.float32(quarter)))
            ).astype(np.float32)
    lane = np.arange(LANE)
    dim = lane % HEAD_DIM
    use_col = (dim // (HEAD_DIM // 2)) == 1
    r = dim % (HEAD_DIM // 2)
    first = r < quarter
    ang = (np.where(use_col[None, :], colp[:, None], row[:, None]) * freq[r % quarter][None, :]).astype(np.float32)
    cos, sin = np.cos(ang), np.sin(ang)
    sa = np.where(first[None, :], -sin, 0.0)
    sb = np.where(first[None, :], 0.0, sin)
    ident = lambda v, fill: jnp.asarray(
        np.concatenate([v, np.full((pad_rows, LANE), fill)], axis=0).astype(np.float32))
    return ident(cos, 1.0), ident(sa, 0.0), ident(sb, 0.0)


def kernel(x, c, ctx, c_ctx, w_mod, b_mod, norm_g, w_in, conv_w, conv_b, a_log_f, a_log_b, dt_bias_f,
           dt_bias_b, d_skip, att_norm_g, ssd_norm_g, sink, w_out, final_norm_g):
    assert w_mod.shape[0] == 1, "single-layer operation"
    b, s, d = x.shape
    nc = ctx.shape[1]

    pad_rows = (-(b + 1)) % SUBLANE
    cc = jnp.concatenate([c, c_ctx[None, :], jnp.zeros((pad_rows, d), F32)], axis=0)
    mod = _modulation(cc, w_mod[0], b_mod)
    shift = mod[:b + 1, :d].reshape(b + 1, 1, d)
    scale = mod[:b + 1, d:2 * d].reshape(b + 1, 1, d)
    gate = mod[:b, 2 * d:].reshape(b, 1, d)

    w_t = jnp.swapaxes(w_in[0], 0, 1).astype(BF16)
    w_dt = jnp.pad(w_t[D_QKV + D_GZX:], ((0, LANE - 2 * SSD_HEADS), (0, 0)))
    w_out_bf = w_out[0].astype(BF16)

    cos_t, sa_t, sb_t = _rope_tables(s, NORM_TILE)
    qkv, xn, dtraw = _norm_qkv(x.reshape(b * s, d), ctx.reshape(b * nc, d), shift, scale, norm_g, w_dt, w_t,
                               cos_t, sa_t, sb_t, s)

    gzx = _matmul(xn, w_t, D_QKV, D_GZX, F32)

    h_att = _attention(sink[0], qkv, gzx, att_norm_g, b, s, nc)

    lane_pad = LANE - 2 * SSD_HEADS
    dt_bias = jnp.pad(jnp.concatenate([dt_bias_f[0], dt_bias_b[0]]), (0, lane_pad))[None, :]
    a_log = jnp.pad(jnp.concatenate([a_log_f[0], a_log_b[0]]), (0, lane_pad))[None, :]
    d_skip_exp = jnp.repeat(d_skip[0], SSD_HEAD_DIM)[None, :]
    h_ssd = _ssd(gzx, dtraw, conv_w[0], conv_b, dt_bias, a_log, d_skip_exp, ssd_norm_g, b, s, nc)

    out = _out_proj(h_att, h_ssd, w_out_bf, x.reshape(b * s, d), gate, final_norm_g[None, :], s)
    return out.reshape(b, s, d)
```

```python
import functools

import jax
import jax.numpy as jnp
import numpy as np
from jax import lax
from jax.experimental import pallas as pl
from jax.experimental.pallas import tpu as pltpu

F32 = jnp.float32
BF16 = jnp.bfloat16

EPS = 1e-6
GRID_W = 64
ROPE_THETA = 10000.0
ATT_HEADS = 16
ATT_KV_HEADS = 4
HEAD_DIM = 64
D_ATT = ATT_HEADS * HEAD_DIM
D_KV = ATT_KV_HEADS * HEAD_DIM
Q_PER_KV = ATT_HEADS // ATT_KV_HEADS
BLOCK = 128
SSD_HEADS = 16
SSD_HEAD_DIM = 64
D_SSD = SSD_HEADS * SSD_HEAD_DIM
SSD_GROUPS = 4
SSD_HPG = SSD_HEADS // SSD_GROUPS
D_STATE = 128
D_BC = SSD_GROUPS * D_STATE
D_XBC = D_SSD + 2 * D_BC
CONV_W = 3
CHUNK = 128
D_QKV = D_ATT + 2 * D_KV
D_GZX = D_ATT + D_SSD + D_XBC

LANE = 128
SUBLANE = 8
V7X_VMEM_BYTES = 64 * 1024 * 1024
MIB = 1024 * 1024

NEG = -1e30
LOG2E = 1.4426950408889634


def _silu(v):
    return v / (1.0 + jnp.exp(-v))


def _dot_nt(a, w):
    return lax.dot_general(a, w, (((1,), (1,)), ((), ())), preferred_element_type=F32)


def _params(semantics, vmem_mib):
    return pltpu.CompilerParams(dimension_semantics=semantics, vmem_limit_bytes=vmem_mib * MIB)


def _row_tile(m, candidates=(1024, 512, 256, 128)):
    for t in candidates:
        if m % t == 0:
            return t
    raise ValueError(f"row count {m} not tileable")


def _mod_kernel(c_ref, w_ref, b_ref, o_ref):
    a = _silu(c_ref[...]).astype(BF16)
    o_ref[...] = jnp.dot(a, w_ref[...].astype(BF16), preferred_element_type=F32) + b_ref[...]


def _modulation(cc, w_mod, b_mod):
    rows, d = cc.shape
    n = w_mod.shape[1]
    tn = 768
    assert n % tn == 0
    return pl.pallas_call(
        _mod_kernel,
        grid=(n // tn,),
        in_specs=[pl.BlockSpec((rows, d), lambda j: (0, 0)),
                  pl.BlockSpec((d, tn), lambda j: (0, j)),
                  pl.BlockSpec((1, tn), lambda j: (0, j))],
        out_specs=pl.BlockSpec((rows, tn), lambda j: (0, j)),
        out_shape=jax.ShapeDtypeStruct((rows, n), F32),
        compiler_params=_params(("parallel",), 32),
        name="modulation",
    )(cc, w_mod, b_mod)


def _norm_qkv_kernel(x_ref, ctx_ref, shift_ref, scale_ref, g_ref, wdt_ref, w_ref, cos_ref, sa_ref, sb_ref,
                     qkv_ref, xn_ref, dt_ref, ya_ref, yb_ref, *, n_lat_tiles, n_tiles, n_rope, n_q):
    i = pl.program_id(0)
    is_lat = jnp.minimum(i, n_tiles - 1) < n_lat_tiles

    @pl.when(i == 0)
    def _():
        yb_ref[...] = jnp.zeros_like(yb_ref)

    def step(y_prev_ref, y_next_ref):
        y = y_prev_ref[...]
        dt_ref[...] = _dot_nt(y, wdt_ref[...])
        acc = _dot_nt(y, w_ref[...])
        cos, sa, sb = cos_ref[...], sa_ref[...], sb_ref[...]
        quarter = HEAD_DIM // 4
        for jb in range(acc.shape[1] // LANE):
            blk = acc[:, jb * LANE:(jb + 1) * LANE]
            if jb < n_rope:
                blk = (blk * cos + pltpu.roll(blk, LANE - quarter, 1) * sa
                       + pltpu.roll(blk, quarter, 1) * sb)
            if jb < n_q:
                blk = blk * (HEAD_DIM ** -0.5 * LOG2E)
            qkv_ref[:, jb * LANE:(jb + 1) * LANE] = blk.astype(qkv_ref.dtype)

        v = jnp.where(is_lat, x_ref[...], ctx_ref[...])
        ms = jnp.mean(v * v, axis=-1, keepdims=True)
        gain = g_ref[...] * (1.0 + scale_ref[0])
        yn = (v * lax.rsqrt(ms + EPS) * gain + shift_ref[0]).astype(BF16)
        xn_ref[...] = yn
        y_next_ref[...] = yn

    @pl.when(i % 2 == 0)
    def _():
        step(yb_ref, ya_ref)

    @pl.when(i % 2 == 1)
    def _():
        step(ya_ref, yb_ref)


NORM_TILE = 512


def _norm_qkv(x2d, ctx2d, shift, scale, norm_g, w_dt, w_t, cos_t, sa_t, sb_t, s):
    m_lat, d = x2d.shape
    m_ctx = ctx2d.shape[0]
    b = m_lat // s
    tr = NORM_TILE
    assert s % tr == 0 and m_ctx % tr == 0
    tiles_per_seq = s // tr
    n_lat_tiles, n_ctx_tiles = m_lat // tr, m_ctx // tr
    n_tiles = n_lat_tiles + n_ctx_tiles
    cur = lambda i: jnp.minimum(i, n_tiles - 1)
    prev = lambda i: jnp.maximum(i - 1, 0)
    mod_idx = lambda i: (jnp.where(cur(i) < n_lat_tiles, cur(i) // tiles_per_seq, b), 0, 0)
    tab = pl.BlockSpec((tr, LANE), lambda i: (jnp.where(prev(i) < n_lat_tiles, prev(i) % tiles_per_seq,
                                                        tiles_per_seq), 0))
    out_prev = lambda width: pl.BlockSpec((tr, width), lambda i: (prev(i), 0))
    const = lambda shape: pl.BlockSpec(shape, lambda i: tuple(0 for _ in shape))
    return pl.pallas_call(
        functools.partial(_norm_qkv_kernel, n_lat_tiles=n_lat_tiles, n_tiles=n_tiles,
                          n_rope=(D_ATT + D_KV) // LANE, n_q=D_ATT // LANE),
        grid=(n_tiles + 1,),
        in_specs=[pl.BlockSpec((tr, d), lambda i: (jnp.minimum(cur(i), n_lat_tiles - 1), 0)),
                  pl.BlockSpec((tr, d), lambda i: (jnp.maximum(cur(i) - n_lat_tiles, 0), 0)),
                  pl.BlockSpec((1, 1, d), mod_idx),
                  pl.BlockSpec((1, 1, d), mod_idx),
                  const((1, d)), const((LANE, d)), const((D_QKV, d)),
                  tab, tab, tab],
        out_specs=[out_prev(D_QKV),
                   pl.BlockSpec((tr, d), lambda i: (cur(i), 0)),
                   out_prev(LANE)],
        out_shape=[jax.ShapeDtypeStruct((m_lat + m_ctx, D_QKV), BF16),
                   jax.ShapeDtypeStruct((m_lat + m_ctx, d), BF16),
                   jax.ShapeDtypeStruct((m_lat + m_ctx, LANE), F32)],
        scratch_shapes=[pltpu.VMEM((tr, d), BF16), pltpu.VMEM((tr, d), BF16)],
        compiler_params=_params(("arbitrary",), 56),
        name="norm_qkv",
    )(x2d, ctx2d, shift, scale, norm_g, w_dt, w_t, cos_t, sa_t, sb_t)


def _mm_kernel(x_ref, w_ref, o_ref):
    o_ref[...] = _dot_nt(x_ref[...], w_ref[...]).astype(o_ref.dtype)


def _matmul(x2d, w_t, col0, n, out_dtype, tn=2048):
    m, d = x2d.shape
    tm = _row_tile(m)
    assert n % tn == 0 and col0 % LANE == 0
    return pl.pallas_call(
        _mm_kernel,
        grid=(n // tn, m // tm),
        in_specs=[pl.BlockSpec((tm, d), lambda j, i: (i, 0)),
                  pl.BlockSpec((pl.Element(tn), pl.Element(d)),
                               lambda j, i: (pl.multiple_of(col0 + j * tn, LANE), 0))],
        out_specs=pl.BlockSpec((tm, tn), lambda j, i: (i, j)),
        out_shape=jax.ShapeDtypeStruct((m, n), out_dtype),
        compiler_params=_params(("parallel", "parallel"), 56),
        name="gzx_proj",
    )(x2d, w_t)


ATT_QBLOCKS = 4


def _attn_kernel(sink_ref, q_ref, kp_ref, ko_ref, kn_ref, kc_ref, vp_ref, vo_ref, vn_ref, vc_ref,
                 g_ref, ng_ref, o_ref, *, n_steps):
    n = pl.program_id(1)
    nctx = kc_ref.shape[0]
    nk = 3 * BLOCK + nctx
    cols_g = Q_PER_KV * BLOCK
    k_blocks = [kp_ref[...]] + [ko_ref[u * BLOCK:(u + 1) * BLOCK] for u in range(ATT_QBLOCKS)] + [kn_ref[...]]
    v_blocks = [vp_ref[...]] + [vo_ref[u * BLOCK:(u + 1) * BLOCK] for u in range(ATT_QBLOCKS)] + [vn_ref[...]]

    kj = lax.broadcasted_iota(jnp.int32, (BLOCK, BLOCK), 0)
    qi = lax.broadcasted_iota(jnp.int32, (BLOCK, BLOCK), 1)
    tile = lambda v: jnp.concatenate([v] * Q_PER_KV, axis=1)
    tri_prev = tile(jnp.where(kj >= qi, 0.0, NEG).astype(F32))
    tri_next = tile(jnp.where(kj <= qi, 0.0, NEG).astype(F32))
    edge_prev = tile(jnp.where((kj >= qi) & (n > 0), 0.0, NEG).astype(F32))
    edge_next = tile(jnp.where((kj <= qi) & (n < n_steps - 1), 0.0, NEG).astype(F32))
    head_of_col = lax.broadcasted_iota(jnp.int32, (1, cols_g), 1) // BLOCK
    n_loc = (ATT_QBLOCKS + 2) * BLOCK
    v_t = jnp.concatenate(v_blocks + [vc_ref[...]], axis=0).T
    tail_rows = 2 * SUBLANE
    ones_tail = jnp.where(lax.broadcasted_iota(jnp.int32, (tail_rows, nk), 0) == 0, 1.0, 0.0).astype(BF16)

    def scores_t(u, kh):
        hs = slice(kh * HEAD_DIM, (kh + 1) * HEAD_DIM)
        k_h = jnp.concatenate([blk[:, hs] for blk in k_blocks[u:u + 3]] + [kc_ref[:, hs]], axis=0)
        qg = jnp.concatenate(
            [q_ref[u * BLOCK:(u + 1) * BLOCK, (kh * Q_PER_KV + i) * HEAD_DIM:(kh * Q_PER_KV + i + 1) * HEAD_DIM]
             for i in range(Q_PER_KV)], axis=0)
        return lax.dot_general(k_h, qg, (((1,), (1,)), ((), ())), preferred_element_type=F32)

    def attend(u, kh, st):
        bias_prev = edge_prev if u == 0 else tri_prev
        bias_next = edge_next if u == ATT_QBLOCKS - 1 else tri_next
        st = jnp.concatenate([st[:BLOCK] + bias_prev, st[BLOCK:2 * BLOCK],
                              st[2 * BLOCK:3 * BLOCK] + bias_next, st[3 * BLOCK:]], axis=0)
        sink_row = jnp.zeros((1, cols_g), F32)
        for i in range(Q_PER_KV):
            sink_row = jnp.where(head_of_col == i, sink_ref[kh * Q_PER_KV + i] * LOG2E, sink_row)
        m = jnp.maximum(jnp.max(st, axis=0, keepdims=True), sink_row)
        pt = jnp.exp2(st - m).astype(BF16)
        hs = slice(kh * HEAD_DIM, (kh + 1) * HEAD_DIM)
        vt_h = jnp.concatenate([v_t[hs, u * BLOCK:(u + 3) * BLOCK], v_t[hs, n_loc:]], axis=1)
        ot = jnp.dot(jnp.concatenate([vt_h, ones_tail], axis=0), pt, preferred_element_type=F32)
        inv = 1.0 / (ot[HEAD_DIM:HEAD_DIM + 1] + jnp.exp2(sink_row - m))
        on = ot[:HEAD_DIM] * inv
        pairs = [jnp.concatenate([on[:, (2 * j) * BLOCK:(2 * j + 1) * BLOCK],
                                  on[:, (2 * j + 1) * BLOCK:(2 * j + 2) * BLOCK]], axis=0).T
                 for j in range(Q_PER_KV // 2)]
        return jnp.concatenate(pairs, axis=1)

    def finish(u, groups):
        rows = slice(u * BLOCK, (u + 1) * BLOCK)
        a = jnp.concatenate(groups, axis=1) * _silu(g_ref[rows])
        ms = jnp.mean(a * a, axis=-1, keepdims=True)
        o_ref[rows] = (a * lax.rsqrt(ms + EPS) * ng_ref[...]).astype(o_ref.dtype)

    order = [(u, kh) for u in range(ATT_QBLOCKS) for kh in range(ATT_KV_HEADS)]
    ahead = 1
    pending = [scores_t(*order[i]) for i in range(ahead)]
    groups = []
    for idx, (u, kh) in enumerate(order):
        s_cur = pending.pop(0)
        if idx + ahead < len(order):
            pending.append(scores_t(*order[idx + ahead]))
        groups.append(attend(u, kh, s_cur))
        if kh == ATT_KV_HEADS - 1:
            finish(u, groups)
            groups = []


def _attention(sink, qkv, gzx, att_norm_g, b, s, nc):
    rows = ATT_QBLOCKS * BLOCK
    assert s % rows == 0 and (b * s) % nc == 0
    nb = s // BLOCK
    n_steps = s // rows
    kcol, vcol = D_ATT // D_KV, D_ATT // D_KV + 1
    before = lambda col: pl.BlockSpec(
        (BLOCK, D_KV), lambda bi, n: (bi * nb + jnp.maximum(ATT_QBLOCKS * n - 1, 0), col))
    own = lambda col: pl.BlockSpec((rows, D_KV), lambda bi, n: (bi * n_steps + n, col))
    after = lambda col: pl.BlockSpec(
        (BLOCK, D_KV), lambda bi, n: (bi * nb + jnp.minimum(ATT_QBLOCKS * (n + 1), nb - 1), col))
    ctx_spec = lambda col: pl.BlockSpec((nc, D_KV), lambda bi, n: (b * s // nc + bi, col))
    q_rows = pl.BlockSpec((rows, D_ATT), lambda bi, n: (bi * n_steps + n, 0))
    return pl.pallas_call(
        functools.partial(_attn_kernel, n_steps=n_steps),
        grid=(b, n_steps),
        in_specs=[pl.BlockSpec(memory_space=pltpu.SMEM),
                  q_rows,
                  before(kcol), own(kcol), after(kcol), ctx_spec(kcol),
                  before(vcol), own(vcol), after(vcol), ctx_spec(vcol),
                  q_rows,
                  pl.BlockSpec((1, D_ATT), lambda bi, n: (0, 0))],
        out_specs=q_rows,
        out_shape=jax.ShapeDtypeStruct((b * s, D_ATT), BF16),
        compiler_params=_params(("parallel", "parallel"), 40),
        name="window_attn",
    )(sink, qkv, qkv, qkv, qkv, qkv, qkv, qkv, qkv, qkv, gzx, att_norm_g)


SSD_CPS = 2


def _ssd_visit_index(t, n_lat, n_ctx):
    n_all = n_lat + n_ctx
    k = t - n_all
    fwd = jnp.where(k < n_ctx, n_lat + k, k - n_ctx)
    return jnp.where(t >= n_all, fwd, n_all - 1 - t)


def _ssd_kernel(xbc_ref, hp_ref, hn_ref, dt_ref, z_ref, cw_ref, cb_ref, dtb_ref, alog_ref, dsk_ref,
                ng_ref, ex_ref, o_ref, sf_ref, sb_ref, sin_ref, ux_ref, ubc_ref, lhs_ref, pack_ref, y_ref, pad_ref,
                *, n_lat, n_ctx):
    t = pl.program_id(1)
    n_all = n_lat + n_ctx
    g_lat, g_ctx = n_lat // SSD_CPS, n_ctx // SSD_CPS
    g_all = g_lat + g_ctx
    gidx = _ssd_visit_index(t, g_lat, g_ctx)
    c_base = gidx * SSD_CPS
    sweep1 = t >= g_all
    is_lat = gidx < g_lat
    L = CHUNK
    gw = SSD_HPG * SSD_HEAD_DIM
    lane = lax.broadcasted_iota(jnp.int32, (1, LANE), 1)
    rows = lax.broadcasted_iota(jnp.int32, (L, 1), 0)
    groups = range(SSD_GROUPS)
    gsl = [slice(g * gw, (g + 1) * gw) for g in groups]

    @pl.when(t == 0)
    def _():
        sf_ref[...] = jnp.zeros_like(sf_ref)
        sb_ref[...] = jnp.zeros_like(sb_ref)
        ux_ref[n_all - 1] = jnp.zeros((CHUNK, D_SSD), F32)
        ubc_ref[n_all - 1] = jnp.zeros((CHUNK, 2 * D_BC), BF16)
        lhs_ref[n_all - 1] = jnp.zeros((CHUNK, 2 * LANE), BF16)

    def expand(lhs, block, g):
        c0 = block * D_SSD + g * gw
        return jnp.dot(lhs, ex_ref[:, c0:c0 + gw], preferred_element_type=F32)

    def update_state(state_ref, chunk, lhs_ex, g, wblock, eblock, end_row):
        r8 = (end_row // SUBLANE) * SUBLANE
        xw = (ux_ref[chunk, :, gsl[g]] * expand(lhs_ex, wblock, g)).astype(BF16)
        bg = ubc_ref[chunk, :, g * D_STATE:(g + 1) * D_STATE]
        loc = lax.dot_general(bg, xw, (((0,), (0,)), ((), ())), preferred_element_type=F32)
        tot = expand(lhs_ex[r8:r8 + SUBLANE], eblock, g)[end_row - r8:end_row - r8 + 1]
        new = state_ref[:, gsl[g]] * tot + loc
        state_ref[:, gsl[g]] = new
        return new

    @pl.when(jnp.logical_not(sweep1))
    def _():
        is_first = (gidx == 0) | (gidx == g_lat)
        is_last = (gidx == g_lat - 1) | (gidx == g_all - 1)
        span = SSD_CPS * L

        for k in range(D_XBC // LANE):
            lanes = slice(k * LANE, (k + 1) * LANE)
            pad_ref[k, SUBLANE:SUBLANE + span, :] = xbc_ref[:, lanes]
            pad_ref[k, SUBLANE - 1:SUBLANE, :] = jnp.where(is_first, 0.0, hp_ref[SUBLANE - 1:SUBLANE, lanes])
            pad_ref[k, SUBLANE + span:SUBLANE + span + 1, :] = jnp.where(is_last, 0.0, hn_ref[0:1, lanes])

        def conv_strip(u, j):
            r0 = SUBLANE + u * L
            for k in range(j * gw // LANE, (j + 1) * gw // LANE):
                lanes = slice(k * LANE, (k + 1) * LANE)
                v = (pad_ref[k, r0 - 1:r0 - 1 + L, :] * cw_ref[0:1, lanes] + pad_ref[k, r0:r0 + L, :] * cw_ref[1:2, lanes]
                     + pad_ref[k, r0 + 1:r0 + 1 + L, :] * cw_ref[2:3, lanes] + cb_ref[:, lanes])
                v = _silu(v)
                if k * LANE < D_SSD:
                    ux_ref[c_base + u, :, lanes] = v
                else:
                    ubc_ref[c_base + u, :, k * LANE - D_SSD:(k + 1) * LANE - D_SSD] = v.astype(BF16)

        def decay_tables(u):
            dtr = dt_ref[u * L:(u + 1) * L, :] + dtb_ref[...]
            dt = jnp.maximum(dtr, 0.0) + jnp.log(1.0 + jnp.exp(-jnp.abs(dtr)))
            a = dt * (-jnp.exp(alog_ref[...]))
            pre, suf = a, a
            d = 1
            while d < L:
                pre = pre + jnp.where(rows >= d, pltpu.roll(pre, d, 0), 0.0)
                suf = suf + jnp.where(rows < L - d, pltpu.roll(suf, L - d, 0), 0.0)
                d *= 2
            is_fwd_lane = lane < SSD_HEADS
            cs = jnp.where(is_fwd_lane, pre, suf)
            tot = jnp.where(is_fwd_lane, cs[L - 1:L, :], cs[0:1, :])
            e = jnp.exp(cs)
            wdt = jnp.exp(tot - cs) * dt
            pack_ref[c_base + u] = jnp.where(lane < 2 * SSD_HEADS, cs * LOG2E, pltpu.roll(dt, 2 * SSD_HEADS, 1))
            fac = jnp.where(lane < 2 * SSD_HEADS, wdt,
                            jnp.where(lane < 4 * SSD_HEADS, pltpu.roll(e, 2 * SSD_HEADS, 1), 0.0))
            hi = fac.astype(BF16).astype(F32)
            r1 = fac - hi
            mid = r1.astype(BF16).astype(F32)
            lo = (r1 - mid).astype(BF16)
            lhs_ref[c_base + u] = jnp.concatenate([(hi + pltpu.roll(mid, LANE // 2, 1)).astype(BF16), lo], axis=1)

        for u in reversed(range(SSD_CPS)):
            cidx = c_base + u
            done = jnp.minimum(cidx + 1, n_all - 1)
            lhs_done = lhs_ref[done]
            wexp = [expand(lhs_done, 1, g) for g in groups]
            tots = [expand(lhs_done[0:SUBLANE], 3, g)[0:1] for g in groups]
            conv_strip(u, 0)
            conv_strip(u, 1)
            xws = [(ux_ref[done, :, gsl[g]] * wexp[g]).astype(BF16) for g in groups]
            locs = [lax.dot_general(ubc_ref[done, :, g * D_STATE:(g + 1) * D_STATE], xws[g],
                                    (((0,), (0,)), ((), ())), preferred_element_type=F32) for g in groups]
            conv_strip(u, 2)
            conv_strip(u, 3)
            conv_strip(u, 4)
            for g in groups:
                new = sb_ref[:, gsl[g]] * tots[g] + locs[g]
                sb_ref[:, gsl[g]] = new
                sin_ref[jnp.minimum(cidx, n_lat - 1), :, gsl[g]] = new.astype(BF16)
            conv_strip(u, 5)
            conv_strip(u, 6)
            conv_strip(u, 7)
            decay_tables(u)

    @pl.when(sweep1 & jnp.logical_not(is_lat))
    def _():
        for u in range(SSD_CPS):
            lhs_ex = lhs_ref[c_base + u]
            for g in groups:
                update_state(sf_ref, c_base + u, lhs_ex, g, 0, 2, L - 1)

    @pl.when(sweep1 & is_lat)
    def _():
        li = lax.broadcasted_iota(jnp.int32, (L, L), 0)
        si = lax.broadcasted_iota(jnp.int32, (L, L), 1)
        masks = [si <= li, si >= li]
        head_of_lane = lax.broadcasted_iota(jnp.int32, (1, gw), 1) // SSD_HEAD_DIM
        lhs_exs = [lhs_ref[c_base + u] for u in range(SSD_CPS)]
        packs = [pack_ref[c_base + u] for u in range(SSD_CPS)]
        pack_ts = [p.T for p in packs]

        def early(u, g):
            cidx = c_base + u
            bg = ubc_ref[cidx, :, g * D_STATE:(g + 1) * D_STATE]
            cg = ubc_ref[cidx, :, D_BC + g * D_STATE:D_BC + (g + 1) * D_STATE]
            cb = lax.dot_general(cg, bg, (((1,), (1,)), ((), ())), preferred_element_type=F32)
            cbm = [jnp.where(m, cb, 0.0) for m in masks]
            y_off = (jnp.dot(cg, sf_ref[:, gsl[g]].astype(BF16), preferred_element_type=F32)
                     * expand(lhs_exs[u], 2, g)
                     + jnp.dot(cg, sin_ref[cidx, :, gsl[g]], preferred_element_type=F32)
                     * expand(lhs_exs[u], 3, g))
            update_state(sf_ref, cidx, lhs_exs[u], g, 0, 2, L - 1)
            return cbm, y_off

        def main(u, g, cbm):
            xg = ux_ref[c_base + u, :, gsl[g]].astype(BF16)
            blockdiag = jnp.concatenate(
                [jnp.where(head_of_lane == hh, xg, jnp.zeros_like(xg)) for hh in range(SSD_HPG)], axis=0)
            y_diag = None
            for dirn in range(2):
                ws = []
                for hh in range(SSD_HPG):
                    ln = dirn * SSD_HEADS + g * SSD_HPG + hh
                    col = packs[u][:, ln:ln + 1]
                    row = pack_ts[u][ln:ln + 1, :]
                    dtrow = pack_ts[u][2 * SSD_HEADS + ln:2 * SSD_HEADS + ln + 1, :]
                    dec = jnp.exp2(jnp.minimum(col - row, 0.0))
                    ws.append((cbm[dirn] * dec * dtrow).astype(BF16))
                yd = jnp.dot(jnp.concatenate(ws, axis=1), blockdiag, preferred_element_type=F32)
                y_diag = yd if y_diag is None else y_diag + yd
            return y_diag

        def finish(u, g, y_off, y_diag):
            yg = ux_ref[c_base + u, :, gsl[g]] * dsk_ref[:, gsl[g]] + y_diag + y_off
            yz = yg * _silu(z_ref[u * L:(u + 1) * L, gsl[g]])
            y_ref[u * L:(u + 1) * L, gsl[g]] = yz
            return jnp.sum(yz * yz, axis=-1, keepdims=True)

        units = [(u, g) for u in range(SSD_CPS) for g in groups]
        early_ahead = 4
        pending = {units[i]: early(*units[i]) for i in range(early_ahead)}
        diag = {}
        ss = [jnp.zeros((L, 1), F32) for _ in range(SSD_CPS)]
        for i, (u, g) in enumerate(units):
            if i + early_ahead < len(units):
                pending[units[i + early_ahead]] = early(*units[i + early_ahead])
            diag[(u, g)] = main(u, g, pending[(u, g)][0])
            if i >= 1:
                pu, pg = units[i - 1]
                ss[pu] = ss[pu] + finish(pu, pg, pending[(pu, pg)][1], diag[(pu, pg)])
        pu, pg = units[-1]
        ss[pu] = ss[pu] + finish(pu, pg, pending[(pu, pg)][1], diag[(pu, pg)])
        for u in range(SSD_CPS):
            rs = lax.rsqrt(ss[u] * (1.0 / D_SSD) + EPS)
            for g in groups:
                o_ref[u * L:(u + 1) * L, gsl[g]] = (
                    y_ref[u * L:(u + 1) * L, gsl[g]] * rs * ng_ref[:, gsl[g]]).astype(o_ref.dtype)


def _expansion_matrix():
    r = jnp.arange(2 * LANE)[:, None]
    col = jnp.arange(4 * D_SSD)[None, :]
    src = (col // D_SSD) * SSD_HEADS + (col % D_SSD) // SSD_HEAD_DIM
    half = LANE // 2
    return ((r % half == src) & (r < 3 * half)).astype(BF16)


def _ssd(gzx, dtraw, conv_w, conv_b, dt_bias, a_log, d_skip_exp, ssd_norm_g, b, s, nc):
    t_rows = gzx.shape[0]
    span = SSD_CPS * CHUNK
    assert s % span == 0 and nc % span == 0
    n_lat, n_ctx = s // CHUNK, nc // CHUNK
    g_lat, g_ctx = s // span, nc // span
    g_all = g_lat + g_ctx
    halo_per_span = span // SUBLANE
    n_halo = t_rows // SUBLANE
    c0 = lambda t: jnp.where(t < g_all, _ssd_visit_index(t, g_lat, g_ctx), 0)
    c1 = lambda t: jnp.where(t >= g_all + g_ctx, t - g_all - g_ctx, 0)
    blk = lambda bi, c: jnp.where(c < g_lat, bi * g_lat + c, b * g_lat + bi * g_ctx + (c - g_lat))
    xbc_col = (D_ATT + D_SSD) // D_XBC
    z_col = D_ATT // D_SSD
    const = lambda shape: pl.BlockSpec(shape, lambda bi, t: tuple(0 for _ in shape))
    return pl.pallas_call(
        functools.partial(_ssd_kernel, n_lat=n_lat, n_ctx=n_ctx),
        grid=(b, 2 * g_all),
        in_specs=[pl.BlockSpec((span, D_XBC), lambda bi, t: (blk(bi, c0(t)), xbc_col)),
                  pl.BlockSpec((SUBLANE, D_XBC),
                               lambda bi, t: (jnp.maximum(blk(bi, c0(t)) * halo_per_span - 1, 0), xbc_col)),
                  pl.BlockSpec((SUBLANE, D_XBC),
                               lambda bi, t: (jnp.minimum((blk(bi, c0(t)) + 1) * halo_per_span, n_halo - 1),
                                              xbc_col)),
                  pl.BlockSpec((span, LANE), lambda bi, t: (blk(bi, c0(t)), 0)),
                  pl.BlockSpec((span, D_SSD), lambda bi, t: (bi * g_lat + c1(t), z_col)),
                  const((CONV_W, D_XBC)), const((1, D_XBC)), const((1, LANE)), const((1, LANE)),
                  const((1, D_SSD)), const((1, D_SSD)), const((2 * LANE, 4 * D_SSD))],
        out_specs=pl.BlockSpec((span, D_SSD), lambda bi, t: (bi * g_lat + c1(t), 0)),
        out_shape=jax.ShapeDtypeStruct((b * s, D_SSD), BF16),
        scratch_shapes=[pltpu.VMEM((D_STATE, D_SSD), F32),
                        pltpu.VMEM((D_STATE, D_SSD), F32),
                        pltpu.VMEM((n_lat, D_STATE, D_SSD), BF16),
                        pltpu.VMEM((n_lat + n_ctx, CHUNK, D_SSD), F32),
                        pltpu.VMEM((n_lat + n_ctx, CHUNK, 2 * D_BC), BF16),
                        pltpu.VMEM((n_lat + n_ctx, CHUNK, 2 * LANE), BF16),
                        pltpu.VMEM((n_lat + n_ctx, CHUNK, LANE), F32),
                        pltpu.VMEM((span, D_SSD), F32),
                        pltpu.VMEM((D_XBC // LANE, span + 2 * SUBLANE, LANE), F32)],
        compiler_params=_params(("arbitrary", "arbitrary"), 60),
        name="bi_ssd",
    )(gzx, gzx, gzx, dtraw, gzx, conv_w, conv_b, dt_bias, a_log, d_skip_exp, ssd_norm_g,
      _expansion_matrix())


def _out_kernel(ha_ref, hs_ref, wa_ref, ws_ref, x_ref, gate_ref, ng_ref, o_ref):
    upd = (jnp.dot(ha_ref[...], wa_ref[...], preferred_element_type=F32)
           + jnp.dot(hs_ref[...], ws_ref[...], preferred_element_type=F32))
    y = x_ref[...] + gate_ref[0] * upd
    ms = jnp.mean(y * y, axis=-1, keepdims=True)
    o_ref[...] = (y * lax.rsqrt(ms + EPS) * ng_ref[...]).astype(o_ref.dtype)


def _out_proj(h_att, h_ssd, w_out_bf, x2d, gate, final_norm_g, s):
    m, d = x2d.shape
    tm = _row_tile(s, (512, 256, 128))
    tiles_per_batch = s // tm
    return pl.pallas_call(
        _out_kernel,
        grid=(m // tm,),
        in_specs=[pl.BlockSpec((tm, D_ATT), lambda i: (i, 0)),
                  pl.BlockSpec((tm, D_SSD), lambda i: (i, 0)),
                  pl.BlockSpec((D_ATT, d), lambda i: (0, 0)),
                  pl.BlockSpec((D_SSD, d), lambda i: (D_ATT // D_SSD, 0)),
                  pl.BlockSpec((tm, d), lambda i: (i, 0)),
                  pl.BlockSpec((1, 1, d), lambda i: (i // tiles_per_batch, 0, 0)),
                  pl.BlockSpec((1, d), lambda i: (0, 0))],
        out_specs=pl.BlockSpec((tm, d), lambda i: (i, 0)),
        out_shape=jax.ShapeDtypeStruct((m, d), x2d.dtype),
        compiler_params=_params(("parallel",), 48),
        name="out_proj",
    )(h_att, h_ssd, w_out_bf, w_out_bf, x2d, gate, final_norm_g)


def _rope_tables(s, pad_rows):
    pos = np.arange(s)
    row = (pos // GRID_W).astype(np.float32)
    colp = (pos % GRID_W).astype(np.float32)
    quarter = HEAD_DIM // 4
    freq = (1.0 / (np.float32(ROPE_THETA) ** (np.arange(quarter, dtype=np.float32) / np.float32(quarter)))
            ).astype(np.float32)
    lane = np.arange(LANE)
    dim = lane % HEAD_DIM
    use_col = (dim // (HEAD_DIM // 2)) == 1
    r = dim % (HEAD_DIM // 2)
    first = r < quarter
    ang = (np.where(use_col[None, :], colp[:, None], row[:, None]) * freq[r % quarter][None, :]).astype(np.float32)
    cos, sin = np.cos(ang), np.sin(ang)
    sa = np.where(first[None, :], -sin, 0.0)
    sb = np.where(first[None, :], 0.0, sin)
    ident = lambda v, fill: jnp.asarray(
        np.concatenate([v, np.full((pad_rows, LANE), fill)], axis=0).astype(np.float32))
    return ident(cos, 1.0), ident(sa, 0.0), ident(sb, 0.0)


def kernel(x, c, ctx, c_ctx, w_mod, b_mod, norm_g, w_in, conv_w, conv_b, a_log_f, a_log_b, dt_bias_f,
           dt_bias_b, d_skip, att_norm_g, ssd_norm_g, sink, w_out, final_norm_g):
    assert w_mod.shape[0] == 1, "single-layer operation"
    b, s, d = x.shape
    nc = ctx.shape[1]

    pad_rows = (-(b + 1)) % SUBLANE
    cc = jnp.concatenate([c, c_ctx[None, :], jnp.zeros((pad_rows, d), F32)], axis=0)
    mod = _modulation(cc, w_mod[0], b_mod)
    shift = mod[:b + 1, :d].reshape(b + 1, 1, d)
    scale = mod[:b + 1, d:2 * d].reshape(b + 1, 1, d)
    gate = mod[:b, 2 * d:].reshape(b, 1, d)

    w_t = jnp.swapaxes(w_in[0], 0, 1).astype(BF16)
    w_dt = jnp.pad(w_t[D_QKV + D_GZX:], ((0, LANE - 2 * SSD_HEADS), (0, 0)))
    w_out_bf = w_out[0].astype(BF16)

    cos_t, sa_t, sb_t = _rope_tables(s, NORM_TILE)
    qkv, xn, dtraw = _norm_qkv(x.reshape(b * s, d), ctx.reshape(b * nc, d), shift, scale, norm_g, w_dt, w_t,
                               cos_t, sa_t, sb_t, s)

    gzx = _matmul(xn, w_t, D_QKV, D_GZX, F32)

    h_att = _attention(sink[0], qkv, gzx, att_norm_g, b, s, nc)

    lane_pad = LANE - 2 * SSD_HEADS
    dt_bias = jnp.pad(jnp.concatenate([dt_bias_f[0], dt_bias_b[0]]), (0, lane_pad))[None, :]
    a_log = jnp.pad(jnp.concatenate([a_log_f[0], a_log_b[0]]), (0, lane_pad))[None, :]
    d_skip_exp = jnp.repeat(d_skip[0], SSD_HEAD_DIM)[None, :]
    h_ssd = _ssd(gzx, dtraw, conv_w[0], conv_b, dt_bias, a_log, d_skip_exp, ssd_norm_g, b, s, nc)

    out = _out_proj(h_att, h_ssd, w_out_bf, x.reshape(b * s, d), gate, final_norm_g[None, :], s)
    return out.reshape(b, s, d)
```

```python
import functools

import jax
import jax.numpy as jnp
import numpy as np
from jax import lax
from jax.experimental import pallas as pl
from jax.experimental.pallas import tpu as pltpu

F32 = jnp.float32
BF16 = jnp.bfloat16

EPS = 1e-6
GRID_W = 64
ROPE_THETA = 10000.0
ATT_HEADS = 16
ATT_KV_HEADS = 4
HEAD_DIM = 64
D_ATT = ATT_HEADS * HEAD_DIM
D_KV = ATT_KV_HEADS * HEAD_DIM
Q_PER_KV = ATT_HEADS // ATT_KV_HEADS
BLOCK = 128
SSD_HEADS = 16
SSD_HEAD_DIM = 64
D_SSD = SSD_HEADS * SSD_HEAD_DIM
SSD_GROUPS = 4
SSD_HPG = SSD_HEADS // SSD_GROUPS
D_STATE = 128
D_BC = SSD_GROUPS * D_STATE
D_XBC = D_SSD + 2 * D_BC
CONV_W = 3
CHUNK = 128
D_QKV = D_ATT + 2 * D_KV
D_GZX = D_ATT + D_SSD + D_XBC

LANE = 128
SUBLANE = 8
V7X_VMEM_BYTES = 64 * 1024 * 1024
MIB = 1024 * 1024

MOD_COLS = 768
NORM_TILE = 512
GZX_COLS = 2048
ATT_QBLOCKS = 4
SSD_CPS = 2
OUT_ROWS = (512, 256, 128)
VMEM_MIB = dict(modulation=32, norm_qkv=56, gzx_proj=56, window_attn=40, bi_ssd=60, out_proj=48)
assert max(VMEM_MIB.values()) * MIB < V7X_VMEM_BYTES

NEG = -1e30
LOG2E = 1.4426950408889634


def _silu(v):
    return v / (1.0 + jnp.exp(-v))


def _dot_nt(a, w):
    return lax.dot_general(a, w, (((1,), (1,)), ((), ())), preferred_element_type=F32)


def _params(name, semantics):
    return pltpu.CompilerParams(dimension_semantics=semantics, vmem_limit_bytes=VMEM_MIB[name] * MIB)


def _row_tile(m, candidates=(1024, 512, 256, 128)):
    for t in candidates:
        if m % t == 0:
            return t
    raise ValueError(f"row count {m} not tileable")


def _mod_kernel(c_ref, w_ref, b_ref, o_ref):
    a = _silu(c_ref[...]).astype(BF16)
    o_ref[...] = jnp.dot(a, w_ref[...].astype(BF16), preferred_element_type=F32) + b_ref[...]


def _modulation(cc, w_mod, b_mod):
    rows, d = cc.shape
    n = w_mod.shape[1]
    tn = MOD_COLS
    assert n % tn == 0
    return pl.pallas_call(
        _mod_kernel,
        grid=(n // tn,),
        in_specs=[pl.BlockSpec((rows, d), lambda j: (0, 0)),
                  pl.BlockSpec((d, tn), lambda j: (0, j)),
                  pl.BlockSpec((1, tn), lambda j: (0, j))],
        out_specs=pl.BlockSpec((rows, tn), lambda j: (0, j)),
        out_shape=jax.ShapeDtypeStruct((rows, n), F32),
        compiler_params=_params("modulation", ("parallel",)),
        name="modulation",
    )(cc, w_mod, b_mod)


def _norm_qkv_kernel(x_ref, ctx_ref, shift_ref, scale_ref, g_ref, wdt_ref, w_ref, cos_ref, sa_ref, sb_ref,
                     qkv_ref, xn_ref, dt_ref, ya_ref, yb_ref, *, n_lat_tiles, n_tiles, n_rope, n_q):
    i = pl.program_id(0)
    is_lat = jnp.minimum(i, n_tiles - 1) < n_lat_tiles

    @pl.when(i == 0)
    def _():
        yb_ref[...] = jnp.zeros_like(yb_ref)

    def step(y_prev_ref, y_next_ref):
        y = y_prev_ref[...]
        dt_ref[...] = _dot_nt(y, wdt_ref[...])
        acc = _dot_nt(y, w_ref[...])
        cos, sa, sb = cos_ref[...], sa_ref[...], sb_ref[...]
        quarter = HEAD_DIM // 4
        for jb in range(acc.shape[1] // LANE):
            blk = acc[:, jb * LANE:(jb + 1) * LANE]
            if jb < n_rope:
                blk = (blk * cos + pltpu.roll(blk, LANE - quarter, 1) * sa
                       + pltpu.roll(blk, quarter, 1) * sb)
            if jb < n_q:
                blk = blk * (HEAD_DIM ** -0.5 * LOG2E)
            qkv_ref[:, jb * LANE:(jb + 1) * LANE] = blk.astype(qkv_ref.dtype)

        v = jnp.where(is_lat, x_ref[...], ctx_ref[...])
        ms = jnp.mean(v * v, axis=-1, keepdims=True)
        gain = g_ref[...] * (1.0 + scale_ref[0])
        yn = (v * lax.rsqrt(ms + EPS) * gain + shift_ref[0]).astype(BF16)
        xn_ref[...] = yn
        y_next_ref[...] = yn

    @pl.when(i % 2 == 0)
    def _():
        step(yb_ref, ya_ref)

    @pl.when(i % 2 == 1)
    def _():
        step(ya_ref, yb_ref)


def _norm_qkv(x2d, ctx2d, shift, scale, norm_g, w_dt, w_t, cos_t, sa_t, sb_t, s):
    m_lat, d = x2d.shape
    m_ctx = ctx2d.shape[0]
    b = m_lat // s
    tr = NORM_TILE
    assert s % tr == 0 and m_ctx % tr == 0
    tiles_per_seq = s // tr
    n_lat_tiles, n_ctx_tiles = m_lat // tr, m_ctx // tr
    n_tiles = n_lat_tiles + n_ctx_tiles
    cur = lambda i: jnp.minimum(i, n_tiles - 1)
    prev = lambda i: jnp.maximum(i - 1, 0)
    mod_idx = lambda i: (jnp.where(cur(i) < n_lat_tiles, cur(i) // tiles_per_seq, b), 0, 0)
    tab = pl.BlockSpec((tr, LANE), lambda i: (jnp.where(prev(i) < n_lat_tiles, prev(i) % tiles_per_seq,
                                                        tiles_per_seq), 0))
    out_prev = lambda width: pl.BlockSpec((tr, width), lambda i: (prev(i), 0))
    const = lambda shape: pl.BlockSpec(shape, lambda i: tuple(0 for _ in shape))
    return pl.pallas_call(
        functools.partial(_norm_qkv_kernel, n_lat_tiles=n_lat_tiles, n_tiles=n_tiles,
                          n_rope=(D_ATT + D_KV) // LANE, n_q=D_ATT // LANE),
        grid=(n_tiles + 1,),
        in_specs=[pl.BlockSpec((tr, d), lambda i: (jnp.minimum(cur(i), n_lat_tiles - 1), 0)),
                  pl.BlockSpec((tr, d), lambda i: (jnp.maximum(cur(i) - n_lat_tiles, 0), 0)),
                  pl.BlockSpec((1, 1, d), mod_idx),
                  pl.BlockSpec((1, 1, d), mod_idx),
                  const((1, d)), const((LANE, d)), const((D_QKV, d)),
                  tab, tab, tab],
        out_specs=[out_prev(D_QKV),
                   pl.BlockSpec((tr, d), lambda i: (cur(i), 0)),
                   out_prev(LANE)],
        out_shape=[jax.ShapeDtypeStruct((m_lat + m_ctx, D_QKV), BF16),
                   jax.ShapeDtypeStruct((m_lat + m_ctx, d), BF16),
                   jax.ShapeDtypeStruct((m_lat + m_ctx, LANE), F32)],
        scratch_shapes=[pltpu.VMEM((tr, d), BF16), pltpu.VMEM((tr, d), BF16)],
        compiler_params=_params("norm_qkv", ("arbitrary",)),
        name="norm_qkv",
    )(x2d, ctx2d, shift, scale, norm_g, w_dt, w_t, cos_t, sa_t, sb_t)


def _mm_kernel(x_ref, w_ref, o_ref):
    o_ref[...] = _dot_nt(x_ref[...], w_ref[...]).astype(o_ref.dtype)


def _matmul(x2d, w_t, col0, n, out_dtype, tn=GZX_COLS):
    m, d = x2d.shape
    tm = _row_tile(m)
    assert n % tn == 0 and col0 % LANE == 0
    return pl.pallas_call(
        _mm_kernel,
        grid=(n // tn, m // tm),
        in_specs=[pl.BlockSpec((tm, d), lambda j, i: (i, 0)),
                  pl.BlockSpec((pl.Element(tn), pl.Element(d)),
                               lambda j, i: (pl.multiple_of(col0 + j * tn, LANE), 0))],
        out_specs=pl.BlockSpec((tm, tn), lambda j, i: (i, j)),
        out_shape=jax.ShapeDtypeStruct((m, n), out_dtype),
        compiler_params=_params("gzx_proj", ("parallel", "parallel")),
        name="gzx_proj",
    )(x2d, w_t)


def _attn_kernel(sink_ref, q_ref, kp_ref, ko_ref, kn_ref, kc_ref, vp_ref, vo_ref, vn_ref, vc_ref,
                 g_ref, ng_ref, o_ref, *, n_steps):
    n = pl.program_id(1)
    nctx = kc_ref.shape[0]
    nk = 3 * BLOCK + nctx
    cols_g = Q_PER_KV * BLOCK
    k_blocks = [kp_ref[...]] + [ko_ref[u * BLOCK:(u + 1) * BLOCK] for u in range(ATT_QBLOCKS)] + [kn_ref[...]]
    v_blocks = [vp_ref[...]] + [vo_ref[u * BLOCK:(u + 1) * BLOCK] for u in range(ATT_QBLOCKS)] + [vn_ref[...]]

    kj = lax.broadcasted_iota(jnp.int32, (BLOCK, BLOCK), 0)
    qi = lax.broadcasted_iota(jnp.int32, (BLOCK, BLOCK), 1)
    tile = lambda v: jnp.concatenate([v] * Q_PER_KV, axis=1)
    tri_prev = tile(jnp.where(kj >= qi, 0.0, NEG).astype(F32))
    tri_next = tile(jnp.where(kj <= qi, 0.0, NEG).astype(F32))
    edge_prev = tile(jnp.where((kj >= qi) & (n > 0), 0.0, NEG).astype(F32))
    edge_next = tile(jnp.where((kj <= qi) & (n < n_steps - 1), 0.0, NEG).astype(F32))
    head_of_col = lax.broadcasted_iota(jnp.int32, (1, cols_g), 1) // BLOCK
    n_loc = (ATT_QBLOCKS + 2) * BLOCK
    v_t = jnp.concatenate(v_blocks + [vc_ref[...]], axis=0).T
    tail_rows = 2 * SUBLANE
    ones_tail = jnp.where(lax.broadcasted_iota(jnp.int32, (tail_rows, nk), 0) == 0, 1.0, 0.0).astype(BF16)

    def scores_t(u, kh):
        hs = slice(kh * HEAD_DIM, (kh + 1) * HEAD_DIM)
        k_h = jnp.concatenate([blk[:, hs] for blk in k_blocks[u:u + 3]] + [kc_ref[:, hs]], axis=0)
        qg = jnp.concatenate(
            [q_ref[u * BLOCK:(u + 1) * BLOCK, (kh * Q_PER_KV + i) * HEAD_DIM:(kh * Q_PER_KV + i + 1) * HEAD_DIM]
             for i in range(Q_PER_KV)], axis=0)
        return lax.dot_general(k_h, qg, (((1,), (1,)), ((), ())), preferred_element_type=F32)

    def attend(u, kh, st):
        bias_prev = edge_prev if u == 0 else tri_prev
        bias_next = edge_next if u == ATT_QBLOCKS - 1 else tri_next
        st = jnp.concatenate([st[:BLOCK] + bias_prev, st[BLOCK:2 * BLOCK],
                              st[2 * BLOCK:3 * BLOCK] + bias_next, st[3 * BLOCK:]], axis=0)
        sink_row = jnp.zeros((1, cols_g), F32)
        for i in range(Q_PER_KV):
            sink_row = jnp.where(head_of_col == i, sink_ref[kh * Q_PER_KV + i] * LOG2E, sink_row)
        m = jnp.maximum(jnp.max(st, axis=0, keepdims=True), sink_row)
        pt = jnp.exp2(st - m).astype(BF16)
        hs = slice(kh * HEAD_DIM, (kh + 1) * HEAD_DIM)
        vt_h = jnp.concatenate([v_t[hs, u * BLOCK:(u + 3) * BLOCK], v_t[hs, n_loc:]], axis=1)
        ot = jnp.dot(jnp.concatenate([vt_h, ones_tail], axis=0), pt, preferred_element_type=F32)
        inv = 1.0 / (ot[HEAD_DIM:HEAD_DIM + 1] + jnp.exp2(sink_row - m))
        on = ot[:HEAD_DIM] * inv
        pairs = [jnp.concatenate([on[:, (2 * j) * BLOCK:(2 * j + 1) * BLOCK],
                                  on[:, (2 * j + 1) * BLOCK:(2 * j + 2) * BLOCK]], axis=0).T
                 for j in range(Q_PER_KV // 2)]
        return jnp.concatenate(pairs, axis=1)

    def finish(u, groups):
        rows = slice(u * BLOCK, (u + 1) * BLOCK)
        a = jnp.concatenate(groups, axis=1) * _silu(g_ref[rows])
        ms = jnp.mean(a * a, axis=-1, keepdims=True)
        o_ref[rows] = (a * lax.rsqrt(ms + EPS) * ng_ref[...]).astype(o_ref.dtype)

    order = [(u, kh) for u in range(ATT_QBLOCKS) for kh in range(ATT_KV_HEADS)]
    ahead = 1
    pending = [scores_t(*order[i]) for i in range(ahead)]
    groups = []
    for idx, (u, kh) in enumerate(order):
        s_cur = pending.pop(0)
        if idx + ahead < len(order):
            pending.append(scores_t(*order[idx + ahead]))
        groups.append(attend(u, kh, s_cur))
        if kh == ATT_KV_HEADS - 1:
            finish(u, groups)
            groups = []


def _attention(sink, qkv, gzx, att_norm_g, b, s, nc):
    rows = ATT_QBLOCKS * BLOCK
    assert s % rows == 0 and (b * s) % nc == 0
    nb = s // BLOCK
    n_steps = s // rows
    kcol, vcol = D_ATT // D_KV, D_ATT // D_KV + 1
    before = lambda col: pl.BlockSpec(
        (BLOCK, D_KV), lambda bi, n: (bi * nb + jnp.maximum(ATT_QBLOCKS * n - 1, 0), col))
    own = lambda col: pl.BlockSpec((rows, D_KV), lambda bi, n: (bi * n_steps + n, col))
    after = lambda col: pl.BlockSpec(
        (BLOCK, D_KV), lambda bi, n: (bi * nb + jnp.minimum(ATT_QBLOCKS * (n + 1), nb - 1), col))
    ctx_spec = lambda col: pl.BlockSpec((nc, D_KV), lambda bi, n: (b * s // nc + bi, col))
    q_rows = pl.BlockSpec((rows, D_ATT), lambda bi, n: (bi * n_steps + n, 0))
    return pl.pallas_call(
        functools.partial(_attn_kernel, n_steps=n_steps),
        grid=(b, n_steps),
        in_specs=[pl.BlockSpec(memory_space=pltpu.SMEM),
                  q_rows,
                  before(kcol), own(kcol), after(kcol), ctx_spec(kcol),
                  before(vcol), own(vcol), after(vcol), ctx_spec(vcol),
                  q_rows,
                  pl.BlockSpec((1, D_ATT), lambda bi, n: (0, 0))],
        out_specs=q_rows,
        out_shape=jax.ShapeDtypeStruct((b * s, D_ATT), BF16),
        compiler_params=_params("window_attn", ("parallel", "parallel")),
        name="window_attn",
    )(sink, qkv, qkv, qkv, qkv, qkv, qkv, qkv, qkv, qkv, gzx, att_norm_g)


def _ssd_visit_index(t, n_lat, n_ctx):
    n_all = n_lat + n_ctx
    k = t - n_all
    fwd = jnp.where(k < n_ctx, n_lat + k, k - n_ctx)
    return jnp.where(t >= n_all, fwd, n_all - 1 - t)


def _ssd_kernel(xbc_ref, hp_ref, hn_ref, dt_ref, z_ref, cw_ref, cb_ref, dtb_ref, alog_ref, dsk_ref,
                ng_ref, ex_ref, o_ref, sf_ref, sb_ref, sin_ref, ux_ref, ubc_ref, lhs_ref, pack_ref, y_ref, pad_ref,
                *, n_lat, n_ctx):
    t = pl.program_id(1)
    n_all = n_lat + n_ctx
    g_lat, g_ctx = n_lat // SSD_CPS, n_ctx // SSD_CPS
    g_all = g_lat + g_ctx
    gidx = _ssd_visit_index(t, g_lat, g_ctx)
    c_base = gidx * SSD_CPS
    sweep1 = t >= g_all
    is_lat = gidx < g_lat
    L = CHUNK
    gw = SSD_HPG * SSD_HEAD_DIM
    lane = lax.broadcasted_iota(jnp.int32, (1, LANE), 1)
    rows = lax.broadcasted_iota(jnp.int32, (L, 1), 0)
    groups = range(SSD_GROUPS)
    gsl = [slice(g * gw, (g + 1) * gw) for g in groups]

    @pl.when(t == 0)
    def _():
        sf_ref[...] = jnp.zeros_like(sf_ref)
        sb_ref[...] = jnp.zeros_like(sb_ref)
        ux_ref[n_all - 1] = jnp.zeros((CHUNK, D_SSD), F32)
        ubc_ref[n_all - 1] = jnp.zeros((CHUNK, 2 * D_BC), BF16)
        lhs_ref[n_all - 1] = jnp.zeros((CHUNK, 2 * LANE), BF16)

    def expand(lhs, block, g):
        c0 = block * D_SSD + g * gw
        return jnp.dot(lhs, ex_ref[:, c0:c0 + gw], preferred_element_type=F32)

    def update_state(state_ref, chunk, lhs_ex, g, wblock, eblock, end_row):
        r8 = (end_row // SUBLANE) * SUBLANE
        xw = (ux_ref[chunk, :, gsl[g]] * expand(lhs_ex, wblock, g)).astype(BF16)
        bg = ubc_ref[chunk, :, g * D_STATE:(g + 1) * D_STATE]
        loc = lax.dot_general(bg, xw, (((0,), (0,)), ((), ())), preferred_element_type=F32)
        tot = expand(lhs_ex[r8:r8 + SUBLANE], eblock, g)[end_row - r8:end_row - r8 + 1]
        new = state_ref[:, gsl[g]] * tot + loc
        state_ref[:, gsl[g]] = new
        return new

    @pl.when(jnp.logical_not(sweep1))
    def _():
        is_first = (gidx == 0) | (gidx == g_lat)
        is_last = (gidx == g_lat - 1) | (gidx == g_all - 1)
        span = SSD_CPS * L

        for k in range(D_XBC // LANE):
            lanes = slice(k * LANE, (k + 1) * LANE)
            pad_ref[k, SUBLANE:SUBLANE + span, :] = xbc_ref[:, lanes]
            pad_ref[k, SUBLANE - 1:SUBLANE, :] = jnp.where(is_first, 0.0, hp_ref[SUBLANE - 1:SUBLANE, lanes])
            pad_ref[k, SUBLANE + span:SUBLANE + span + 1, :] = jnp.where(is_last, 0.0, hn_ref[0:1, lanes])

        def conv_strip(u, j):
            r0 = SUBLANE + u * L
            for k in range(j * gw // LANE, (j + 1) * gw // LANE):
                lanes = slice(k * LANE, (k + 1) * LANE)
                v = (pad_ref[k, r0 - 1:r0 - 1 + L, :] * cw_ref[0:1, lanes] + pad_ref[k, r0:r0 + L, :] * cw_ref[1:2, lanes]
                     + pad_ref[k, r0 + 1:r0 + 1 + L, :] * cw_ref[2:3, lanes] + cb_ref[:, lanes])
                v = _silu(v)
                if k * LANE < D_SSD:
                    ux_ref[c_base + u, :, lanes] = v
                else:
                    ubc_ref[c_base + u, :, k * LANE - D_SSD:(k + 1) * LANE - D_SSD] = v.astype(BF16)

        def decay_tables(u):
            dtr = dt_ref[u * L:(u + 1) * L, :] + dtb_ref[...]
            dt = jnp.maximum(dtr, 0.0) + jnp.log(1.0 + jnp.exp(-jnp.abs(dtr)))
            a = dt * (-jnp.exp(alog_ref[...]))
            pre, suf = a, a
            d = 1
            while d < L:
                pre = pre + jnp.where(rows >= d, pltpu.roll(pre, d, 0), 0.0)
                suf = suf + jnp.where(rows < L - d, pltpu.roll(suf, L - d, 0), 0.0)
                d *= 2
            is_fwd_lane = lane < SSD_HEADS
            cs = jnp.where(is_fwd_lane, pre, suf)
            tot = jnp.where(is_fwd_lane, cs[L - 1:L, :], cs[0:1, :])
            e = jnp.exp(cs)
            wdt = jnp.exp(tot - cs) * dt
            pack_ref[c_base + u] = jnp.where(lane < 2 * SSD_HEADS, cs * LOG2E, pltpu.roll(dt, 2 * SSD_HEADS, 1))
            fac = jnp.where(lane < 2 * SSD_HEADS, wdt,
                            jnp.where(lane < 4 * SSD_HEADS, pltpu.roll(e, 2 * SSD_HEADS, 1), 0.0))
            hi = fac.astype(BF16).astype(F32)
            r1 = fac - hi
            mid = r1.astype(BF16).astype(F32)
            lo = (r1 - mid).astype(BF16)
            lhs_ref[c_base + u] = jnp.concatenate([(hi + pltpu.roll(mid, LANE // 2, 1)).astype(BF16), lo], axis=1)

        for u in reversed(range(SSD_CPS)):
            cidx = c_base + u
            done = jnp.minimum(cidx + 1, n_all - 1)
            lhs_done = lhs_ref[done]
            wexp = [expand(lhs_done, 1, g) for g in groups]
            tots = [expand(lhs_done[0:SUBLANE], 3, g)[0:1] for g in groups]
            conv_strip(u, 0)
            conv_strip(u, 1)
            xws = [(ux_ref[done, :, gsl[g]] * wexp[g]).astype(BF16) for g in groups]
            locs = [lax.dot_general(ubc_ref[done, :, g * D_STATE:(g + 1) * D_STATE], xws[g],
                                    (((0,), (0,)), ((), ())), preferred_element_type=F32) for g in groups]
            conv_strip(u, 2)
            conv_strip(u, 3)
            conv_strip(u, 4)
            for g in groups:
                new = sb_ref[:, gsl[g]] * tots[g] + locs[g]
                sb_ref[:, gsl[g]] = new
                sin_ref[jnp.minimum(cidx, n_lat - 1), :, gsl[g]] = new.astype(BF16)
            conv_strip(u, 5)
            conv_strip(u, 6)
            conv_strip(u, 7)
            decay_tables(u)

    @pl.when(sweep1 & jnp.logical_not(is_lat))
    def _():
        for u in range(SSD_CPS):
            lhs_ex = lhs_ref[c_base + u]
            for g in groups:
                update_state(sf_ref, c_base + u, lhs_ex, g, 0, 2, L - 1)

    @pl.when(sweep1 & is_lat)
    def _():
        li = lax.broadcasted_iota(jnp.int32, (L, L), 0)
        si = lax.broadcasted_iota(jnp.int32, (L, L), 1)
        masks = [si <= li, si >= li]
        head_of_lane = lax.broadcasted_iota(jnp.int32, (1, gw), 1) // SSD_HEAD_DIM
        lhs_exs = [lhs_ref[c_base + u] for u in range(SSD_CPS)]
        packs = [pack_ref[c_base + u] for u in range(SSD_CPS)]
        pack_ts = [p.T for p in packs]

        def early(u, g):
            cidx = c_base + u
            bg = ubc_ref[cidx, :, g * D_STATE:(g + 1) * D_STATE]
            cg = ubc_ref[cidx, :, D_BC + g * D_STATE:D_BC + (g + 1) * D_STATE]
            cb = lax.dot_general(cg, bg, (((1,), (1,)), ((), ())), preferred_element_type=F32)
            cbm = [jnp.where(m, cb, 0.0) for m in masks]
            y_off = (jnp.dot(cg, sf_ref[:, gsl[g]].astype(BF16), preferred_element_type=F32)
                     * expand(lhs_exs[u], 2, g)
                     + jnp.dot(cg, sin_ref[cidx, :, gsl[g]], preferred_element_type=F32)
                     * expand(lhs_exs[u], 3, g))
            update_state(sf_ref, cidx, lhs_exs[u], g, 0, 2, L - 1)
            return cbm, y_off

        def main(u, g, cbm):
            xg = ux_ref[c_base + u, :, gsl[g]].astype(BF16)
            blockdiag = jnp.concatenate(
                [jnp.where(head_of_lane == hh, xg, jnp.zeros_like(xg)) for hh in range(SSD_HPG)], axis=0)
            y_diag = None
            for dirn in range(2):
                ws = []
                for hh in range(SSD_HPG):
                    ln = dirn * SSD_HEADS + g * SSD_HPG + hh
                    col = packs[u][:, ln:ln + 1]
                    row = pack_ts[u][ln:ln + 1, :]
                    dtrow = pack_ts[u][2 * SSD_HEADS + ln:2 * SSD_HEADS + ln + 1, :]
                    dec = jnp.exp2(jnp.minimum(col - row, 0.0))
                    ws.append((cbm[dirn] * dec * dtrow).astype(BF16))
                yd = jnp.dot(jnp.concatenate(ws, axis=1), blockdiag, preferred_element_type=F32)
                y_diag = yd if y_diag is None else y_diag + yd
            return y_diag

        def finish(u, g, y_off, y_diag):
            yg = ux_ref[c_base + u, :, gsl[g]] * dsk_ref[:, gsl[g]] + y_diag + y_off
            yz = yg * _silu(z_ref[u * L:(u + 1) * L, gsl[g]])
            y_ref[u * L:(u + 1) * L, gsl[g]] = yz
            return jnp.sum(yz * yz, axis=-1, keepdims=True)

        units = [(u, g) for u in range(SSD_CPS) for g in groups]
        early_ahead = 4
        pending = {units[i]: early(*units[i]) for i in range(early_ahead)}
        diag = {}
        ss = [jnp.zeros((L, 1), F32) for _ in range(SSD_CPS)]
        for i, (u, g) in enumerate(units):
            if i + early_ahead < len(units):
                pending[units[i + early_ahead]] = early(*units[i + early_ahead])
            diag[(u, g)] = main(u, g, pending[(u, g)][0])
            if i >= 1:
                pu, pg = units[i - 1]
                ss[pu] = ss[pu] + finish(pu, pg, pending[(pu, pg)][1], diag[(pu, pg)])
        pu, pg = units[-1]
        ss[pu] = ss[pu] + finish(pu, pg, pending[(pu, pg)][1], diag[(pu, pg)])
        for u in range(SSD_CPS):
            rs = lax.rsqrt(ss[u] * (1.0 / D_SSD) + EPS)
            for g in groups:
                o_ref[u * L:(u + 1) * L, gsl[g]] = (
                    y_ref[u * L:(u + 1) * L, gsl[g]] * rs * ng_ref[:, gsl[g]]).astype(o_ref.dtype)


def _expansion_matrix():
    r = jnp.arange(2 * LANE)[:, None]
    col = jnp.arange(4 * D_SSD)[None, :]
    src = (col // D_SSD) * SSD_HEADS + (col % D_SSD) // SSD_HEAD_DIM
    half = LANE // 2
    return ((r % half == src) & (r < 3 * half)).astype(BF16)


def _ssd(gzx, dtraw, conv_w, conv_b, dt_bias, a_log, d_skip_exp, ssd_norm_g, b, s, nc):
    t_rows = gzx.shape[0]
    span = SSD_CPS * CHUNK
    assert s % span == 0 and nc % span == 0
    n_lat, n_ctx = s // CHUNK, nc // CHUNK
    g_lat, g_ctx = s // span, nc // span
    g_all = g_lat + g_ctx
    halo_per_span = span // SUBLANE
    n_halo = t_rows // SUBLANE
    c0 = lambda t: jnp.where(t < g_all, _ssd_visit_index(t, g_lat, g_ctx), 0)
    c1 = lambda t: jnp.where(t >= g_all + g_ctx, t - g_all - g_ctx, 0)
    blk = lambda bi, c: jnp.where(c < g_lat, bi * g_lat + c, b * g_lat + bi * g_ctx + (c - g_lat))
    xbc_col = (D_ATT + D_SSD) // D_XBC
    z_col = D_ATT // D_SSD
    const = lambda shape: pl.BlockSpec(shape, lambda bi, t: tuple(0 for _ in shape))
    return pl.pallas_call(
        functools.partial(_ssd_kernel, n_lat=n_lat, n_ctx=n_ctx),
        grid=(b, 2 * g_all),
        in_specs=[pl.BlockSpec((span, D_XBC), lambda bi, t: (blk(bi, c0(t)), xbc_col)),
                  pl.BlockSpec((SUBLANE, D_XBC),
                               lambda bi, t: (jnp.maximum(blk(bi, c0(t)) * halo_per_span - 1, 0), xbc_col)),
                  pl.BlockSpec((SUBLANE, D_XBC),
                               lambda bi, t: (jnp.minimum((blk(bi, c0(t)) + 1) * halo_per_span, n_halo - 1),
                                              xbc_col)),
                  pl.BlockSpec((span, LANE), lambda bi, t: (blk(bi, c0(t)), 0)),
                  pl.BlockSpec((span, D_SSD), lambda bi, t: (bi * g_lat + c1(t), z_col)),
                  const((CONV_W, D_XBC)), const((1, D_XBC)), const((1, LANE)), const((1, LANE)),
                  const((1, D_SSD)), const((1, D_SSD)), const((2 * LANE, 4 * D_SSD))],
        out_specs=pl.BlockSpec((span, D_SSD), lambda bi, t: (bi * g_lat + c1(t), 0)),
        out_shape=jax.ShapeDtypeStruct((b * s, D_SSD), BF16),
        scratch_shapes=[pltpu.VMEM((D_STATE, D_SSD), F32),
                        pltpu.VMEM((D_STATE, D_SSD), F32),
                        pltpu.VMEM((n_lat, D_STATE, D_SSD), BF16),
                        pltpu.VMEM((n_lat + n_ctx, CHUNK, D_SSD), F32),
                        pltpu.VMEM((n_lat + n_ctx, CHUNK, 2 * D_BC), BF16),
                        pltpu.VMEM((n_lat + n_ctx, CHUNK, 2 * LANE), BF16),
                        pltpu.VMEM((n_lat + n_ctx, CHUNK, LANE), F32),
                        pltpu.VMEM((span, D_SSD), F32),
                        pltpu.VMEM((D_XBC // LANE, span + 2 * SUBLANE, LANE), F32)],
        compiler_params=_params("bi_ssd", ("arbitrary", "arbitrary")),
        name="bi_ssd",
    )(gzx, gzx, gzx, dtraw, gzx, conv_w, conv_b, dt_bias, a_log, d_skip_exp, ssd_norm_g,
      _expansion_matrix())


def _out_kernel(ha_ref, hs_ref, wa_ref, ws_ref, x_ref, gate_ref, ng_ref, o_ref):
    upd = (jnp.dot(ha_ref[...], wa_ref[...], preferred_element_type=F32)
           + jnp.dot(hs_ref[...], ws_ref[...], preferred_element_type=F32))
    y = x_ref[...] + gate_ref[0] * upd
    ms = jnp.mean(y * y, axis=-1, keepdims=True)
    o_ref[...] = (y * lax.rsqrt(ms + EPS) * ng_ref[...]).astype(o_ref.dtype)


def _out_proj(h_att, h_ssd, w_out_bf, x2d, gate, final_norm_g, s):
    m, d = x2d.shape
    tm = _row_tile(s, OUT_ROWS)
    tiles_per_batch = s // tm
    return pl.pallas_call(
        _out_kernel,
        grid=(m // tm,),
        in_specs=[pl.BlockSpec((tm, D_ATT), lambda i: (i, 0)),
                  pl.BlockSpec((tm, D_SSD), lambda i: (i, 0)),
                  pl.BlockSpec((D_ATT, d), lambda i: (0, 0)),
                  pl.BlockSpec((D_SSD, d), lambda i: (D_ATT // D_SSD, 0)),
                  pl.BlockSpec((tm, d), lambda i: (i, 0)),
                  pl.BlockSpec((1, 1, d), lambda i: (i // tiles_per_batch, 0, 0)),
                  pl.BlockSpec((1, d), lambda i: (0, 0))],
        out_specs=pl.BlockSpec((tm, d), lambda i: (i, 0)),
        out_shape=jax.ShapeDtypeStruct((m, d), x2d.dtype),
        compiler_params=_params("out_proj", ("parallel",)),
        name="out_proj",
    )(h_att, h_ssd, w_out_bf, w_out_bf, x2d, gate, final_norm_g)


def _rope_tables(s, pad_rows):
    pos = np.arange(s)
    row = (pos // GRID_W).astype(np.float32)
    colp = (pos % GRID_W).astype(np.float32)
    quarter = HEAD_DIM // 4
    freq = (1.0 / (np.float32(ROPE_THETA) ** (np.arange(quarter, dtype=np.float32) / np.float32(quarter)))
            ).astype(np.float32)
    lane = np.arange(LANE)
    dim = lane % HEAD_DIM
    use_col = (dim // (HEAD_DIM // 2)) == 1
    r = dim % (HEAD_DIM // 2)
    first = r < quarter
    ang = (np.where(use_col[None, :], colp[:, None], row[:, None]) * freq[r % quarter][None, :]).astype(np.float32)
    cos, sin = np.cos(ang), np.sin(ang)
    sa = np.where(first[None, :], -sin, 0.0)
    sb = np.where(first[None, :], 0.0, sin)
    ident = lambda v, fill: jnp.asarray(
        np.concatenate([v, np.full((pad_rows, LANE), fill)], axis=0).astype(np.float32))
    return ident(cos, 1.0), ident(sa, 0.0), ident(sb, 0.0)


def kernel(x, c, ctx, c_ctx, w_mod, b_mod, norm_g, w_in, conv_w, conv_b, a_log_f, a_log_b, dt_bias_f,
           dt_bias_b, d_skip, att_norm_g, ssd_norm_g, sink, w_out, final_norm_g):
    assert w_mod.shape[0] == 1, "single-layer operation"
    b, s, d = x.shape
    nc = ctx.shape[1]

    pad_rows = (-(b + 1)) % SUBLANE
    cc = jnp.concatenate([c, c_ctx[None, :], jnp.zeros((pad_rows, d), F32)], axis=0)
    mod = _modulation(cc, w_mod[0], b_mod)
    shift = mod[:b + 1, :d].reshape(b + 1, 1, d)
    scale = mod[:b + 1, d:2 * d].reshape(b + 1, 1, d)
    gate = mod[:b, 2 * d:].reshape(b, 1, d)

    w_t = jnp.swapaxes(w_in[0], 0, 1).astype(BF16)
    w_dt = jnp.pad(w_t[D_QKV + D_GZX:], ((0, LANE - 2 * SSD_HEADS), (0, 0)))
    w_out_bf = w_out[0].astype(BF16)

    cos_t, sa_t, sb_t = _rope_tables(s, NORM_TILE)
    qkv, xn, dtraw = _norm_qkv(x.reshape(b * s, d), ctx.reshape(b * nc, d), shift, scale, norm_g, w_dt, w_t,
                               cos_t, sa_t, sb_t, s)

    gzx = _matmul(xn, w_t, D_QKV, D_GZX, F32)

    h_att = _attention(sink[0], qkv, gzx, att_norm_g, b, s, nc)

    lane_pad = LANE - 2 * SSD_HEADS
    dt_bias = jnp.pad(jnp.concatenate([dt_bias_f[0], dt_bias_b[0]]), (0, lane_pad))[None, :]
    a_log = jnp.pad(jnp.concatenate([a_log_f[0], a_log_b[0]]), (0, lane_pad))[None, :]
    d_skip_exp = jnp.repeat(d_skip[0], SSD_HEAD_DIM)[None, :]
    h_ssd = _ssd(gzx, dtraw, conv_w[0], conv_b, dt_bias, a_log, d_skip_exp, ssd_norm_g, b, s, nc)

    out = _out_proj(h_att, h_ssd, w_out_bf, x.reshape(b * s, d), gate, final_norm_g[None, :], s)
    return out.reshape(b, s, d)
```

```python
import functools

import jax
import jax.numpy as jnp
import numpy as np
from jax import lax
from jax.experimental import pallas as pl
from jax.experimental.pallas import tpu as pltpu

F32 = jnp.float32
BF16 = jnp.bfloat16

EPS = 1e-6
GRID_W = 64
ROPE_THETA = 10000.0
ATT_HEADS = 16
ATT_KV_HEADS = 4
HEAD_DIM = 64
D_ATT = ATT_HEADS * HEAD_DIM
D_KV = ATT_KV_HEADS * HEAD_DIM
Q_PER_KV = ATT_HEADS // ATT_KV_HEADS
BLOCK = 128
SSD_HEADS = 16
SSD_HEAD_DIM = 64
D_SSD = SSD_HEADS * SSD_HEAD_DIM
SSD_GROUPS = 4
SSD_HPG = SSD_HEADS // SSD_GROUPS
D_STATE = 128
D_BC = SSD_GROUPS * D_STATE
D_XBC = D_SSD + 2 * D_BC
CONV_W = 3
CHUNK = 128
D_QKV = D_ATT + 2 * D_KV
D_GZX = D_ATT + D_SSD + D_XBC

LANE = 128
SUBLANE = 8
V7X_VMEM_BYTES = 64 * 1024 * 1024
MIB = 1024 * 1024

MOD_COLS = 768
NORM_TILE = 512
GZX_COLS = D_XBC
ATT_QBLOCKS = 4
SSD_CPS = 2
OUT_ROWS = (512, 256, 128)
VMEM_MIB = dict(modulation=32, norm_qkv=56, gzx_proj=56, window_attn=40, bi_ssd=60, out_proj=48)
assert max(VMEM_MIB.values()) * MIB < V7X_VMEM_BYTES

NEG = -1e30
LOG2E = 1.4426950408889634


def _silu(v):
    return v / (1.0 + jnp.exp(-v))


def _dot_nt(a, w):
    return lax.dot_general(a, w, (((1,), (1,)), ((), ())), preferred_element_type=F32)


def _params(name, semantics):
    return pltpu.CompilerParams(dimension_semantics=semantics, vmem_limit_bytes=VMEM_MIB[name] * MIB)


def _row_tile(m, candidates=(1024, 512, 256, 128)):
    for t in candidates:
        if m % t == 0:
            return t
    raise ValueError(f"row count {m} not tileable")


def _mod_kernel(c_ref, w_ref, b_ref, o_ref):
    a = _silu(c_ref[...]).astype(BF16)
    o_ref[...] = jnp.dot(a, w_ref[...].astype(BF16), preferred_element_type=F32) + b_ref[...]


def _modulation(cc, w_mod, b_mod):
    rows, d = cc.shape
    n = w_mod.shape[1]
    tn = MOD_COLS
    assert n % tn == 0
    return pl.pallas_call(
        _mod_kernel,
        grid=(n // tn,),
        in_specs=[pl.BlockSpec((rows, d), lambda j: (0, 0)),
                  pl.BlockSpec((d, tn), lambda j: (0, j)),
                  pl.BlockSpec((1, tn), lambda j: (0, j))],
        out_specs=pl.BlockSpec((rows, tn), lambda j: (0, j)),
        out_shape=jax.ShapeDtypeStruct((rows, n), F32),
        compiler_params=_params("modulation", ("parallel",)),
        name="modulation",
    )(cc, w_mod, b_mod)


def _norm_qkv_kernel(x_ref, ctx_ref, shift_ref, scale_ref, g_ref, wdt_ref, w_ref, cos_ref, sa_ref, sb_ref,
                     qkv_ref, xn_ref, dt_ref, ya_ref, yb_ref, *, n_lat_tiles, n_tiles, n_rope, n_q):
    i = pl.program_id(0)
    is_lat = jnp.minimum(i, n_tiles - 1) < n_lat_tiles

    @pl.when(i == 0)
    def _():
        yb_ref[...] = jnp.zeros_like(yb_ref)

    def step(y_prev_ref, y_next_ref):
        y = y_prev_ref[...]
        dt_ref[...] = _dot_nt(y, wdt_ref[...])
        acc = _dot_nt(y, w_ref[...])
        cos, sa, sb = cos_ref[...], sa_ref[...], sb_ref[...]
        quarter = HEAD_DIM // 4
        for jb in range(acc.shape[1] // LANE):
            blk = acc[:, jb * LANE:(jb + 1) * LANE]
            if jb < n_rope:
                blk = (blk * cos + pltpu.roll(blk, LANE - quarter, 1) * sa
                       + pltpu.roll(blk, quarter, 1) * sb)
            if jb < n_q:
                blk = blk * (HEAD_DIM ** -0.5 * LOG2E)
            qkv_ref[:, jb * LANE:(jb + 1) * LANE] = blk.astype(qkv_ref.dtype)

        v = jnp.where(is_lat, x_ref[...], ctx_ref[...])
        ms = jnp.mean(v * v, axis=-1, keepdims=True)
        gain = g_ref[...] * (1.0 + scale_ref[0])
        yn = (v * lax.rsqrt(ms + EPS) * gain + shift_ref[0]).astype(BF16)
        xn_ref[...] = yn
        y_next_ref[...] = yn

    @pl.when(i % 2 == 0)
    def _():
        step(yb_ref, ya_ref)

    @pl.when(i % 2 == 1)
    def _():
        step(ya_ref, yb_ref)


def _norm_qkv(x2d, ctx2d, shift, scale, norm_g, w_dt, w_t, cos_t, sa_t, sb_t, s):
    m_lat, d = x2d.shape
    m_ctx = ctx2d.shape[0]
    b = m_lat // s
    tr = NORM_TILE
    assert s % tr == 0 and m_ctx % tr == 0
    tiles_per_seq = s // tr
    n_lat_tiles, n_ctx_tiles = m_lat // tr, m_ctx // tr
    n_tiles = n_lat_tiles + n_ctx_tiles
    cur = lambda i: jnp.minimum(i, n_tiles - 1)
    prev = lambda i: jnp.maximum(i - 1, 0)
    mod_idx = lambda i: (jnp.where(cur(i) < n_lat_tiles, cur(i) // tiles_per_seq, b), 0, 0)
    tab = pl.BlockSpec((tr, LANE), lambda i: (jnp.where(prev(i) < n_lat_tiles, prev(i) % tiles_per_seq,
                                                        tiles_per_seq), 0))
    out_prev = lambda width: pl.BlockSpec((tr, width), lambda i: (prev(i), 0))
    const = lambda shape: pl.BlockSpec(shape, lambda i: tuple(0 for _ in shape))
    return pl.pallas_call(
        functools.partial(_norm_qkv_kernel, n_lat_tiles=n_lat_tiles, n_tiles=n_tiles,
                          n_rope=(D_ATT + D_KV) // LANE, n_q=D_ATT // LANE),
        grid=(n_tiles + 1,),
        in_specs=[pl.BlockSpec((tr, d), lambda i: (jnp.minimum(cur(i), n_lat_tiles - 1), 0)),
                  pl.BlockSpec((tr, d), lambda i: (jnp.maximum(cur(i) - n_lat_tiles, 0), 0)),
                  pl.BlockSpec((1, 1, d), mod_idx),
                  pl.BlockSpec((1, 1, d), mod_idx),
                  const((1, d)), const((LANE, d)), const((D_QKV, d)),
                  tab, tab, tab],
        out_specs=[out_prev(D_QKV),
                   pl.BlockSpec((tr, d), lambda i: (cur(i), 0)),
                   out_prev(LANE)],
        out_shape=[jax.ShapeDtypeStruct((m_lat + m_ctx, D_QKV), BF16),
                   jax.ShapeDtypeStruct((m_lat + m_ctx, d), BF16),
                   jax.ShapeDtypeStruct((m_lat + m_ctx, LANE), F32)],
        scratch_shapes=[pltpu.VMEM((tr, d), BF16), pltpu.VMEM((tr, d), BF16)],
        compiler_params=_params("norm_qkv", ("arbitrary",)),
        name="norm_qkv",
    )(x2d, ctx2d, shift, scale, norm_g, w_dt, w_t, cos_t, sa_t, sb_t)


def _mm_kernel(x_ref, w_ref, o_ref):
    o_ref[...] = _dot_nt(x_ref[...], w_ref[...]).astype(o_ref.dtype)


def _gzx_proj(x2d, w_t, lat_rows):
    m, d = x2d.shape
    tm = _row_tile(m)
    assert D_ATT + D_SSD == GZX_COLS and D_XBC == GZX_COLS and lat_rows % tm == 0 and D_QKV % LANE == 0
    n_lat = lat_rows // tm
    return pl.pallas_call(
        _mm_kernel,
        grid=(n_lat + m // tm,),
        in_specs=[pl.BlockSpec((tm, d), lambda t: (jnp.where(t < n_lat, t, t - n_lat), 0)),
                  pl.BlockSpec((pl.Element(GZX_COLS), pl.Element(d)),
                               lambda t: (pl.multiple_of(D_QKV + jnp.where(t < n_lat, 0, GZX_COLS), LANE), 0))],
        out_specs=pl.BlockSpec((tm, GZX_COLS), lambda t: (t, 0)),
        out_shape=jax.ShapeDtypeStruct((lat_rows + m, GZX_COLS), F32),
        compiler_params=_params("gzx_proj", ("parallel",)),
        name="gzx_proj",
    )(x2d, w_t)


def _attn_kernel(sink_ref, q_ref, kp_ref, ko_ref, kn_ref, kc_ref, vp_ref, vo_ref, vn_ref, vc_ref,
                 g_ref, ng_ref, o_ref, *, n_steps):
    n = pl.program_id(1)
    nctx = kc_ref.shape[0]
    nk = 3 * BLOCK + nctx
    cols_g = Q_PER_KV * BLOCK
    k_blocks = [kp_ref[...]] + [ko_ref[u * BLOCK:(u + 1) * BLOCK] for u in range(ATT_QBLOCKS)] + [kn_ref[...]]
    v_blocks = [vp_ref[...]] + [vo_ref[u * BLOCK:(u + 1) * BLOCK] for u in range(ATT_QBLOCKS)] + [vn_ref[...]]

    kj = lax.broadcasted_iota(jnp.int32, (BLOCK, BLOCK), 0)
    qi = lax.broadcasted_iota(jnp.int32, (BLOCK, BLOCK), 1)
    tile = lambda v: jnp.concatenate([v] * Q_PER_KV, axis=1)
    tri_prev = tile(jnp.where(kj >= qi, 0.0, NEG).astype(F32))
    tri_next = tile(jnp.where(kj <= qi, 0.0, NEG).astype(F32))
    edge_prev = tile(jnp.where((kj >= qi) & (n > 0), 0.0, NEG).astype(F32))
    edge_next = tile(jnp.where((kj <= qi) & (n < n_steps - 1), 0.0, NEG).astype(F32))
    head_of_col = lax.broadcasted_iota(jnp.int32, (1, cols_g), 1) // BLOCK
    n_loc = (ATT_QBLOCKS + 2) * BLOCK
    v_t = jnp.concatenate(v_blocks + [vc_ref[...]], axis=0).T
    tail_rows = 2 * SUBLANE
    ones_tail = jnp.where(lax.broadcasted_iota(jnp.int32, (tail_rows, nk), 0) == 0, 1.0, 0.0).astype(BF16)

    def scores_t(u, kh):
        hs = slice(kh * HEAD_DIM, (kh + 1) * HEAD_DIM)
        k_h = jnp.concatenate([blk[:, hs] for blk in k_blocks[u:u + 3]] + [kc_ref[:, hs]], axis=0)
        qg = jnp.concatenate(
            [q_ref[u * BLOCK:(u + 1) * BLOCK, (kh * Q_PER_KV + i) * HEAD_DIM:(kh * Q_PER_KV + i + 1) * HEAD_DIM]
             for i in range(Q_PER_KV)], axis=0)
        return lax.dot_general(k_h, qg, (((1,), (1,)), ((), ())), preferred_element_type=F32)

    def attend(u, kh, st):
        bias_prev = edge_prev if u == 0 else tri_prev
        bias_next = edge_next if u == ATT_QBLOCKS - 1 else tri_next
        st = jnp.concatenate([st[:BLOCK] + bias_prev, st[BLOCK:2 * BLOCK],
                              st[2 * BLOCK:3 * BLOCK] + bias_next, st[3 * BLOCK:]], axis=0)
        sink_row = jnp.zeros((1, cols_g), F32)
        for i in range(Q_PER_KV):
            sink_row = jnp.where(head_of_col == i, sink_ref[kh * Q_PER_KV + i] * LOG2E, sink_row)
        m = jnp.maximum(jnp.max(st, axis=0, keepdims=True), sink_row)
        pt = jnp.exp2(st - m).astype(BF16)
        hs = slice(kh * HEAD_DIM, (kh + 1) * HEAD_DIM)
        vt_h = jnp.concatenate([v_t[hs, u * BLOCK:(u + 3) * BLOCK], v_t[hs, n_loc:]], axis=1)
        ot = jnp.dot(jnp.concatenate([vt_h, ones_tail], axis=0), pt, preferred_element_type=F32)
        inv = 1.0 / (ot[HEAD_DIM:HEAD_DIM + 1] + jnp.exp2(sink_row - m))
        on = ot[:HEAD_DIM] * inv
        pairs = [jnp.concatenate([on[:, (2 * j) * BLOCK:(2 * j + 1) * BLOCK],
                                  on[:, (2 * j + 1) * BLOCK:(2 * j + 2) * BLOCK]], axis=0).T
                 for j in range(Q_PER_KV // 2)]
        return jnp.concatenate(pairs, axis=1)

    def finish(u, groups):
        rows = slice(u * BLOCK, (u + 1) * BLOCK)
        a = jnp.concatenate(groups, axis=1) * _silu(g_ref[rows])
        ms = jnp.mean(a * a, axis=-1, keepdims=True)
        o_ref[rows] = (a * lax.rsqrt(ms + EPS) * ng_ref[...]).astype(o_ref.dtype)

    order = [(u, kh) for u in range(ATT_QBLOCKS) for kh in range(ATT_KV_HEADS)]
    ahead = 1
    pending = [scores_t(*order[i]) for i in range(ahead)]
    groups = []
    for idx, (u, kh) in enumerate(order):
        s_cur = pending.pop(0)
        if idx + ahead < len(order):
            pending.append(scores_t(*order[idx + ahead]))
        groups.append(attend(u, kh, s_cur))
        if kh == ATT_KV_HEADS - 1:
            finish(u, groups)
            groups = []


def _attention(sink, qkv, gzx, att_norm_g, b, s, nc):
    rows = ATT_QBLOCKS * BLOCK
    assert s % rows == 0 and (b * s) % nc == 0
    nb = s // BLOCK
    n_steps = s // rows
    kcol, vcol = D_ATT // D_KV, D_ATT // D_KV + 1
    before = lambda col: pl.BlockSpec(
        (BLOCK, D_KV), lambda bi, n: (bi * nb + jnp.maximum(ATT_QBLOCKS * n - 1, 0), col))
    own = lambda col: pl.BlockSpec((rows, D_KV), lambda bi, n: (bi * n_steps + n, col))
    after = lambda col: pl.BlockSpec(
        (BLOCK, D_KV), lambda bi, n: (bi * nb + jnp.minimum(ATT_QBLOCKS * (n + 1), nb - 1), col))
    ctx_spec = lambda col: pl.BlockSpec((nc, D_KV), lambda bi, n: (b * s // nc + bi, col))
    q_rows = pl.BlockSpec((rows, D_ATT), lambda bi, n: (bi * n_steps + n, 0))
    return pl.pallas_call(
        functools.partial(_attn_kernel, n_steps=n_steps),
        grid=(b, n_steps),
        in_specs=[pl.BlockSpec(memory_space=pltpu.SMEM),
                  q_rows,
                  before(kcol), own(kcol), after(kcol), ctx_spec(kcol),
                  before(vcol), own(vcol), after(vcol), ctx_spec(vcol),
                  q_rows,
                  pl.BlockSpec((1, D_ATT), lambda bi, n: (0, 0))],
        out_specs=q_rows,
        out_shape=jax.ShapeDtypeStruct((b * s, D_ATT), BF16),
        compiler_params=_params("window_attn", ("parallel", "parallel")),
        name="window_attn",
    )(sink, qkv, qkv, qkv, qkv, qkv, qkv, qkv, qkv, qkv, gzx, att_norm_g)


def _ssd_visit_index(t, n_lat, n_ctx):
    n_all = n_lat + n_ctx
    k = t - n_all
    fwd = jnp.where(k < n_ctx, n_lat + k, k - n_ctx)
    return jnp.where(t >= n_all, fwd, n_all - 1 - t)


def _ssd_kernel(xbc_ref, hp_ref, hn_ref, dt_ref, z_ref, cw_ref, cb_ref, dtb_ref, alog_ref, dsk_ref,
                ng_ref, ex_ref, o_ref, sf_ref, sb_ref, sin_ref, ux_ref, ubc_ref, lhs_ref, pack_ref, y_ref, pad_ref,
                *, n_lat, n_ctx):
    t = pl.program_id(1)
    n_all = n_lat + n_ctx
    g_lat, g_ctx = n_lat // SSD_CPS, n_ctx // SSD_CPS
    g_all = g_lat + g_ctx
    gidx = _ssd_visit_index(t, g_lat, g_ctx)
    c_base = gidx * SSD_CPS
    sweep1 = t >= g_all
    is_lat = gidx < g_lat
    L = CHUNK
    gw = SSD_HPG * SSD_HEAD_DIM
    lane = lax.broadcasted_iota(jnp.int32, (1, LANE), 1)
    rows = lax.broadcasted_iota(jnp.int32, (L, 1), 0)
    groups = range(SSD_GROUPS)
    gsl = [slice(g * gw, (g + 1) * gw) for g in groups]

    @pl.when(t == 0)
    def _():
        sf_ref[...] = jnp.zeros_like(sf_ref)
        sb_ref[...] = jnp.zeros_like(sb_ref)
        ux_ref[n_all - 1] = jnp.zeros((CHUNK, D_SSD), F32)
        ubc_ref[n_all - 1] = jnp.zeros((CHUNK, 2 * D_BC), BF16)
        lhs_ref[n_all - 1] = jnp.zeros((CHUNK, 2 * LANE), BF16)

    def expand(lhs, block, g):
        c0 = block * D_SSD + g * gw
        return jnp.dot(lhs, ex_ref[:, c0:c0 + gw], preferred_element_type=F32)

    def update_state(state_ref, chunk, lhs_ex, g, wblock, eblock, end_row):
        r8 = (end_row // SUBLANE) * SUBLANE
        xw = (ux_ref[chunk, :, gsl[g]] * expand(lhs_ex, wblock, g)).astype(BF16)
        bg = ubc_ref[chunk, :, g * D_STATE:(g + 1) * D_STATE]
        loc = lax.dot_general(bg, xw, (((0,), (0,)), ((), ())), preferred_element_type=F32)
        tot = expand(lhs_ex[r8:r8 + SUBLANE], eblock, g)[end_row - r8:end_row - r8 + 1]
        new = state_ref[:, gsl[g]] * tot + loc
        state_ref[:, gsl[g]] = new
        return new

    @pl.when(jnp.logical_not(sweep1))
    def _():
        is_first = (gidx == 0) | (gidx == g_lat)
        is_last = (gidx == g_lat - 1) | (gidx == g_all - 1)
        span = SSD_CPS * L

        for k in range(D_XBC // LANE):
            lanes = slice(k * LANE, (k + 1) * LANE)
            pad_ref[k, SUBLANE:SUBLANE + span, :] = xbc_ref[:, lanes]
            pad_ref[k, SUBLANE - 1:SUBLANE, :] = jnp.where(is_first, 0.0, hp_ref[SUBLANE - 1:SUBLANE, lanes])
            pad_ref[k, SUBLANE + span:SUBLANE + span + 1, :] = jnp.where(is_last, 0.0, hn_ref[0:1, lanes])

        def conv_strip(u, j):
            r0 = SUBLANE + u * L
            for k in range(j * gw // LANE, (j + 1) * gw // LANE):
                lanes = slice(k * LANE, (k + 1) * LANE)
                v = (pad_ref[k, r0 - 1:r0 - 1 + L, :] * cw_ref[0:1, lanes] + pad_ref[k, r0:r0 + L, :] * cw_ref[1:2, lanes]
                     + pad_ref[k, r0 + 1:r0 + 1 + L, :] * cw_ref[2:3, lanes] + cb_ref[:, lanes])
                v = _silu(v)
                if k * LANE < D_SSD:
                    ux_ref[c_base + u, :, lanes] = v
                else:
                    ubc_ref[c_base + u, :, k * LANE - D_SSD:(k + 1) * LANE - D_SSD] = v.astype(BF16)

        def decay_tables(u):
            dtr = dt_ref[u * L:(u + 1) * L, :] + dtb_ref[...]
            dt = jnp.maximum(dtr, 0.0) + jnp.log(1.0 + jnp.exp(-jnp.abs(dtr)))
            a = dt * (-jnp.exp(alog_ref[...]))
            pre, suf = a, a
            d = 1
            while d < L:
                pre = pre + jnp.where(rows >= d, pltpu.roll(pre, d, 0), 0.0)
                suf = suf + jnp.where(rows < L - d, pltpu.roll(suf, L - d, 0), 0.0)
                d *= 2
            is_fwd_lane = lane < SSD_HEADS
            cs = jnp.where(is_fwd_lane, pre, suf)
            tot = jnp.where(is_fwd_lane, cs[L - 1:L, :], cs[0:1, :])
            e = jnp.exp(cs)
            wdt = jnp.exp(tot - cs) * dt
            pack_ref[c_base + u] = jnp.where(lane < 2 * SSD_HEADS, cs * LOG2E, pltpu.roll(dt, 2 * SSD_HEADS, 1))
            fac = jnp.where(lane < 2 * SSD_HEADS, wdt,
                            jnp.where(lane < 4 * SSD_HEADS, pltpu.roll(e, 2 * SSD_HEADS, 1), 0.0))
            hi = fac.astype(BF16).astype(F32)
            r1 = fac - hi
            mid = r1.astype(BF16).astype(F32)
            lo = (r1 - mid).astype(BF16)
            lhs_ref[c_base + u] = jnp.concatenate([(hi + pltpu.roll(mid, LANE // 2, 1)).astype(BF16), lo], axis=1)

        for u in reversed(range(SSD_CPS)):
            cidx = c_base + u
            done = jnp.minimum(cidx + 1, n_all - 1)
            lhs_done = lhs_ref[done]
            wexp = [expand(lhs_done, 1, g) for g in groups]
            tots = [expand(lhs_done[0:SUBLANE], 3, g)[0:1] for g in groups]
            conv_strip(u, 0)
            conv_strip(u, 1)
            xws = [(ux_ref[done, :, gsl[g]] * wexp[g]).astype(BF16) for g in groups]
            locs = [lax.dot_general(ubc_ref[done, :, g * D_STATE:(g + 1) * D_STATE], xws[g],
                                    (((0,), (0,)), ((), ())), preferred_element_type=F32) for g in groups]
            conv_strip(u, 2)
            conv_strip(u, 3)
            conv_strip(u, 4)
            for g in groups:
                new = sb_ref[:, gsl[g]] * tots[g] + locs[g]
                sb_ref[:, gsl[g]] = new
                sin_ref[jnp.minimum(cidx, n_lat - 1), :, gsl[g]] = new.astype(BF16)
            conv_strip(u, 5)
            conv_strip(u, 6)
            conv_strip(u, 7)
            decay_tables(u)

    @pl.when(sweep1 & jnp.logical_not(is_lat))
    def _():
        for u in range(SSD_CPS):
            lhs_ex = lhs_ref[c_base + u]
            for g in groups:
                update_state(sf_ref, c_base + u, lhs_ex, g, 0, 2, L - 1)

    @pl.when(sweep1 & is_lat)
    def _():
        li = lax.broadcasted_iota(jnp.int32, (L, L), 0)
        si = lax.broadcasted_iota(jnp.int32, (L, L), 1)
        masks = [si <= li, si >= li]
        head_of_lane = lax.broadcasted_iota(jnp.int32, (1, gw), 1) // SSD_HEAD_DIM
        lhs_exs = [lhs_ref[c_base + u] for u in range(SSD_CPS)]
        packs = [pack_ref[c_base + u] for u in range(SSD_CPS)]
        pack_ts = [p.T for p in packs]

        def early(u, g):
            cidx = c_base + u
            bg = ubc_ref[cidx, :, g * D_STATE:(g + 1) * D_STATE]
            cg = ubc_ref[cidx, :, D_BC + g * D_STATE:D_BC + (g + 1) * D_STATE]
            cb = lax.dot_general(cg, bg, (((1,), (1,)), ((), ())), preferred_element_type=F32)
            cbm = [jnp.where(m, cb, 0.0) for m in masks]
            y_off = (jnp.dot(cg, sf_ref[:, gsl[g]].astype(BF16), preferred_element_type=F32)
                     * expand(lhs_exs[u], 2, g)
                     + jnp.dot(cg, sin_ref[cidx, :, gsl[g]], preferred_element_type=F32)
                     * expand(lhs_exs[u], 3, g))
            update_state(sf_ref, cidx, lhs_exs[u], g, 0, 2, L - 1)
            return cbm, y_off

        def main(u, g, cbm):
            xg = ux_ref[c_base + u, :, gsl[g]].astype(BF16)
            blockdiag = jnp.concatenate(
                [jnp.where(head_of_lane == hh, xg, jnp.zeros_like(xg)) for hh in range(SSD_HPG)], axis=0)
            y_diag = None
            for dirn in range(2):
                ws = []
                for hh in range(SSD_HPG):
                    ln = dirn * SSD_HEADS + g * SSD_HPG + hh
                    col = packs[u][:, ln:ln + 1]
                    row = pack_ts[u][ln:ln + 1, :]
                    dtrow = pack_ts[u][2 * SSD_HEADS + ln:2 * SSD_HEADS + ln + 1, :]
                    dec = jnp.exp2(jnp.minimum(col - row, 0.0))
                    ws.append((cbm[dirn] * dec * dtrow).astype(BF16))
                yd = jnp.dot(jnp.concatenate(ws, axis=1), blockdiag, preferred_element_type=F32)
                y_diag = yd if y_diag is None else y_diag + yd
            return y_diag

        def finish(u, g, y_off, y_diag):
            yg = ux_ref[c_base + u, :, gsl[g]] * dsk_ref[:, gsl[g]] + y_diag + y_off
            yz = yg * _silu(z_ref[u * L:(u + 1) * L, gsl[g]])
            y_ref[u * L:(u + 1) * L, gsl[g]] = yz
            return jnp.sum(yz * yz, axis=-1, keepdims=True)

        units = [(u, g) for u in range(SSD_CPS) for g in groups]
        early_ahead = 4
        pending = {units[i]: early(*units[i]) for i in range(early_ahead)}
        diag = {}
        ss = [jnp.zeros((L, 1), F32) for _ in range(SSD_CPS)]
        for i, (u, g) in enumerate(units):
            if i + early_ahead < len(units):
                pending[units[i + early_ahead]] = early(*units[i + early_ahead])
            diag[(u, g)] = main(u, g, pending[(u, g)][0])
            if i >= 1:
                pu, pg = units[i - 1]
                ss[pu] = ss[pu] + finish(pu, pg, pending[(pu, pg)][1], diag[(pu, pg)])
        pu, pg = units[-1]
        ss[pu] = ss[pu] + finish(pu, pg, pending[(pu, pg)][1], diag[(pu, pg)])
        for u in range(SSD_CPS):
            rs = lax.rsqrt(ss[u] * (1.0 / D_SSD) + EPS)
            for g in groups:
                o_ref[u * L:(u + 1) * L, gsl[g]] = (
                    y_ref[u * L:(u + 1) * L, gsl[g]] * rs * ng_ref[:, gsl[g]]).astype(o_ref.dtype)


def _expansion_matrix():
    r = jnp.arange(2 * LANE)[:, None]
    col = jnp.arange(4 * D_SSD)[None, :]
    src = (col // D_SSD) * SSD_HEADS + (col % D_SSD) // SSD_HEAD_DIM
    half = LANE // 2
    return ((r % half == src) & (r < 3 * half)).astype(BF16)


def _ssd(gzx, dtraw, conv_w, conv_b, dt_bias, a_log, d_skip_exp, ssd_norm_g, b, s, nc):
    span = SSD_CPS * CHUNK
    assert s % span == 0 and nc % span == 0
    n_lat, n_ctx = s // CHUNK, nc // CHUNK
    g_lat, g_ctx = s // span, nc // span
    g_all = g_lat + g_ctx
    halo_per_span = span // SUBLANE
    n_halo = gzx.shape[0] // SUBLANE
    xbc0 = b * g_lat
    c0 = lambda t: jnp.where(t < g_all, _ssd_visit_index(t, g_lat, g_ctx), 0)
    c1 = lambda t: jnp.where(t >= g_all + g_ctx, t - g_all - g_ctx, 0)
    blk = lambda bi, c: jnp.where(c < g_lat, bi * g_lat + c, b * g_lat + bi * g_ctx + (c - g_lat))
    xblk = lambda bi, t: xbc0 + blk(bi, c0(t))
    z_col = D_ATT // D_SSD
    const = lambda shape: pl.BlockSpec(shape, lambda bi, t: tuple(0 for _ in shape))
    return pl.pallas_call(
        functools.partial(_ssd_kernel, n_lat=n_lat, n_ctx=n_ctx),
        grid=(b, 2 * g_all),
        in_specs=[pl.BlockSpec((span, D_XBC), lambda bi, t: (xblk(bi, t), 0)),
                  pl.BlockSpec((SUBLANE, D_XBC), lambda bi, t: (xblk(bi, t) * halo_per_span - 1, 0)),
                  pl.BlockSpec((SUBLANE, D_XBC),
                               lambda bi, t: (jnp.minimum((xblk(bi, t) + 1) * halo_per_span, n_halo - 1), 0)),
                  pl.BlockSpec((span, LANE), lambda bi, t: (blk(bi, c0(t)), 0)),
                  pl.BlockSpec((span, D_SSD), lambda bi, t: (bi * g_lat + c1(t), z_col)),
                  const((CONV_W, D_XBC)), const((1, D_XBC)), const((1, LANE)), const((1, LANE)),
                  const((1, D_SSD)), const((1, D_SSD)), const((2 * LANE, 4 * D_SSD))],
        out_specs=pl.BlockSpec((span, D_SSD), lambda bi, t: (bi * g_lat + c1(t), 0)),
        out_shape=jax.ShapeDtypeStruct((b * s, D_SSD), BF16),
        scratch_shapes=[pltpu.VMEM((D_STATE, D_SSD), F32),
                        pltpu.VMEM((D_STATE, D_SSD), F32),
                        pltpu.VMEM((n_lat, D_STATE, D_SSD), BF16),
                        pltpu.VMEM((n_lat + n_ctx, CHUNK, D_SSD), F32),
                        pltpu.VMEM((n_lat + n_ctx, CHUNK, 2 * D_BC), BF16),
                        pltpu.VMEM((n_lat + n_ctx, CHUNK, 2 * LANE), BF16),
                        pltpu.VMEM((n_lat + n_ctx, CHUNK, LANE), F32),
                        pltpu.VMEM((span, D_SSD), F32),
                        pltpu.VMEM((D_XBC // LANE, span + 2 * SUBLANE, LANE), F32)],
        compiler_params=_params("bi_ssd", ("arbitrary", "arbitrary")),
        name="bi_ssd",
    )(gzx, gzx, gzx, dtraw, gzx, conv_w, conv_b, dt_bias, a_log, d_skip_exp, ssd_norm_g,
      _expansion_matrix())


def _out_kernel(ha_ref, hs_ref, wa_ref, ws_ref, x_ref, gate_ref, ng_ref, o_ref):
    upd = (jnp.dot(ha_ref[...], wa_ref[...], preferred_element_type=F32)
           + jnp.dot(hs_ref[...], ws_ref[...], preferred_element_type=F32))
    y = x_ref[...] + gate_ref[0] * upd
    ms = jnp.mean(y * y, axis=-1, keepdims=True)
    o_ref[...] = (y * lax.rsqrt(ms + EPS) * ng_ref[...]).astype(o_ref.dtype)


def _out_proj(h_att, h_ssd, w_out_bf, x2d, gate, final_norm_g, s):
    m, d = x2d.shape
    tm = _row_tile(s, OUT_ROWS)
    tiles_per_batch = s // tm
    return pl.pallas_call(
        _out_kernel,
        grid=(m // tm,),
        in_specs=[pl.BlockSpec((tm, D_ATT), lambda i: (i, 0)),
                  pl.BlockSpec((tm, D_SSD), lambda i: (i, 0)),
                  pl.BlockSpec((D_ATT, d), lambda i: (0, 0)),
                  pl.BlockSpec((D_SSD, d), lambda i: (D_ATT // D_SSD, 0)),
                  pl.BlockSpec((tm, d), lambda i: (i, 0)),
                  pl.BlockSpec((1, 1, d), lambda i: (i // tiles_per_batch, 0, 0)),
                  pl.BlockSpec((1, d), lambda i: (0, 0))],
        out_specs=pl.BlockSpec((tm, d), lambda i: (i, 0)),
        out_shape=jax.ShapeDtypeStruct((m, d), x2d.dtype),
        compiler_params=_params("out_proj", ("parallel",)),
        name="out_proj",
    )(h_att, h_ssd, w_out_bf, w_out_bf, x2d, gate, final_norm_g)


def _rope_tables(s, pad_rows):
    pos = np.arange(s)
    row = (pos // GRID_W).astype(np.float32)
    colp = (pos % GRID_W).astype(np.float32)
    quarter = HEAD_DIM // 4
    freq = (1.0 / (np.float32(ROPE_THETA) ** (np.arange(quarter, dtype=np.float32) / np.float32(quarter)))
            ).astype(np.float32)
    lane = np.arange(LANE)
    dim = lane % HEAD_DIM
    use_col = (dim // (HEAD_DIM // 2)) == 1
    r = dim % (HEAD_DIM // 2)
    first = r < quarter
    ang = (np.where(use_col[None, :], colp[:, None], row[:, None]) * freq[r % quarter][None, :]).astype(np.float32)
    cos, sin = np.cos(ang), np.sin(ang)
    sa = np.where(first[None, :], -sin, 0.0)
    sb = np.where(first[None, :], 0.0, sin)
    ident = lambda v, fill: jnp.asarray(
        np.concatenate([v, np.full((pad_rows, LANE), fill)], axis=0).astype(np.float32))
    return ident(cos, 1.0), ident(sa, 0.0), ident(sb, 0.0)


def kernel(x, c, ctx, c_ctx, w_mod, b_mod, norm_g, w_in, conv_w, conv_b, a_log_f, a_log_b, dt_bias_f,
           dt_bias_b, d_skip, att_norm_g, ssd_norm_g, sink, w_out, final_norm_g):
    assert w_mod.shape[0] == 1, "single-layer operation"
    b, s, d = x.shape
    nc = ctx.shape[1]

    pad_rows = (-(b + 1)) % SUBLANE
    cc = jnp.concatenate([c, c_ctx[None, :], jnp.zeros((pad_rows, d), F32)], axis=0)
    mod = _modulation(cc, w_mod[0], b_mod)
    shift = mod[:b + 1, :d].reshape(b + 1, 1, d)
    scale = mod[:b + 1, d:2 * d].reshape(b + 1, 1, d)
    gate = mod[:b, 2 * d:].reshape(b, 1, d)

    w_t = jnp.swapaxes(w_in[0], 0, 1).astype(BF16)
    w_dt = jnp.pad(w_t[D_QKV + D_GZX:], ((0, LANE - 2 * SSD_HEADS), (0, 0)))
    w_out_bf = w_out[0].astype(BF16)

    cos_t, sa_t, sb_t = _rope_tables(s, NORM_TILE)
    qkv, xn, dtraw = _norm_qkv(x.reshape(b * s, d), ctx.reshape(b * nc, d), shift, scale, norm_g, w_dt, w_t,
                               cos_t, sa_t, sb_t, s)

    gzx = _gzx_proj(xn, w_t, b * s)

    h_att = _attention(sink[0], qkv, gzx, att_norm_g, b, s, nc)

    lane_pad = LANE - 2 * SSD_HEADS
    dt_bias = jnp.pad(jnp.concatenate([dt_bias_f[0], dt_bias_b[0]]), (0, lane_pad))[None, :]
    a_log = jnp.pad(jnp.concatenate([a_log_f[0], a_log_b[0]]), (0, lane_pad))[None, :]
    d_skip_exp = jnp.repeat(d_skip[0], SSD_HEAD_DIM)[None, :]
    h_ssd = _ssd(gzx, dtraw, conv_w[0], conv_b, dt_bias, a_log, d_skip_exp, ssd_norm_g, b, s, nc)

    out = _out_proj(h_att, h_ssd, w_out_bf, x.reshape(b * s, d), gate, final_norm_g[None, :], s)
    return out.reshape(b, s, d)
```

```python
import functools

import jax
import jax.numpy as jnp
import numpy as np
from jax import lax
from jax.experimental import pallas as pl
from jax.experimental.pallas import tpu as pltpu

F32 = jnp.float32
BF16 = jnp.bfloat16

EPS = 1e-6
GRID_W = 64
ROPE_THETA = 10000.0
ATT_HEADS = 16
ATT_KV_HEADS = 4
HEAD_DIM = 64
D_ATT = ATT_HEADS * HEAD_DIM
D_KV = ATT_KV_HEADS * HEAD_DIM
Q_PER_KV = ATT_HEADS // ATT_KV_HEADS
BLOCK = 128
SSD_HEADS = 16
SSD_HEAD_DIM = 64
D_SSD = SSD_HEADS * SSD_HEAD_DIM
SSD_GROUPS = 4
SSD_HPG = SSD_HEADS // SSD_GROUPS
D_STATE = 128
D_BC = SSD_GROUPS * D_STATE
D_XBC = D_SSD + 2 * D_BC
CONV_W = 3
CHUNK = 128
D_QKV = D_ATT + 2 * D_KV
D_GZX = D_ATT + D_SSD + D_XBC

LANE = 128
SUBLANE = 8
V7X_VMEM_BYTES = 64 * 1024 * 1024
MIB = 1024 * 1024

MOD_COLS = 768
NORM_TILE = 512
GZX_COLS = D_XBC
ATT_QBLOCKS = 4
SSD_CPS = 2
OUT_ROWS = (1024, 512, 256, 128)
OUT_SUB = 256
VMEM_MIB = dict(modulation=32, norm_qkv=56, gzx_proj=56, window_attn=40, bi_ssd=60, out_proj=58)
assert max(VMEM_MIB.values()) * MIB < V7X_VMEM_BYTES

NEG = -1e30
LOG2E = 1.4426950408889634


def _silu(v):
    return v / (1.0 + jnp.exp(-v))


def _dot_nt(a, w):
    return lax.dot_general(a, w, (((1,), (1,)), ((), ())), preferred_element_type=F32)


def _params(name, semantics):
    return pltpu.CompilerParams(dimension_semantics=semantics, vmem_limit_bytes=VMEM_MIB[name] * MIB)


def _row_tile(m, candidates=(1024, 512, 256, 128)):
    for t in candidates:
        if m % t == 0:
            return t
    raise ValueError(f"row count {m} not tileable")


def _mod_kernel(c_ref, w_ref, b_ref, o_ref):
    a = _silu(c_ref[...]).astype(BF16)
    o_ref[...] = jnp.dot(a, w_ref[...].astype(BF16), preferred_element_type=F32) + b_ref[...]


def _modulation(cc, w_mod, b_mod):
    rows, d = cc.shape
    n = w_mod.shape[1]
    tn = MOD_COLS
    assert n % tn == 0
    return pl.pallas_call(
        _mod_kernel,
        grid=(n // tn,),
        in_specs=[pl.BlockSpec((rows, d), lambda j: (0, 0)),
                  pl.BlockSpec((d, tn), lambda j: (0, j)),
                  pl.BlockSpec((1, tn), lambda j: (0, j))],
        out_specs=pl.BlockSpec((rows, tn), lambda j: (0, j)),
        out_shape=jax.ShapeDtypeStruct((rows, n), F32),
        compiler_params=_params("modulation", ("parallel",)),
        name="modulation",
    )(cc, w_mod, b_mod)


def _norm_qkv_kernel(x_ref, ctx_ref, shift_ref, scale_ref, g_ref, wdt_ref, w_ref, cos_ref, sa_ref, sb_ref,
                     qkv_ref, xn_ref, dt_ref, ya_ref, yb_ref, *, n_lat_tiles, n_tiles, n_rope, n_q):
    i = pl.program_id(0)
    is_lat = jnp.minimum(i, n_tiles - 1) < n_lat_tiles

    @pl.when(i == 0)
    def _():
        yb_ref[...] = jnp.zeros_like(yb_ref)

    def step(y_prev_ref, y_next_ref, src_ref):
        y = y_prev_ref[...]
        dt_ref[...] = _dot_nt(y, wdt_ref[...])
        acc = _dot_nt(y, w_ref[...])
        cos, sa, sb = cos_ref[...], sa_ref[...], sb_ref[...]
        quarter = HEAD_DIM // 4
        for jb in range(acc.shape[1] // LANE):
            blk = acc[:, jb * LANE:(jb + 1) * LANE]
            if jb < n_rope:
                blk = (blk * cos + pltpu.roll(blk, LANE - quarter, 1) * sa
                       + pltpu.roll(blk, quarter, 1) * sb)
            if jb < n_q:
                blk = blk * (HEAD_DIM ** -0.5 * LOG2E)
            qkv_ref[:, jb * LANE:(jb + 1) * LANE] = blk.astype(qkv_ref.dtype)

        v = src_ref[...]
        ms = jnp.mean(v * v, axis=-1, keepdims=True)
        gain = g_ref[...] * (1.0 + scale_ref[0])
        yn = (v * lax.rsqrt(ms + EPS) * gain + shift_ref[0]).astype(BF16)
        xn_ref[...] = yn
        y_next_ref[...] = yn

    for parity, (y_prev_ref, y_next_ref) in enumerate(((yb_ref, ya_ref), (ya_ref, yb_ref))):
        for chosen, src_ref in ((is_lat, x_ref), (jnp.logical_not(is_lat), ctx_ref)):
            pl.when((i % 2 == parity) & chosen)(functools.partial(step, y_prev_ref, y_next_ref, src_ref))


def _norm_qkv(x2d, ctx2d, shift, scale, norm_g, w_dt, w_t, cos_t, sa_t, sb_t, s):
    m_lat, d = x2d.shape
    m_ctx = ctx2d.shape[0]
    b = m_lat // s
    tr = NORM_TILE
    assert s % tr == 0 and m_ctx % tr == 0
    tiles_per_seq = s // tr
    n_lat_tiles, n_ctx_tiles = m_lat // tr, m_ctx // tr
    n_tiles = n_lat_tiles + n_ctx_tiles
    cur = lambda i: jnp.minimum(i, n_tiles - 1)
    prev = lambda i: jnp.maximum(i - 1, 0)
    mod_idx = lambda i: (jnp.where(cur(i) < n_lat_tiles, cur(i) // tiles_per_seq, b), 0, 0)
    tab = pl.BlockSpec((tr, LANE), lambda i: (jnp.where(prev(i) < n_lat_tiles, prev(i) % tiles_per_seq,
                                                        tiles_per_seq), 0))
    out_prev = lambda width: pl.BlockSpec((tr, width), lambda i: (prev(i), 0))
    const = lambda shape: pl.BlockSpec(shape, lambda i: tuple(0 for _ in shape))
    return pl.pallas_call(
        functools.partial(_norm_qkv_kernel, n_lat_tiles=n_lat_tiles, n_tiles=n_tiles,
                          n_rope=(D_ATT + D_KV) // LANE, n_q=D_ATT // LANE),
        grid=(n_tiles + 1,),
        in_specs=[pl.BlockSpec((tr, d), lambda i: (jnp.minimum(cur(i), n_lat_tiles - 1), 0)),
                  pl.BlockSpec((tr, d), lambda i: (jnp.maximum(cur(i) - n_lat_tiles, 0), 0)),
                  pl.BlockSpec((1, 1, d), mod_idx),
                  pl.BlockSpec((1, 1, d), mod_idx),
                  const((1, d)), const((LANE, d)), const((D_QKV, d)),
                  tab, tab, tab],
        out_specs=[out_prev(D_QKV),
                   pl.BlockSpec((tr, d), lambda i: (cur(i), 0)),
                   out_prev(LANE)],
        out_shape=[jax.ShapeDtypeStruct((m_lat + m_ctx, D_QKV), BF16),
                   jax.ShapeDtypeStruct((m_lat + m_ctx, d), BF16),
                   jax.ShapeDtypeStruct((m_lat + m_ctx, LANE), F32)],
        scratch_shapes=[pltpu.VMEM((tr, d), BF16), pltpu.VMEM((tr, d), BF16)],
        compiler_params=_params("norm_qkv", ("arbitrary",)),
        name="norm_qkv",
    )(x2d, ctx2d, shift, scale, norm_g, w_dt, w_t, cos_t, sa_t, sb_t)


def _mm_kernel(x_ref, w_ref, o_ref):
    o_ref[...] = _dot_nt(x_ref[...], w_ref[...]).astype(o_ref.dtype)


def _gzx_proj(x2d, w_t, lat_rows):
    m, d = x2d.shape
    tm = _row_tile(m)
    assert D_ATT + D_SSD == GZX_COLS and D_XBC == GZX_COLS and lat_rows % tm == 0 and D_QKV % LANE == 0
    n_lat = lat_rows // tm
    return pl.pallas_call(
        _mm_kernel,
        grid=(n_lat + m // tm,),
        in_specs=[pl.BlockSpec((tm, d), lambda t: (jnp.where(t < n_lat, t, t - n_lat), 0)),
                  pl.BlockSpec((pl.Element(GZX_COLS), pl.Element(d)),
                               lambda t: (pl.multiple_of(D_QKV + jnp.where(t < n_lat, 0, GZX_COLS), LANE), 0))],
        out_specs=pl.BlockSpec((tm, GZX_COLS), lambda t: (t, 0)),
        out_shape=jax.ShapeDtypeStruct((lat_rows + m, GZX_COLS), F32),
        compiler_params=_params("gzx_proj", ("parallel",)),
        name="gzx_proj",
    )(x2d, w_t)


def _attn_kernel(sink_ref, q_ref, kvp_ref, kvo_ref, kvn_ref, kvc_ref, g_ref, ng_ref, o_ref, *, n_steps):
    n = pl.program_id(1)
    nctx = kvc_ref.shape[0]
    nk = 3 * BLOCK + nctx
    cols_g = Q_PER_KV * BLOCK
    kcols, vcols = slice(0, D_KV), slice(D_KV, 2 * D_KV)
    own = [slice(u * BLOCK, (u + 1) * BLOCK) for u in range(ATT_QBLOCKS)]
    k_blocks = [kvp_ref[:, kcols]] + [kvo_ref[rows, kcols] for rows in own] + [kvn_ref[:, kcols]]
    v_blocks = [kvp_ref[:, vcols]] + [kvo_ref[rows, vcols] for rows in own] + [kvn_ref[:, vcols]]

    kj = lax.broadcasted_iota(jnp.int32, (BLOCK, BLOCK), 0)
    qi = lax.broadcasted_iota(jnp.int32, (BLOCK, BLOCK), 1)
    tile = lambda v: jnp.concatenate([v] * Q_PER_KV, axis=1)
    tri_prev = tile(jnp.where(kj >= qi, 0.0, NEG).astype(F32))
    tri_next = tile(jnp.where(kj <= qi, 0.0, NEG).astype(F32))
    edge_prev = tile(jnp.where((kj >= qi) & (n > 0), 0.0, NEG).astype(F32))
    edge_next = tile(jnp.where((kj <= qi) & (n < n_steps - 1), 0.0, NEG).astype(F32))
    head_of_col = lax.broadcasted_iota(jnp.int32, (1, cols_g), 1) // BLOCK
    n_loc = (ATT_QBLOCKS + 2) * BLOCK
    v_t = jnp.concatenate(v_blocks + [kvc_ref[:, vcols]], axis=0).T
    tail_rows = 2 * SUBLANE
    ones_tail = jnp.where(lax.broadcasted_iota(jnp.int32, (tail_rows, nk), 0) == 0, 1.0, 0.0).astype(BF16)

    def scores_t(u, kh):
        hs = slice(kh * HEAD_DIM, (kh + 1) * HEAD_DIM)
        k_h = jnp.concatenate([blk[:, hs] for blk in k_blocks[u:u + 3]] + [kvc_ref[:, hs]], axis=0)
        qg = jnp.concatenate(
            [q_ref[u * BLOCK:(u + 1) * BLOCK, (kh * Q_PER_KV + i) * HEAD_DIM:(kh * Q_PER_KV + i + 1) * HEAD_DIM]
             for i in range(Q_PER_KV)], axis=0)
        return lax.dot_general(k_h, qg, (((1,), (1,)), ((), ())), preferred_element_type=F32)

    def attend(u, kh, st):
        bias_prev = edge_prev if u == 0 else tri_prev
        bias_next = edge_next if u == ATT_QBLOCKS - 1 else tri_next
        st = jnp.concatenate([st[:BLOCK] + bias_prev, st[BLOCK:2 * BLOCK],
                              st[2 * BLOCK:3 * BLOCK] + bias_next, st[3 * BLOCK:]], axis=0)
        sink_row = jnp.zeros((1, cols_g), F32)
        for i in range(Q_PER_KV):
            sink_row = jnp.where(head_of_col == i, sink_ref[kh * Q_PER_KV + i] * LOG2E, sink_row)
        m = jnp.maximum(jnp.max(st, axis=0, keepdims=True), sink_row)
        pt = jnp.exp2(st - m).astype(BF16)
        hs = slice(kh * HEAD_DIM, (kh + 1) * HEAD_DIM)
        vt_h = jnp.concatenate([v_t[hs, u * BLOCK:(u + 3) * BLOCK], v_t[hs, n_loc:]], axis=1)
        ot = jnp.dot(jnp.concatenate([vt_h, ones_tail], axis=0), pt, preferred_element_type=F32)
        inv = 1.0 / (ot[HEAD_DIM:HEAD_DIM + 1] + jnp.exp2(sink_row - m))
        on = ot[:HEAD_DIM] * inv
        pairs = [jnp.concatenate([on[:, (2 * j) * BLOCK:(2 * j + 1) * BLOCK],
                                  on[:, (2 * j + 1) * BLOCK:(2 * j + 2) * BLOCK]], axis=0).T
                 for j in range(Q_PER_KV // 2)]
        return jnp.concatenate(pairs, axis=1)

    def finish(u, groups):
        rows = slice(u * BLOCK, (u + 1) * BLOCK)
        a = jnp.concatenate(groups, axis=1) * _silu(g_ref[rows])
        ms = jnp.mean(a * a, axis=-1, keepdims=True)
        o_ref[rows] = (a * lax.rsqrt(ms + EPS) * ng_ref[...]).astype(o_ref.dtype)

    order = [(u, kh) for u in range(ATT_QBLOCKS) for kh in range(ATT_KV_HEADS)]
    ahead = 1
    pending = [scores_t(*order[i]) for i in range(ahead)]
    groups = []
    for idx, (u, kh) in enumerate(order):
        s_cur = pending.pop(0)
        if idx + ahead < len(order):
            pending.append(scores_t(*order[idx + ahead]))
        groups.append(attend(u, kh, s_cur))
        if kh == ATT_KV_HEADS - 1:
            finish(u, groups)
            groups = []


def _attention(sink, qkv, gzx, att_norm_g, b, s, nc):
    rows = ATT_QBLOCKS * BLOCK
    assert s % rows == 0 and (b * s) % nc == 0
    nb = s // BLOCK
    n_steps = s // rows
    kv_w = 2 * D_KV
    assert D_ATT % kv_w == 0
    kv_col = D_ATT // kv_w
    before = pl.BlockSpec((BLOCK, kv_w), lambda bi, n: (bi * nb + jnp.maximum(ATT_QBLOCKS * n - 1, 0), kv_col))
    own = pl.BlockSpec((rows, kv_w), lambda bi, n: (bi * n_steps + n, kv_col))
    after = pl.BlockSpec((BLOCK, kv_w),
                         lambda bi, n: (bi * nb + jnp.minimum(ATT_QBLOCKS * (n + 1), nb - 1), kv_col))
    ctx_spec = pl.BlockSpec((nc, kv_w), lambda bi, n: (b * s // nc + bi, kv_col))
    q_rows = pl.BlockSpec((rows, D_ATT), lambda bi, n: (bi * n_steps + n, 0))
    return pl.pallas_call(
        functools.partial(_attn_kernel, n_steps=n_steps),
        grid=(b, n_steps),
        in_specs=[pl.BlockSpec(memory_space=pltpu.SMEM),
                  q_rows,
                  before, own, after, ctx_spec,
                  q_rows,
                  pl.BlockSpec((1, D_ATT), lambda bi, n: (0, 0))],
        out_specs=q_rows,
        out_shape=jax.ShapeDtypeStruct((b * s, D_ATT), BF16),
        compiler_params=_params("window_attn", ("parallel", "parallel")),
        name="window_attn",
    )(sink, qkv, qkv, qkv, qkv, qkv, gzx, att_norm_g)


def _ssd_visit_index(t, n_lat, n_ctx):
    n_all = n_lat + n_ctx
    k = t - n_all
    fwd = jnp.where(k < n_ctx, n_lat + k, k - n_ctx)
    return jnp.where(t >= n_all, fwd, n_all - 1 - t)


def _ssd_kernel(xbc_ref, hp_ref, hn_ref, dt_ref, z_ref, cw_ref, cb_ref, dtb_ref, alog_ref, dsk_ref,
                ng_ref, ex_ref, o_ref, sf_ref, sb_ref, sin_ref, ux_ref, ubc_ref, lhs_ref, pack_ref, y_ref, pad_ref,
                *, n_lat, n_ctx):
    t = pl.program_id(1)
    n_all = n_lat + n_ctx
    g_lat, g_ctx = n_lat // SSD_CPS, n_ctx // SSD_CPS
    g_all = g_lat + g_ctx
    gidx = _ssd_visit_index(t, g_lat, g_ctx)
    c_base = gidx * SSD_CPS
    sweep1 = t >= g_all
    is_lat = gidx < g_lat
    L = CHUNK
    gw = SSD_HPG * SSD_HEAD_DIM
    lane = lax.broadcasted_iota(jnp.int32, (1, LANE), 1)
    rows = lax.broadcasted_iota(jnp.int32, (L, 1), 0)
    groups = range(SSD_GROUPS)
    gsl = [slice(g * gw, (g + 1) * gw) for g in groups]

    @pl.when(t == 0)
    def _():
        sf_ref[...] = jnp.zeros_like(sf_ref)
        sb_ref[...] = jnp.zeros_like(sb_ref)
        ux_ref[n_all - 1] = jnp.zeros((CHUNK, D_SSD), F32)
        ubc_ref[n_all - 1] = jnp.zeros((CHUNK, 2 * D_BC), BF16)
        lhs_ref[n_all - 1] = jnp.zeros((CHUNK, 2 * LANE), BF16)

    def expand(lhs, block, g):
        c0 = block * D_SSD + g * gw
        return jnp.dot(lhs, ex_ref[:, c0:c0 + gw], preferred_element_type=F32)

    def update_state(state_ref, chunk, lhs_ex, g, wblock, eblock, end_row):
        r8 = (end_row // SUBLANE) * SUBLANE
        xw = (ux_ref[chunk, :, gsl[g]] * expand(lhs_ex, wblock, g)).astype(BF16)
        bg = ubc_ref[chunk, :, g * D_STATE:(g + 1) * D_STATE]
        loc = lax.dot_general(bg, xw, (((0,), (0,)), ((), ())), preferred_element_type=F32)
        tot = expand(lhs_ex[r8:r8 + SUBLANE], eblock, g)[end_row - r8:end_row - r8 + 1]
        new = state_ref[:, gsl[g]] * tot + loc
        state_ref[:, gsl[g]] = new
        return new

    @pl.when(jnp.logical_not(sweep1))
    def _():
        is_first = (gidx == 0) | (gidx == g_lat)
        is_last = (gidx == g_lat - 1) | (gidx == g_all - 1)
        span = SSD_CPS * L

        for k in range(D_XBC // LANE):
            lanes = slice(k * LANE, (k + 1) * LANE)
            pad_ref[k, SUBLANE:SUBLANE + span, :] = xbc_ref[:, lanes]
            pad_ref[k, SUBLANE - 1:SUBLANE, :] = jnp.where(is_first, 0.0, hp_ref[SUBLANE - 1:SUBLANE, lanes])
            pad_ref[k, SUBLANE + span:SUBLANE + span + 1, :] = jnp.where(is_last, 0.0, hn_ref[0:1, lanes])

        def conv_strip(u, j):
            r0 = SUBLANE + u * L
            for k in range(j * gw // LANE, (j + 1) * gw // LANE):
                lanes = slice(k * LANE, (k + 1) * LANE)
                v = (pad_ref[k, r0 - 1:r0 - 1 + L, :] * cw_ref[0:1, lanes] + pad_ref[k, r0:r0 + L, :] * cw_ref[1:2, lanes]
                     + pad_ref[k, r0 + 1:r0 + 1 + L, :] * cw_ref[2:3, lanes] + cb_ref[:, lanes])
                v = _silu(v)
                if k * LANE < D_SSD:
                    ux_ref[c_base + u, :, lanes] = v
                else:
                    ubc_ref[c_base + u, :, k * LANE - D_SSD:(k + 1) * LANE - D_SSD] = v.astype(BF16)

        def decay_tables(u):
            dtr = dt_ref[u * L:(u + 1) * L, :] + dtb_ref[...]
            dt = jnp.maximum(dtr, 0.0) + jnp.log(1.0 + jnp.exp(-jnp.abs(dtr)))
            a = dt * (-jnp.exp(alog_ref[...]))
            pre, suf = a, a
            d = 1
            while d < L:
                pre = pre + jnp.where(rows >= d, pltpu.roll(pre, d, 0), 0.0)
                suf = suf + jnp.where(rows < L - d, pltpu.roll(suf, L - d, 0), 0.0)
                d *= 2
            is_fwd_lane = lane < SSD_HEADS
            cs = jnp.where(is_fwd_lane, pre, suf)
            tot = jnp.where(is_fwd_lane, cs[L - 1:L, :], cs[0:1, :])
            e = jnp.exp(cs)
            wdt = jnp.exp(tot - cs) * dt
            pack_ref[c_base + u] = jnp.where(lane < 2 * SSD_HEADS, cs * LOG2E, pltpu.roll(dt, 2 * SSD_HEADS, 1))
            fac = jnp.where(lane < 2 * SSD_HEADS, wdt,
                            jnp.where(lane < 4 * SSD_HEADS, pltpu.roll(e, 2 * SSD_HEADS, 1), 0.0))
            hi = fac.astype(BF16).astype(F32)
            r1 = fac - hi
            mid = r1.astype(BF16).astype(F32)
            lo = (r1 - mid).astype(BF16)
            lhs_ref[c_base + u] = jnp.concatenate([(hi + pltpu.roll(mid, LANE // 2, 1)).astype(BF16), lo], axis=1)

        for u in reversed(range(SSD_CPS)):
            cidx = c_base + u
            done = jnp.minimum(cidx + 1, n_all - 1)
            lhs_done = lhs_ref[done]
            wexp = [expand(lhs_done, 1, g) for g in groups]
            tots = [expand(lhs_done[0:SUBLANE], 3, g)[0:1] for g in groups]
            conv_strip(u, 0)
            conv_strip(u, 1)
            xws = [(ux_ref[done, :, gsl[g]] * wexp[g]).astype(BF16) for g in groups]
            locs = [lax.dot_general(ubc_ref[done, :, g * D_STATE:(g + 1) * D_STATE], xws[g],
                                    (((0,), (0,)), ((), ())), preferred_element_type=F32) for g in groups]
            conv_strip(u, 2)
            conv_strip(u, 3)
            conv_strip(u, 4)
            for g in groups:
                new = sb_ref[:, gsl[g]] * tots[g] + locs[g]
                sb_ref[:, gsl[g]] = new
                sin_ref[jnp.minimum(cidx, n_lat - 1), :, gsl[g]] = new.astype(BF16)
            conv_strip(u, 5)
            conv_strip(u, 6)
            conv_strip(u, 7)
            decay_tables(u)

    @pl.when(sweep1 & jnp.logical_not(is_lat))
    def _():
        for u in range(SSD_CPS):
            lhs_ex = lhs_ref[c_base + u]
            for g in groups:
                update_state(sf_ref, c_base + u, lhs_ex, g, 0, 2, L - 1)

    @pl.when(sweep1 & is_lat)
    def _():
        li = lax.broadcasted_iota(jnp.int32, (L, L), 0)
        si = lax.broadcasted_iota(jnp.int32, (L, L), 1)
        masks = [si <= li, si >= li]
        head_of_lane = lax.broadcasted_iota(jnp.int32, (1, gw), 1) // SSD_HEAD_DIM
        lhs_exs = [lhs_ref[c_base + u] for u in range(SSD_CPS)]
        packs = [pack_ref[c_base + u] for u in range(SSD_CPS)]
        pack_ts = [p.T for p in packs]

        def early(u, g):
            cidx = c_base + u
            bg = ubc_ref[cidx, :, g * D_STATE:(g + 1) * D_STATE]
            cg = ubc_ref[cidx, :, D_BC + g * D_STATE:D_BC + (g + 1) * D_STATE]
            cb = lax.dot_general(cg, bg, (((1,), (1,)), ((), ())), preferred_element_type=F32)
            cbm = [jnp.where(m, cb, 0.0) for m in masks]
            y_off = (jnp.dot(cg, sf_ref[:, gsl[g]].astype(BF16), preferred_element_type=F32)
                     * expand(lhs_exs[u], 2, g)
                     + jnp.dot(cg, sin_ref[cidx, :, gsl[g]], preferred_element_type=F32)
                     * expand(lhs_exs[u], 3, g))
            update_state(sf_ref, cidx, lhs_exs[u], g, 0, 2, L - 1)
            return cbm, y_off

        def main(u, g, cbm):
            xg = ux_ref[c_base + u, :, gsl[g]].astype(BF16)
            blockdiag = jnp.concatenate(
                [jnp.where(head_of_lane == hh, xg, jnp.zeros_like(xg)) for hh in range(SSD_HPG)], axis=0)
            y_diag = None
            for dirn in range(2):
                ws = []
                for hh in range(SSD_HPG):
                    ln = dirn * SSD_HEADS + g * SSD_HPG + hh
                    col = packs[u][:, ln:ln + 1]
                    row = pack_ts[u][ln:ln + 1, :]
                    dtrow = pack_ts[u][2 * SSD_HEADS + ln:2 * SSD_HEADS + ln + 1, :]
                    dec = jnp.exp2(jnp.minimum(col - row, 0.0))
                    ws.append((cbm[dirn] * dec * dtrow).astype(BF16))
                yd = jnp.dot(jnp.concatenate(ws, axis=1), blockdiag, preferred_element_type=F32)
                y_diag = yd if y_diag is None else y_diag + yd
            return y_diag

        def finish(u, g, y_off, y_diag):
            yg = ux_ref[c_base + u, :, gsl[g]] * dsk_ref[:, gsl[g]] + y_diag + y_off
            yz = yg * _silu(z_ref[u * L:(u + 1) * L, gsl[g]])
            y_ref[u * L:(u + 1) * L, gsl[g]] = yz
            return jnp.sum(yz * yz, axis=-1, keepdims=True)

        units = [(u, g) for u in range(SSD_CPS) for g in groups]
        early_ahead = 4
        pending = {units[i]: early(*units[i]) for i in range(early_ahead)}
        diag = {}
        ss = [jnp.zeros((L, 1), F32) for _ in range(SSD_CPS)]
        for i, (u, g) in enumerate(units):
            if i + early_ahead < len(units):
                pending[units[i + early_ahead]] = early(*units[i + early_ahead])
            diag[(u, g)] = main(u, g, pending[(u, g)][0])
            if i >= 1:
                pu, pg = units[i - 1]
                ss[pu] = ss[pu] + finish(pu, pg, pending[(pu, pg)][1], diag[(pu, pg)])
        pu, pg = units[-1]
        ss[pu] = ss[pu] + finish(pu, pg, pending[(pu, pg)][1], diag[(pu, pg)])
        for u in range(SSD_CPS):
            rs = lax.rsqrt(ss[u] * (1.0 / D_SSD) + EPS)
            for g in groups:
                o_ref[u * L:(u + 1) * L, gsl[g]] = (
                    y_ref[u * L:(u + 1) * L, gsl[g]] * rs * ng_ref[:, gsl[g]]).astype(o_ref.dtype)


def _expansion_matrix():
    r = jnp.arange(2 * LANE)[:, None]
    col = jnp.arange(4 * D_SSD)[None, :]
    src = (col // D_SSD) * SSD_HEADS + (col % D_SSD) // SSD_HEAD_DIM
    half = LANE // 2
    return ((r % half == src) & (r < 3 * half)).astype(BF16)


def _ssd(gzx, dtraw, conv_w, conv_b, dt_bias, a_log, d_skip_exp, ssd_norm_g, b, s, nc):
    span = SSD_CPS * CHUNK
    assert s % span == 0 and nc % span == 0
    n_lat, n_ctx = s // CHUNK, nc // CHUNK
    g_lat, g_ctx = s // span, nc // span
    g_all = g_lat + g_ctx
    halo_per_span = span // SUBLANE
    n_halo = gzx.shape[0] // SUBLANE
    xbc0 = b * g_lat
    c0 = lambda t: jnp.where(t < g_all, _ssd_visit_index(t, g_lat, g_ctx), 0)
    c1 = lambda t: jnp.where(t >= g_all + g_ctx, t - g_all - g_ctx, 0)
    blk = lambda bi, c: jnp.where(c < g_lat, bi * g_lat + c, b * g_lat + bi * g_ctx + (c - g_lat))
    xblk = lambda bi, t: xbc0 + blk(bi, c0(t))
    z_col = D_ATT // D_SSD
    const = lambda shape: pl.BlockSpec(shape, lambda bi, t: tuple(0 for _ in shape))
    return pl.pallas_call(
        functools.partial(_ssd_kernel, n_lat=n_lat, n_ctx=n_ctx),
        grid=(b, 2 * g_all),
        in_specs=[pl.BlockSpec((span, D_XBC), lambda bi, t: (xblk(bi, t), 0)),
                  pl.BlockSpec((SUBLANE, D_XBC), lambda bi, t: (xblk(bi, t) * halo_per_span - 1, 0)),
                  pl.BlockSpec((SUBLANE, D_XBC),
                               lambda bi, t: (jnp.minimum((xblk(bi, t) + 1) * halo_per_span, n_halo - 1), 0)),
                  pl.BlockSpec((span, LANE), lambda bi, t: (blk(bi, c0(t)), 0)),
                  pl.BlockSpec((span, D_SSD), lambda bi, t: (bi * g_lat + c1(t), z_col)),
                  const((CONV_W, D_XBC)), const((1, D_XBC)), const((1, LANE)), const((1, LANE)),
                  const((1, D_SSD)), const((1, D_SSD)), const((2 * LANE, 4 * D_SSD))],
        out_specs=pl.BlockSpec((span, D_SSD), lambda bi, t: (bi * g_lat + c1(t), 0)),
        out_shape=jax.ShapeDtypeStruct((b * s, D_SSD), BF16),
        scratch_shapes=[pltpu.VMEM((D_STATE, D_SSD), F32),
                        pltpu.VMEM((D_STATE, D_SSD), F32),
                        pltpu.VMEM((n_lat, D_STATE, D_SSD), BF16),
                        pltpu.VMEM((n_lat + n_ctx, CHUNK, D_SSD), F32),
                        pltpu.VMEM((n_lat + n_ctx, CHUNK, 2 * D_BC), BF16),
                        pltpu.VMEM((n_lat + n_ctx, CHUNK, 2 * LANE), BF16),
                        pltpu.VMEM((n_lat + n_ctx, CHUNK, LANE), F32),
                        pltpu.VMEM((span, D_SSD), F32),
                        pltpu.VMEM((D_XBC // LANE, span + 2 * SUBLANE, LANE), F32)],
        compiler_params=_params("bi_ssd", ("arbitrary", "arbitrary")),
        name="bi_ssd",
    )(gzx, gzx, gzx, dtraw, gzx, conv_w, conv_b, dt_bias, a_log, d_skip_exp, ssd_norm_g,
      _expansion_matrix())


def _out_kernel(ha_ref, hs_ref, w_ref, x_ref, gate_ref, ng_ref, o_ref):
    sub = min(OUT_SUB, ha_ref.shape[0])
    n_sub = ha_ref.shape[0] // sub

    def project(r):
        rows = slice(r * sub, (r + 1) * sub)
        h = jnp.concatenate([ha_ref[rows, :], hs_ref[rows, :]], axis=1)
        return jnp.dot(h, w_ref[...], preferred_element_type=F32)

    def finish(r, upd):
        rows = slice(r * sub, (r + 1) * sub)
        y = x_ref[rows, :] + gate_ref[0] * upd
        ms = jnp.mean(y * y, axis=-1, keepdims=True)
        o_ref[rows, :] = (y * lax.rsqrt(ms + EPS) * ng_ref[...]).astype(o_ref.dtype)

    upd = project(0)
    for r in range(n_sub):
        nxt = project(r + 1) if r + 1 < n_sub else None
        finish(r, upd)
        upd = nxt


def _out_proj(h_att, h_ssd, w_out_bf, x2d, gate, final_norm_g, s):
    m, d = x2d.shape
    tm = _row_tile(s, OUT_ROWS)
    tiles_per_batch = s // tm
    return pl.pallas_call(
        _out_kernel,
        grid=(m // tm,),
        in_specs=[pl.BlockSpec((tm, D_ATT), lambda i: (i, 0)),
                  pl.BlockSpec((tm, D_SSD), lambda i: (i, 0)),
                  pl.BlockSpec((D_ATT + D_SSD, d), lambda i: (0, 0), pipeline_mode=pl.Buffered(1)),
                  pl.BlockSpec((tm, d), lambda i: (i, 0)),
                  pl.BlockSpec((1, 1, d), lambda i: (i // tiles_per_batch, 0, 0)),
                  pl.BlockSpec((1, d), lambda i: (0, 0))],
        out_specs=pl.BlockSpec((tm, d), lambda i: (i, 0)),
        out_shape=jax.ShapeDtypeStruct((m, d), x2d.dtype),
        compiler_params=_params("out_proj", ("parallel",)),
        name="out_proj",
    )(h_att, h_ssd, w_out_bf, x2d, gate, final_norm_g)


def _rope_tables(s, pad_rows):
    pos = np.arange(s)
    row = (pos // GRID_W).astype(np.float32)
    colp = (pos % GRID_W).astype(np.float32)
    quarter = HEAD_DIM // 4
    freq = (1.0 / (np.float32(ROPE_THETA) ** (np.arange(quarter, dtype=np.float32) / np.float32(quarter)))
            ).astype(np.float32)
    lane = np.arange(LANE)
    dim = lane % HEAD_DIM
    use_col = (dim // (HEAD_DIM // 2)) == 1
    r = dim % (HEAD_DIM // 2)
    first = r < quarter
    ang = (np.where(use_col[None, :], colp[:, None], row[:, None]) * freq[r % quarter][None, :]).astype(np.float32)
    cos, sin = np.cos(ang), np.sin(ang)
    sa = np.where(first[None, :], -sin, 0.0)
    sb = np.where(first[None, :], 0.0, sin)
    ident = lambda v, fill: jnp.asarray(
        np.concatenate([v, np.full((pad_rows, LANE), fill)], axis=0).astype(np.float32))
    return ident(cos, 1.0), ident(sa, 0.0), ident(sb, 0.0)


def kernel(x, c, ctx, c_ctx, w_mod, b_mod, norm_g, w_in, conv_w, conv_b, a_log_f, a_log_b, dt_bias_f,
           dt_bias_b, d_skip, att_norm_g, ssd_norm_g, sink, w_out, final_norm_g):
    assert w_mod.shape[0] == 1, "single-layer operation"
    b, s, d = x.shape
    nc = ctx.shape[1]

    pad_rows = (-(b + 1)) % SUBLANE
    cc = jnp.concatenate([c, c_ctx[None, :], jnp.zeros((pad_rows, d), F32)], axis=0)
    mod = _modulation(cc, w_mod[0], b_mod)
    shift = mod[:b + 1, :d].reshape(b + 1, 1, d)
    scale = mod[:b + 1, d:2 * d].reshape(b + 1, 1, d)
    gate = mod[:b, 2 * d:].reshape(b, 1, d)

    w_t = jnp.swapaxes(w_in[0], 0, 1).astype(BF16)
    w_dt = jnp.pad(w_t[D_QKV + D_GZX:], ((0, LANE - 2 * SSD_HEADS), (0, 0)))
    w_out_bf = w_out[0].astype(BF16)

    cos_t, sa_t, sb_t = _rope_tables(s, NORM_TILE)
    qkv, xn, dtraw = _norm_qkv(x.reshape(b * s, d), ctx.reshape(b * nc, d), shift, scale, norm_g, w_dt, w_t,
                               cos_t, sa_t, sb_t, s)

    gzx = _gzx_proj(xn, w_t, b * s)

    h_att = _attention(sink[0], qkv, gzx, att_norm_g, b, s, nc)

    lane_pad = LANE - 2 * SSD_HEADS
    dt_bias = jnp.pad(jnp.concatenate([dt_bias_f[0], dt_bias_b[0]]), (0, lane_pad))[None, :]
    a_log = jnp.pad(jnp.concatenate([a_log_f[0], a_log_b[0]]), (0, lane_pad))[None, :]
    d_skip_exp = jnp.repeat(d_skip[0], SSD_HEAD_DIM)[None, :]
    h_ssd = _ssd(gzx, dtraw, conv_w[0], conv_b, dt_bias, a_log, d_skip_exp, ssd_norm_g, b, s, nc)

    out = _out_proj(h_att, h_ssd, w_out_bf, x.reshape(b * s, d), gate, final_norm_g[None, :], s)
    return out.reshape(b, s, d)
```
